```python
import jax
import jax.numpy as jnp
from jax import lax
import numpy as np

D_MODEL = 1024
BATCH = 2
SEQ = 16384
DEPTH = 4
DEC_BATCH = 8
DEC_SEQ = 64
PAST_LEN = 1024

CHUNK = 64
MIX_WIDTH = D_MODEL
LRU_WIDTH = MIX_WIDTH // 2
LRU_HEADS = 8
LRU_HEAD_DIM = LRU_WIDTH // LRU_HEADS
CONV_WIDTH = 4
RG_C = 8.0
SB_WIDTH = MIX_WIDTH - LRU_WIDTH
SB_HEADS = 4
SB_HEAD_DIM = SB_WIDTH // SB_HEADS
SB_SCALE = SB_HEAD_DIM ** -0.5
SB_BLOCK = 128
N_EXPERTS = 32
TOP_K = 4
D_FF = D_MODEL // 2
SWIGLU_LIMIT = 7.0
SWIGLU_ALPHA = 1.702
MOE_BLOCK = 256
LN_EPS = 1e-5
DEEPNORM_ALPHA = (2 * DEPTH) ** 0.25
DEEPNORM_BETA = (8 * DEPTH) ** -0.25
IN_COLS = 2 * LRU_WIDTH + 3 * SB_WIDTH
IN_SPLITS = [LRU_WIDTH, 2 * LRU_WIDTH, 2 * LRU_WIDTH + SB_WIDTH, 2 * LRU_WIDTH + 2 * SB_WIDTH]

kernel_name = 'hybrid_rglru_stickbreak_moe_stream'


def layer_norm(x, g, b):
    xf = x.astype(jnp.float32)
    mu = jnp.mean(xf, axis=-1, keepdims=True)
    var = jnp.mean(jnp.square(xf - mu), axis=-1, keepdims=True)
    return ((xf - mu) * lax.rsqrt(var + LN_EPS) * g + b).astype(x.dtype)


def adaln(c, w, b):
    mod = (jax.nn.silu(c) @ w + b).reshape(c.shape[0], 6, 1, D_MODEL)
    return [mod[:, j] for j in range(6)]


def modulate(x, shift, scale):
    return x * (1.0 + scale) + shift


def _lin_combine(e1, e2):
    a1, b1 = e1
    a2, b2 = e2
    return a1 * a2, a2 * b1 + b2


def _strict_upper(n):
    r = jnp.arange(n)
    return (r[:, None] > r[None, :]).astype(jnp.float32)


def stick_breaking(q, k, v, q_pos, k_pos):
    B, K, H, Dh = k.shape
    nk = K // SB_BLOCK
    kb = k.reshape(B, nk, SB_BLOCK, H, Dh).astype(jnp.float32)
    vb = v.reshape(B, nk, SB_BLOCK, H, Dh).astype(jnp.float32)
    z = jnp.einsum('bqhd,bjshd->bhqjs', q.astype(jnp.float32), kb) * SB_SCALE
    mask = k_pos.reshape(nk, SB_BLOCK)[None] < q_pos[:, None, None]
    log_keep = jnp.where(mask, jax.nn.log_sigmoid(-z), 0.0)
    tail_in = jnp.einsum('bhqjs,su->bhqju', log_keep, _strict_upper(SB_BLOCK))
    off = jnp.einsum('bhqj,jk->bhqk', jnp.sum(log_keep, axis=-1), _strict_upper(nk))
    w = jnp.where(mask, jnp.exp(jax.nn.log_sigmoid(z) + tail_in + off[..., None]), 0.0)
    return jnp.einsum('bhqjs,bjshd->bqhd', w, vb).astype(q.dtype)


def stick_breaking_prompt(q, k, v):
    S = q.shape[1]
    outs = []
    for i in range(S // SB_BLOCK):
        lo, hi = i * SB_BLOCK, (i + 1) * SB_BLOCK
        outs.append(stick_breaking(q[:, lo:hi], k[:, :hi], v[:, :hi],
                                   lo + jnp.arange(SB_BLOCK), jnp.arange(hi)))
    return jnp.concatenate(outs, axis=1)


def token_mixer(h, conv_state, h0, k_past, v_past, w_in, conv_w, conv_b,
                gate_a_w, gate_a_b, gate_x_w, gate_x_b, lru_lambda, w_out):
    B, S, _ = h.shape
    xr, yr, q, k, v = jnp.split(h @ w_in, IN_SPLITS, axis=-1)
    xpad = jnp.concatenate([conv_state.astype(xr.dtype), xr], axis=1)
    xc = conv_b + sum(xpad[:, j:j + S] * conv_w[j] for j in range(CONV_WIDTH))
    new_conv = xpad[:, S:]
    xh = xc.reshape(B, S, LRU_HEADS, LRU_HEAD_DIM)
    r = jax.nn.sigmoid(jnp.einsum('bshi,hij->bshj', xh, gate_a_w).reshape(B, S, LRU_WIDTH) + gate_a_b)
    i = jax.nn.sigmoid(jnp.einsum('bshi,hij->bshj', xh, gate_x_w).reshape(B, S, LRU_WIDTH) + gate_x_b)
    log_a = RG_C * r.astype(jnp.float32) * jax.nn.log_sigmoid(lru_lambda.astype(jnp.float32))
    a = jnp.exp(log_a)
    b = jnp.sqrt(-jnp.expm1(2.0 * log_a)) * (i * xc).astype(jnp.float32)
    a_cum, b_cum = lax.associative_scan(_lin_combine, (a, b), axis=1)
    hs = b_cum + a_cum * h0.astype(jnp.float32)[:, None]
    lru_out = hs.astype(h.dtype) * jax.nn.gelu(yr)
    new_h = hs[:, -1].astype(h.dtype)
    q = q.reshape(B, S, SB_HEADS, SB_HEAD_DIM)
    k = k.reshape(B, S, SB_HEADS, SB_HEAD_DIM)
    v = v.reshape(B, S, SB_HEADS, SB_HEAD_DIM)
    if k_past is None:
        attn = stick_breaking_prompt(q, k, v)
    else:
        P = k_past.shape[1]
        pad = (-(P + S)) % SB_BLOCK
        zpad = jnp.zeros((B, pad, SB_HEADS, SB_HEAD_DIM), k.dtype)
        k_all = jnp.concatenate([k_past.astype(k.dtype), k, zpad], axis=1)
        v_all = jnp.concatenate([v_past.astype(v.dtype), v, zpad], axis=1)
        attn = stick_breaking(q, k_all, v_all, P + jnp.arange(S), jnp.arange(P + S + pad))
    out = jnp.concatenate([lru_out, attn.reshape(B, S, SB_WIDTH)], axis=-1) @ w_out
    return out, k, v, new_conv, new_h


def moe(x2, router_w, router_b, w_gu, b_gu, w_down, b_down):
    T, D = x2.shape
    TK = T * TOP_K
    logits = (x2 @ router_w).astype(jnp.float32) + router_b.astype(jnp.float32)
    top_val, top_idx = lax.top_k(logits, TOP_K)
    gates = jax.nn.softmax(top_val, axis=-1).astype(x2.dtype)
    flat_e = top_idx.reshape(TK)
    order = jnp.argsort(flat_e, stable=True)
    sorted_e = flat_e[order]
    counts = jnp.bincount(flat_e, length=N_EXPERTS)
    padded = (counts + MOE_BLOCK - 1) // MOE_BLOCK * MOE_BLOCK
    pad_end = jnp.cumsum(padded)
    pad_start = pad_end - padded
    grp_start = jnp.cumsum(counts) - counts
    dest_sorted = pad_start[sorted_e] + jnp.arange(TK) - grp_start[sorted_e]
    n_blocks = -(-(TK + N_EXPERTS * (MOE_BLOCK - 1)) // MOE_BLOCK)
    rows = n_blocks * MOE_BLOCK
    row_tok = jnp.full((rows,), T, jnp.int32).at[dest_sorted].set((order // TOP_K).astype(jnp.int32))
    block_e = jnp.minimum(jnp.searchsorted(pad_end, jnp.arange(n_blocks) * MOE_BLOCK, side='right'), N_EXPERTS - 1)
    x_rows = jnp.concatenate([x2, jnp.zeros((1, D), x2.dtype)], axis=0)[row_tok].reshape(n_blocks, MOE_BLOCK, D)

    def expert_block(args):
        xb, e = args
        gu = xb @ w_gu[e] + b_gu[e]
        gate = jnp.minimum(gu[:, :D_FF], SWIGLU_LIMIT)
        up = jnp.clip(gu[:, D_FF:], -SWIGLU_LIMIT, SWIGLU_LIMIT)
        act = (up + 1.0) * gate * jax.nn.sigmoid(SWIGLU_ALPHA * gate)
        return act @ w_down[e] + b_down[e]

    y_rows = lax.map(expert_block, (x_rows, block_e)).reshape(rows, D)
    dest = jnp.zeros((TK,), jnp.int32).at[order].set(dest_sorted.astype(jnp.int32))
    return jnp.einsum('tkd,tk->td', y_rows[dest].reshape(T, TOP_K, D), gates)


def setup_inputs(seed: int = 0) -> dict:
    key = jax.random.key(seed)
    ks = jax.random.split(key, 32)
    f32 = jnp.float32

    def nrm(k, shape, s):
        return jax.random.normal(k, shape, f32) * s

    u = jax.random.uniform(ks[19], (DEPTH, LRU_WIDTH), f32, 0.9, 0.999)
    a_base = u ** (1.0 / RG_C)
    return {
        'x_prompt': nrm(ks[0], (BATCH, SEQ, D_MODEL), 1.0),
        'x_sample': nrm(ks[1], (DEC_BATCH, DEC_SEQ, D_MODEL), 1.0),
        'cache_k': nrm(ks[2], (DEPTH, DEC_BATCH, PAST_LEN, SB_HEADS, SB_HEAD_DIM), 1.0),
        'cache_v': nrm(ks[3], (DEPTH, DEC_BATCH, PAST_LEN, SB_HEADS, SB_HEAD_DIM), 1.0),
        'state_conv': nrm(ks[4], (DEPTH, DEC_BATCH, CONV_WIDTH - 1, LRU_WIDTH), 1.0),
        'state_lru': nrm(ks[5], (DEPTH, DEC_BATCH, LRU_WIDTH), 0.5),
        'c_prompt': nrm(ks[6], (BATCH, D_MODEL), 1.0),
        'c_sample': nrm(ks[7], (DEC_BATCH, D_MODEL), 1.0),
        'ln_in_g': 1.0 + nrm(ks[8], (D_MODEL,), 0.02),
        'ln_in_b': nrm(ks[9], (D_MODEL,), 0.02),
        'w_ada': nrm(ks[10], (DEPTH, D_MODEL, 6 * D_MODEL), 0.5 * D_MODEL ** -0.5),
        'b_ada': nrm(ks[11], (DEPTH, 6 * D_MODEL), 0.02),
        'w_in': nrm(ks[12], (DEPTH, D_MODEL, IN_COLS), D_MODEL ** -0.5),
        'conv_w': nrm(ks[13], (DEPTH, CONV_WIDTH, LRU_WIDTH), CONV_WIDTH ** -0.5),
        'conv_b': nrm(ks[14], (DEPTH, LRU_WIDTH), 0.02),
        'gate_a_w': nrm(ks[15], (DEPTH, LRU_HEADS, LRU_HEAD_DIM, LRU_HEAD_DIM), LRU_HEAD_DIM ** -0.5),
        'gate_a_b': nrm(ks[16], (DEPTH, LRU_WIDTH), 0.02),
        'gate_x_w': nrm(ks[17], (DEPTH, LRU_HEADS, LRU_HEAD_DIM, LRU_HEAD_DIM), LRU_HEAD_DIM ** -0.5),
        'gate_x_b': nrm(ks[18], (DEPTH, LRU_WIDTH), 0.02),
        'lru_lambda': jnp.log(a_base) - jnp.log1p(-a_base),
        'w_out': nrm(ks[20], (DEPTH, MIX_WIDTH, D_MODEL), MIX_WIDTH ** -0.5 * DEEPNORM_BETA),
        'ln1_g': 1.0 + nrm(ks[21], (DEPTH, D_MODEL), 0.02),
        'ln1_b': nrm(ks[22], (DEPTH, D_MODEL), 0.02),
        'router_w': nrm(ks[23], (DEPTH, D_MODEL, N_EXPERTS), D_MODEL ** -0.5),
        'router_b': nrm(ks[24], (DEPTH, N_EXPERTS), 0.01),
        'w_gu': nrm(ks[25], (DEPTH, N_EXPERTS, D_MODEL, 2 * D_FF), D_MODEL ** -0.5),
        'b_gu': nrm(ks[26], (DEPTH, N_EXPERTS, 2 * D_FF), 0.02),
        'w_down': nrm(ks[27], (DEPTH, N_EXPERTS, D_FF, D_MODEL), D_FF ** -0.5 * DEEPNORM_BETA),
        'b_down': nrm(ks[28], (DEPTH, N_EXPERTS, D_MODEL), 0.01),
        'ln2_g': 1.0 + nrm(ks[29], (DEPTH, D_MODEL), 0.02),
        'ln2_b': nrm(ks[30], (DEPTH, D_MODEL), 0.02),
    }


def reference(x_prompt, x_sample, cache_k, cache_v, state_conv, state_lru, c_prompt, c_sample,
              ln_in_g, ln_in_b, w_ada, b_ada, w_in, conv_w, conv_b, gate_a_w, gate_a_b,
              gate_x_w, gate_x_b, lru_lambda, w_out, ln1_g, ln1_b, router_w, router_b,
              w_gu, b_gu, w_down, b_down, ln2_g, ln2_b):
    B, S, D = x_prompt.shape
    DB, DS, _ = x_sample.shape
    xp = layer_norm(x_prompt, ln_in_g, ln_in_b)
    xs = layer_norm(x_sample, ln_in_g, ln_in_b)
    zero_conv = jnp.zeros((B, CONV_WIDTH - 1, LRU_WIDTH), x_prompt.dtype)
    zero_h = jnp.zeros((B, LRU_WIDTH), x_prompt.dtype)
    kp_l, vp_l, cp_l, hp_l = [], [], [], []
    ks_l, vs_l, cs_l, hs_l = [], [], [], []
    for l in range(DEPTH):
        mp = adaln(c_prompt, w_ada[l], b_ada[l])
        ms = adaln(c_sample, w_ada[l], b_ada[l])
        mix_w = (w_in[l], conv_w[l], conv_b[l], gate_a_w[l], gate_a_b[l],
                 gate_x_w[l], gate_x_b[l], lru_lambda[l], w_out[l])
        op, kp, vp, cp, hp = token_mixer(modulate(xp, mp[0], mp[1]), zero_conv, zero_h, None, None, *mix_w)
        os_, ks_, vs_, cs_, hs_ = token_mixer(modulate(xs, ms[0], ms[1]), state_conv[l], state_lru[l],
                                              cache_k[l], cache_v[l], *mix_w)
        xp = layer_norm(DEEPNORM_ALPHA * xp + mp[2] * op, ln1_g[l], ln1_b[l])
        xs = layer_norm(DEEPNORM_ALPHA * xs + ms[2] * os_, ln1_g[l], ln1_b[l])
        tokens = jnp.concatenate([modulate(xp, mp[3], mp[4]).reshape(B * S, D),
                                  modulate(xs, ms[3], ms[4]).reshape(DB * DS, D)], axis=0)
        f = moe(tokens, router_w[l], router_b[l], w_gu[l], b_gu[l], w_down[l], b_down[l])
        xp = layer_norm(DEEPNORM_ALPHA * xp + mp[5] * f[:B * S].reshape(B, S, D), ln2_g[l], ln2_b[l])
        xs = layer_norm(DEEPNORM_ALPHA * xs + ms[5] * f[B * S:].reshape(DB, DS, D), ln2_g[l], ln2_b[l])
        kp_l.append(kp); vp_l.append(vp); cp_l.append(cp); hp_l.append(hp)
        ks_l.append(ks_); vs_l.append(vs_); cs_l.append(cs_); hs_l.append(hs_)
    return (xp, xs, jnp.stack(kp_l), jnp.stack(vp_l), jnp.stack(cp_l), jnp.stack(hp_l),
            jnp.stack(ks_l), jnp.stack(vs_l), jnp.stack(cs_l), jnp.stack(hs_l))
```

```python
import functools

import jax
import jax.numpy as jnp
import numpy as np
from jax import lax
from jax.experimental import pallas as pl
from jax.experimental.pallas import tpu as pltpu

F32 = jnp.float32
BF16 = jnp.bfloat16

LANES = 128
LRU_HEADS = 8
CONV_WIDTH = 4
RG_C = 8.0
SB_HEADS = 4
SB_HEAD_DIM = 128
SB_SCALE = SB_HEAD_DIM ** -0.5
SB_TK = 256
N_EXPERTS = 32
TOP_K = 4
SWIGLU_LIMIT = 7.0
SWIGLU_ALPHA = 1.702
MOE_BLOCK = 256
LN_EPS = 1e-5
NEG_BIG = -1e30
VMEM_LIMIT = 56 * 1024 * 1024


def _cparams(*sem):
    return pltpu.CompilerParams(dimension_semantics=sem, vmem_limit_bytes=VMEM_LIMIT)


def _split3(a):
    hi = a.astype(BF16)
    r1 = a - hi.astype(F32)
    mid = r1.astype(BF16)
    lo = (r1 - mid.astype(F32)).astype(BF16)
    return hi, mid, lo


def _dot_f32(a, b):
    a0, a1, a2 = _split3(a)
    b0, b1, b2 = _split3(b)
    d = functools.partial(jnp.dot, preferred_element_type=F32)
    return (d(a0, b0) + (d(a0, b1) + d(a1, b0))
            + (d(a0, b2) + d(a1, b1) + d(a2, b0)))


def _layer_norm(y, g, b):
    mu = jnp.mean(y, axis=-1, keepdims=True)
    d = y - mu
    var = jnp.mean(d * d, axis=-1, keepdims=True)
    return d * lax.rsqrt(var + LN_EPS) * g + b


def _softplus(z):
    return jnp.maximum(z, 0.0) + jnp.log(1.0 + jnp.exp(-jnp.abs(z)))


def _gelu_tanh(x):
    c = np.sqrt(2.0 / np.pi).astype(np.float32)
    return 0.5 * x * (1.0 + jnp.tanh(c * (x + 0.044715 * (x * x * x))))


def _adaln_kernel(c_ref, w_ref, b_ref, o_ref):
    c = c_ref[...]
    s = c * jax.nn.sigmoid(c)
    o_ref[0] = _dot_f32(s, w_ref[0]) + b_ref[0]


def _adaln(c_all, w_ada, b_ada):
    depth, d, n = w_ada.shape
    rows = c_all.shape[0]
    tn = 1024
    return pl.pallas_call(
        _adaln_kernel,
        grid=(depth, n // tn),
        in_specs=[pl.BlockSpec((rows, d), lambda l, j: (0, 0)),
                  pl.BlockSpec((1, d, tn), lambda l, j: (l, 0, j)),
                  pl.BlockSpec((1, 1, tn), lambda l, j: (l, 0, j))],
        out_specs=pl.BlockSpec((1, rows, tn), lambda l, j: (l, 0, j)),
        out_shape=jax.ShapeDtypeStruct((depth, rows, n), F32),
        compiler_params=_cparams("parallel", "parallel"),
        name="adaln",
    )(c_all, w_ada, b_ada.reshape(depth, 1, n))


def _ln_kernel(x_ref, g_ref, b_ref, o_ref):
    o_ref[...] = _layer_norm(x_ref[...], g_ref[...], b_ref[...])


def _ln_in(x2, g, b, tm):
    t, d = x2.shape
    return pl.pallas_call(
        _ln_kernel,
        grid=(t // tm,),
        in_specs=[pl.BlockSpec((tm, d), lambda i: (i, 0)),
                  pl.BlockSpec((1, d), lambda i: (0, 0)),
                  pl.BlockSpec((1, d), lambda i: (0, 0))],
        out_specs=pl.BlockSpec((tm, d), lambda i: (i, 0)),
        out_shape=jax.ShapeDtypeStruct((t, d), F32),
        compiler_params=_cparams("parallel"),
        name="ln_in",
    )(x2, g.reshape(1, d), b.reshape(1, d))


def _inproj_kernel(x_ref, mod_ref, w_ref, xr_ref, gy_ref, q_ref, k_ref, v_ref, kb_ref, vb_ref):
    w = xr_ref.shape[-1]
    shift = mod_ref[0, 0:1, :]
    scale = mod_ref[0, 1:2, :]
    h = (x_ref[...] * (1.0 + scale) + shift).astype(BF16)

    def proj(j):
        return jnp.dot(h, w_ref[:, j * w:(j + 1) * w], preferred_element_type=F32)

    xr_ref[...] = proj(0)
    gy_ref[...] = _gelu_tanh(proj(1))
    q_ref[...] = (proj(2) * SB_SCALE).astype(BF16)
    k = proj(3)
    k_ref[...] = k
    kb_ref[...] = k.astype(BF16)
    v = proj(4)
    v_ref[...] = v
    vb_ref[...] = v.astype(BF16)


def _inproj(x2, mod, w_in_b, seq0, seq_len, tm):
    t, d = x2.shape
    w = w_in_b.shape[1] // 5
    tiles_per_seq = seq_len // tm
    row = pl.BlockSpec((tm, w), lambda i: (i, 0))
    f32o = jax.ShapeDtypeStruct((t, w), F32)
    bf16o = jax.ShapeDtypeStruct((t, w), BF16)
    return pl.pallas_call(
        _inproj_kernel,
        grid=(t // tm,),
        in_specs=[pl.BlockSpec((tm, d), lambda i: (i, 0)),
                  pl.BlockSpec((1, 6, d), lambda i: (seq0 + i // tiles_per_seq, 0, 0)),
                  pl.BlockSpec((d, 5 * w), lambda i: (0, 0))],
        out_specs=[row] * 7,
        out_shape=[f32o, f32o, bf16o, f32o, f32o, bf16o, bf16o],
        compiler_params=_cparams("parallel"),
        name="inproj",
    )(x2, mod, w_in_b)


def _rglru_kernel(xr_ref, gy_ref, cs_ref, h0_ref, cw_ref, cb_ref, wa_ref, ba_ref, wx_ref, bx_ref,
                  lam_ref, out_ref, nconv_ref, nh_ref, xpad_ref, h_ref):
    ts, w = xr_ref.shape
    j = pl.program_id(1)

    @pl.when(j == 0)
    def _():
        xpad_ref[0:8, :] = jnp.zeros((8, w), F32)
        xpad_ref[8 - (CONV_WIDTH - 1):8, :] = cs_ref[0]
        h_ref[...] = h0_ref[0]

    xpad_ref[8:8 + ts, :] = xr_ref[...]
    xc = cb_ref[...]
    for c in range(CONV_WIDTH):
        lo = 8 - (CONV_WIDTH - 1) + c
        xc = xc + xpad_ref[lo:lo + ts, :] * cw_ref[c:c + 1, :]
    nconv_ref[0] = xpad_ref[ts + 8 - (CONV_WIDTH - 1):ts + 8, :]
    xpad_ref[0:8, :] = xpad_ref[ts:ts + 8, :]

    xcb = xc.astype(BF16)
    r = jax.nn.sigmoid(jnp.dot(xcb, wa_ref[...], preferred_element_type=F32) + ba_ref[...])
    gi = jax.nn.sigmoid(jnp.dot(xcb, wx_ref[...], preferred_element_type=F32) + bx_ref[...])
    log_a = (RG_C * r) * (-_softplus(-lam_ref[...]))
    a = jnp.exp(log_a)
    b = jnp.sqrt(1.0 - a * a) * (gi * xc)

    rows = lax.broadcasted_iota(jnp.int32, (ts, w), 0)
    dist = 1
    while dist < ts:
        keep = rows >= dist
        a_prev = jnp.where(keep, pltpu.roll(a, dist, 0), 1.0)
        b_prev = jnp.where(keep, pltpu.roll(b, dist, 0), 0.0)
        b = a * b_prev + b
        a = a * a_prev
        dist *= 2
    hs = b + a * h_ref[...]
    h_last = hs[ts - 1:ts, :]
    h_ref[...] = h_last
    nh_ref[0] = h_last
    out_ref[...] = (hs * gy_ref[...]).astype(BF16)


def _rglru(xr, gy, conv_state, h0, cw, cb, wa, ba, wx, bx, lam, nseq, seq_len, ts):
    t, w = xr.shape
    n = seq_len // ts
    row = pl.BlockSpec((ts, w), lambda b, j: (b * n + j, 0))
    vec = pl.BlockSpec((1, w), lambda b, j: (0, 0))
    return pl.pallas_call(
        _rglru_kernel,
        grid=(nseq, n),
        in_specs=[row, row,
                  pl.BlockSpec((1, CONV_WIDTH - 1, w), lambda b, j: (b, 0, 0)),
                  pl.BlockSpec((1, 1, w), lambda b, j: (b, 0, 0)),
                  pl.BlockSpec((CONV_WIDTH, w), lambda b, j: (0, 0)),
                  vec,
                  pl.BlockSpec((w, w), lambda b, j: (0, 0)), vec,
                  pl.BlockSpec((w, w), lambda b, j: (0, 0)), vec,
                  vec],
        out_specs=[row,
                   pl.BlockSpec((1, CONV_WIDTH - 1, w), lambda b, j: (b, 0, 0)),
                   pl.BlockSpec((1, 1, w), lambda b, j: (b, 0, 0))],
        out_shape=[jax.ShapeDtypeStruct((t, w), BF16),
                   jax.ShapeDtypeStruct((nseq, CONV_WIDTH - 1, w), F32),
                   jax.ShapeDtypeStruct((nseq, 1, w), F32)],
        scratch_shapes=[pltpu.VMEM((ts + 8, w), F32), pltpu.VMEM((1, w), F32)],
        compiler_params=_cparams("parallel", "arbitrary"),
        name="rglru",
    )(xr, gy, conv_state, h0.reshape(nseq, 1, w), cw, cb.reshape(1, w), wa, ba.reshape(1, w),
      wx, bx.reshape(1, w), lam.reshape(1, w))


def _sb_block(q, k, v, upper, off, acc, mask):
    z = lax.dot_general(q, k, (((1,), (1,)), ((), ())), preferred_element_type=F32)
    sp = _softplus(z)
    log_keep = -sp
    if mask is not None:
        log_keep = jnp.where(mask, log_keep, 0.0)
    lk_hi = log_keep.astype(BF16)
    lk_lo = (log_keep - lk_hi.astype(F32)).astype(BF16)
    tail = (jnp.dot(lk_hi, upper, preferred_element_type=F32)
            + jnp.dot(lk_lo, upper, preferred_element_type=F32))
    wgt = jnp.exp((z - sp) + tail + off)
    if mask is not None:
        wgt = jnp.where(mask, wgt, 0.0)
    acc = acc + jnp.dot(wgt.astype(BF16), v, preferred_element_type=F32)
    off = off + jnp.sum(log_keep, axis=-1, keepdims=True)
    return off, acc


def _sb_kernel(q_ref, k_ref, v_ref, up_ref, o_ref, *, tq, q_pos0, full_per_tile, full_base, n_diag):
    i = pl.program_id(2)
    tk = SB_TK
    q = q_ref[...]
    upper = up_ref[...]
    n_full = full_base + i * full_per_tile
    q_pos = q_pos0 + i * tq + lax.broadcasted_iota(jnp.int32, (tq, tk), 0)
    col = lax.broadcasted_iota(jnp.int32, (tq, tk), 1)
    off = jnp.zeros((tq, 1), F32)
    acc = jnp.zeros((tq, SB_HEAD_DIM), F32)
    for dblk in range(n_diag - 1, -1, -1):
        start = pl.multiple_of((n_full + dblk) * tk, tk)
        mask = (start + col) < q_pos
        off, acc = _sb_block(q, k_ref[pl.ds(start, tk), :], v_ref[pl.ds(start, tk), :],
                             upper, off, acc, mask)

    def body(t, carry):
        off, acc = carry
        start = pl.multiple_of((n_full - 1 - t) * tk, tk)
        return _sb_block(q, k_ref[pl.ds(start, tk), :], v_ref[pl.ds(start, tk), :],
                         upper, off, acc, None)

    off, acc = lax.fori_loop(0, n_full, body, (off, acc))
    o_ref[...] = acc.astype(o_ref.dtype)


def _stick_breaking(q, k_all, v_all, upper, nseq, seq_len, tq, q_pos0):
    t, hw = q.shape
    kpad = k_all.shape[1]
    nq = seq_len // tq
    n_diag = -(-tq // SB_TK)
    assert q_pos0 % SB_TK == 0 and kpad % SB_TK == 0
    assert tq % SB_TK == 0 or nq == 1
    assert kpad >= q_pos0 + nq * n_diag * SB_TK
    kern = functools.partial(_sb_kernel, tq=tq, q_pos0=q_pos0, full_per_tile=tq // SB_TK,
                             full_base=q_pos0 // SB_TK, n_diag=n_diag)
    qspec = pl.BlockSpec((tq, SB_HEAD_DIM), lambda b, h, i: (b * nq + i, h))
    kvspec = pl.BlockSpec((None, kpad, SB_HEAD_DIM), lambda b, h, i: (b, 0, h))
    return pl.pallas_call(
        kern,
        grid=(nseq, SB_HEADS, nq),
        in_specs=[qspec, kvspec, kvspec, pl.BlockSpec((SB_TK, SB_TK), lambda b, h, i: (0, 0))],
        out_specs=qspec,
        out_shape=jax.ShapeDtypeStruct((t, hw), BF16),
        compiler_params=_cparams("parallel", "parallel", "arbitrary"),
        name="stick_breaking",
    )(q, k_all, v_all, upper)


def _outproj_kernel(lru_ref, att_ref, x_ref, mod_ref, wo_ref, g_ref, b_ref, rw_ref, rb_ref,
                    x1_ref, tok_ref, code_ref, gate_ref, cnt_ref, *, alpha):
    i = pl.program_id(0)
    w = lru_ref.shape[-1]
    out = (jnp.dot(lru_ref[...], wo_ref[0:w, :], preferred_element_type=F32)
           + jnp.dot(att_ref[...], wo_ref[w:2 * w, :], preferred_element_type=F32))
    gate1 = mod_ref[0, 2:3, :]
    x1 = _layer_norm(alpha * x_ref[...] + gate1 * out, g_ref[...], b_ref[...])
    x1_ref[...] = x1
    tok = x1 * (1.0 + mod_ref[0, 4:5, :]) + mod_ref[0, 3:4, :]
    tok_ref[...] = tok

    logits = _dot_f32(tok, rw_ref[...]) + rb_ref[...]
    lane = lax.broadcasted_iota(jnp.int32, logits.shape, 1).astype(F32)
    code = jnp.zeros(logits.shape, F32)
    gates = jnp.zeros(logits.shape, F32)
    denom = jnp.zeros((logits.shape[0], 1), F32)
    top0 = None
    for kk in range(TOP_K):
        m = jnp.max(logits, axis=-1, keepdims=True)
        idx = jnp.min(jnp.where(logits == m, lane, float(LANES)), axis=-1, keepdims=True)
        hit = lane == idx
        if kk == 0:
            top0 = m
        e = jnp.exp(m - top0)
        denom = denom + e
        code = jnp.where(hit, float(kk + 1), code)
        gates = jnp.where(lane == kk, e, gates)
        logits = jnp.where(hit, -jnp.inf, logits)
    code_ref[...] = code.astype(BF16)
    gate_ref[...] = gates / denom

    @pl.when(i == 0)
    def _():
        cnt_ref[...] = jnp.zeros(cnt_ref.shape, F32)

    cnt_ref[...] += jnp.sum((code > 0.0).astype(F32), axis=0, keepdims=True)


def _outproj(lru, att, x2, mod, wo_b, g, b, rw_pad, rb_pad, seq0, seq_len, tm, alpha):
    t, d = x2.shape
    w = lru.shape[1]
    tiles_per_seq = seq_len // tm
    rowd = pl.BlockSpec((tm, d), lambda i: (i, 0))
    roww = pl.BlockSpec((tm, w), lambda i: (i, 0))
    rowl = pl.BlockSpec((tm, LANES), lambda i: (i, 0))
    vec = pl.BlockSpec((1, d), lambda i: (0, 0))
    return pl.pallas_call(
        functools.partial(_outproj_kernel, alpha=alpha),
        grid=(t // tm,),
        in_specs=[roww, roww, rowd,
                  pl.BlockSpec((1, 6, d), lambda i: (seq0 + i // tiles_per_seq, 0, 0)),
                  pl.BlockSpec((2 * w, d), lambda i: (0, 0)),
                  vec, vec,
                  pl.BlockSpec((d, LANES), lambda i: (0, 0)),
                  pl.BlockSpec((1, LANES), lambda i: (0, 0))],
        out_specs=[rowd, rowd, rowl, rowl, pl.BlockSpec((8, LANES), lambda i: (0, 0))],
        out_shape=[jax.ShapeDtypeStruct((t, d), F32), jax.ShapeDtypeStruct((t, d), F32),
                   jax.ShapeDtypeStruct((t, LANES), BF16), jax.ShapeDtypeStruct((t, LANES), F32),
                   jax.ShapeDtypeStruct((8, LANES), F32)],
        compiler_params=_cparams("arbitrary"),
        name="outproj_router",
    )(lru, att, x2, mod, wo_b, g.reshape(1, d), b.reshape(1, d), rw_pad, rb_pad)


def _rank_kernel(code_ref, cin_ref, ps_ref, dest_ref, cout_ref, carry_ref, low_ref):
    i = pl.program_id(0)
    tr = code_ref.shape[0]

    @pl.when(i == 0)
    def _():
        carry_ref[...] = cin_ref[0:1, :]
        r = lax.broadcasted_iota(jnp.int32, (tr, tr), 0)
        c = lax.broadcasted_iota(jnp.int32, (tr, tr), 1)
        low_ref[...] = (c < r).astype(BF16)

    code = code_ref[...].astype(F32)
    member = (code > 0.0).astype(BF16)
    before = jnp.dot(low_ref[...], member, preferred_element_type=F32) + carry_ref[...]
    base = before + ps_ref[...]
    lane = lax.broadcasted_iota(jnp.int32, code.shape, 1)
    dest = jnp.zeros(code.shape, jnp.int32)
    for kk in range(TOP_K):
        d = jnp.sum(jnp.where(code == float(kk + 1), base, 0.0), axis=-1, keepdims=True)
        dest = jnp.where(lane == kk, d.astype(jnp.int32), dest)
    dest_ref[...] = dest
    carry = before[tr - 1:tr, :] + (code[tr - 1:tr, :] > 0.0).astype(F32)
    carry_ref[...] = carry
    cout_ref[...] = jnp.broadcast_to(carry, cout_ref.shape)


def _rank(code, carry_in, pad_start, tr):
    t = code.shape[0]
    rowl = pl.BlockSpec((tr, LANES), lambda i: (i, 0))
    small = pl.BlockSpec((8, LANES), lambda i: (0, 0))
    return pl.pallas_call(
        _rank_kernel,
        grid=(t // tr,),
        in_specs=[rowl, small, pl.BlockSpec((1, LANES), lambda i: (0, 0))],
        out_specs=[rowl, small],
        out_shape=[jax.ShapeDtypeStruct((t, LANES), jnp.int32),
                   jax.ShapeDtypeStruct((8, LANES), F32)],
        scratch_shapes=[pltpu.VMEM((1, LANES), F32), pltpu.VMEM((tr, tr), BF16)],
        compiler_params=_cparams("arbitrary"),
        name="moe_rank",
    )(code, carry_in, pad_start)


def _scatter_kernel(dest_ref, tok_ref, rows_in_ref, rows_ref, sem):
    del rows_in_ref
    tm = tok_ref.shape[0]

    def row_copy(s):
        return pltpu.make_async_copy(tok_ref.at[pl.ds(s // TOP_K, 1), :],
                                     rows_ref.at[pl.ds(dest_ref[s], 1), :], sem)

    def start(s, c):
        row_copy(s).start()
        return c

    def wait(s, c):
        row_copy(s).wait()
        return c

    lax.fori_loop(0, tm * TOP_K, start, 0)
    lax.fori_loop(0, tm * TOP_K, wait, 0)


def _scatter(dest_flat, tok, rows_buf, tm):
    t, d = tok.shape
    return pl.pallas_call(
        _scatter_kernel,
        grid=(t // tm,),
        in_specs=[pl.BlockSpec((tm * TOP_K,), lambda i: (i,), memory_space=pltpu.SMEM),
                  pl.BlockSpec((tm, d), lambda i: (i, 0)),
                  pl.BlockSpec(memory_space=pl.ANY)],
        out_specs=pl.BlockSpec(memory_space=pl.ANY),
        out_shape=jax.ShapeDtypeStruct(rows_buf.shape, rows_buf.dtype),
        scratch_shapes=[pltpu.SemaphoreType.DMA(())],
        input_output_aliases={2: 0},
        compiler_params=_cparams("arbitrary"),
        name="moe_scatter",
    )(dest_flat, tok, rows_buf)


def _expert_kernel(be_ref, x_ref, wgu_ref, bgu_ref, wd_ref, bd_ref, y_ref):
    del be_ref
    dff = wd_ref.shape[1]
    xb = x_ref[...].astype(BF16)
    gu = jnp.dot(xb, wgu_ref[0], preferred_element_type=F32) + bgu_ref[0]
    gate = jnp.minimum(gu[:, :dff], SWIGLU_LIMIT)
    up = jnp.clip(gu[:, dff:], -SWIGLU_LIMIT, SWIGLU_LIMIT)
    act = (up + 1.0) * gate * jax.nn.sigmoid(SWIGLU_ALPHA * gate)
    y_ref[...] = jnp.dot(act.astype(BF16), wd_ref[0], preferred_element_type=F32) + bd_ref[0]


def _experts(block_e, x_rows, wgu_b, bgu, wd_b, bd):
    rows, d = x_rows.shape
    ne, _, n_gu = wgu_b.shape
    dff = wd_b.shape[1]
    nb = rows // MOE_BLOCK
    return pl.pallas_call(
        _expert_kernel,
        grid_spec=pltpu.PrefetchScalarGridSpec(
            num_scalar_prefetch=1,
            grid=(nb,),
            in_specs=[pl.BlockSpec((MOE_BLOCK, d), lambda i, be: (i, 0)),
                      pl.BlockSpec((1, d, n_gu), lambda i, be: (be[i], 0, 0)),
                      pl.BlockSpec((1, 1, n_gu), lambda i, be: (be[i], 0, 0)),
                      pl.BlockSpec((1, dff, d), lambda i, be: (be[i], 0, 0)),
                      pl.BlockSpec((1, 1, d), lambda i, be: (be[i], 0, 0))],
            out_specs=pl.BlockSpec((MOE_BLOCK, d), lambda i, be: (i, 0))),
        out_shape=jax.ShapeDtypeStruct((rows, d), F32),
        compiler_params=_cparams("arbitrary"),
        name="moe_experts",
    )(block_e, x_rows, wgu_b, bgu.reshape(ne, 1, n_gu), wd_b, bd.reshape(ne, 1, d))


def _combine_kernel(dest_ref, y_hbm, gate_ref, x_ref, mod_ref, g_ref, b_ref, o_ref, ybuf, sem, *, alpha):
    tm = x_ref.shape[0]

    def row_copy(s):
        return pltpu.make_async_copy(y_hbm.at[pl.ds(dest_ref[s], 1), :],
                                     ybuf.at[s % TOP_K, pl.ds(s // TOP_K, 1), :], sem)

    def start(s, c):
        row_copy(s).start()
        return c

    def wait(s, c):
        row_copy(s).wait()
        return c

    lax.fori_loop(0, tm * TOP_K, start, 0)
    lax.fori_loop(0, tm * TOP_K, wait, 0)

    gates = gate_ref[...]
    f = ybuf[0] * gates[:, 0:1]
    for kk in range(1, TOP_K):
        f = f + ybuf[kk] * gates[:, kk:kk + 1]
    gate2 = mod_ref[0, 5:6, :]
    o_ref[...] = _layer_norm(alpha * x_ref[...] + gate2 * f, g_ref[...], b_ref[...])


def _combine(dest_flat, y_rows, gates, x1, mod, g, b, seq0, seq_len, tm, alpha):
    t, d = x1.shape
    tiles_per_seq = seq_len // tm
    rowd = pl.BlockSpec((tm, d), lambda i: (i, 0))
    vec = pl.BlockSpec((1, d), lambda i: (0, 0))
    return pl.pallas_call(
        functools.partial(_combine_kernel, alpha=alpha),
        grid=(t // tm,),
        in_specs=[pl.BlockSpec((tm * TOP_K,), lambda i: (i,), memory_space=pltpu.SMEM),
                  pl.BlockSpec(memory_space=pl.ANY),
                  pl.BlockSpec((tm, LANES), lambda i: (i, 0)),
                  rowd,
                  pl.BlockSpec((1, 6, d), lambda i: (seq0 + i // tiles_per_seq, 0, 0)),
                  vec, vec],
        out_specs=rowd,
        out_shape=jax.ShapeDtypeStruct((t, d), F32),
        scratch_shapes=[pltpu.VMEM((TOP_K, tm, d), F32), pltpu.SemaphoreType.DMA(())],
        compiler_params=_cparams("arbitrary"),
        name="moe_combine",
    )(dest_flat, y_rows, gates, x1, mod, g.reshape(1, d), b.reshape(1, d))


def _block_diag(wh):
    heads, hd, _ = wh.shape
    eye = jnp.eye(heads, dtype=wh.dtype)
    return jnp.einsum('hij,hg->higj', wh, eye).reshape(heads * hd, heads * hd)


def kernel(x_prompt, x_sample, cache_k, cache_v, state_conv, state_lru, c_prompt, c_sample, ln_in_g, ln_in_b, w_ada, b_ada, w_in, conv_w, conv_b, gate_a_w, gate_a_b, gate_x_w, gate_x_b, lru_lambda, w_out, ln1_g, ln1_b, router_w, router_b, w_gu, b_gu, w_down, b_down, ln2_g, ln2_b):
    nb_p, s_p, d = x_prompt.shape
    nb_s, s_s, _ = x_sample.shape
    depth = w_in.shape[0]
    past = cache_k.shape[2]
    w = state_lru.shape[-1]
    hw = SB_HEADS * SB_HEAD_DIM
    t_p, t_s = nb_p * s_p, nb_s * s_s
    alpha = float((2 * depth) ** 0.25)

    tm_p, tm_s = min(512, s_p), min(512, s_s)
    ts_p, ts_s = min(256, s_p), min(256, s_s)
    tq_p, tq_s = min(SB_TK, s_p), min(SB_TK, s_s)
    tg_p, tg_s = min(256, s_p), min(256, s_s)

    n_seq = nb_p + nb_s
    c_all = jnp.concatenate([c_prompt, c_sample, jnp.zeros((-n_seq % 8, d), F32)], axis=0)
    mod_all = _adaln(c_all, w_ada, b_ada).reshape(depth, c_all.shape[0], 6, d)

    xp = _ln_in(x_prompt.reshape(t_p, d), ln_in_g, ln_in_b, tm_p)
    xs = _ln_in(x_sample.reshape(t_s, d), ln_in_g, ln_in_b, tm_s)

    upper = jnp.asarray(np.arange(SB_TK)[:, None] > np.arange(SB_TK)[None, :], BF16)
    kpad_s = -(-(past + s_s) // SB_TK) * SB_TK
    zero_conv = jnp.zeros((nb_p, CONV_WIDTH - 1, w), F32)
    zero_h = jnp.zeros((nb_p, w), F32)

    tk_total = (t_p + t_s) * TOP_K
    n_blocks = -(-(tk_total + N_EXPERTS * (MOE_BLOCK - 1)) // MOE_BLOCK)
    n_rows = n_blocks * MOE_BLOCK

    outs = [[] for _ in range(8)]
    for l in range(depth):
        mod = mod_all[l]
        w_in_b = w_in[l].astype(BF16)
        wo_b = w_out[l].astype(BF16)
        wa = _block_diag(gate_a_w[l]).astype(BF16)
        wx = _block_diag(gate_x_w[l]).astype(BF16)
        rw_pad = jnp.pad(router_w[l], ((0, 0), (0, LANES - N_EXPERTS)))
        rb_pad = jnp.pad(router_b[l], (0, LANES - N_EXPERTS), constant_values=NEG_BIG).reshape(1, LANES)
        lru_w = (conv_w[l], conv_b[l], wa, gate_a_b[l], wx, gate_x_b[l], lru_lambda[l])

        xr, gy, q, k, v, kb, vb = _inproj(xp, mod, w_in_b, 0, s_p, tm_p)
        lru_p, nconv_p, nh_p = _rglru(xr, gy, zero_conv, zero_h, *lru_w, nb_p, s_p, ts_p)
        att_p = _stick_breaking(q, kb.reshape(nb_p, s_p, hw), vb.reshape(nb_p, s_p, hw), upper,
                                nb_p, s_p, tq_p, 0)
        outs[0].append(k.reshape(nb_p, s_p, SB_HEADS, SB_HEAD_DIM))
        outs[1].append(v.reshape(nb_p, s_p, SB_HEADS, SB_HEAD_DIM))
        outs[2].append(nconv_p)
        outs[3].append(nh_p.reshape(nb_p, w))

        xr, gy, q, k, v, kb, vb = _inproj(xs, mod, w_in_b, nb_p, s_s, tm_s)
        lru_s, nconv_s, nh_s = _rglru(xr, gy, state_conv[l], state_lru[l], *lru_w, nb_s, s_s, ts_s)
        kv_pad = jnp.zeros((nb_s, kpad_s - past - s_s, hw), BF16)
        k_all = jnp.concatenate([cache_k[l].reshape(nb_s, past, hw).astype(BF16),
                                 kb.reshape(nb_s, s_s, hw), kv_pad], axis=1)
        v_all = jnp.concatenate([cache_v[l].reshape(nb_s, past, hw).astype(BF16),
                                 vb.reshape(nb_s, s_s, hw), kv_pad], axis=1)
        att_s = _stick_breaking(q, k_all, v_all, upper, nb_s, s_s, tq_s, past)
        outs[4].append(k.reshape(nb_s, s_s, SB_HEADS, SB_HEAD_DIM))
        outs[5].append(v.reshape(nb_s, s_s, SB_HEADS, SB_HEAD_DIM))
        outs[6].append(nconv_s)
        outs[7].append(nh_s.reshape(nb_s, w))

        x1_p, tok_p, code_p, gate_p, cnt_p = _outproj(lru_p, att_p, xp, mod, wo_b, ln1_g[l], ln1_b[l],
                                                      rw_pad, rb_pad, 0, s_p, tm_p, alpha)
        x1_s, tok_s, code_s, gate_s, cnt_s = _outproj(lru_s, att_s, xs, mod, wo_b, ln1_g[l], ln1_b[l],
                                                      rw_pad, rb_pad, nb_p, s_s, tm_s, alpha)

        counts = (cnt_p[0, :N_EXPERTS] + cnt_s[0, :N_EXPERTS]).astype(jnp.int32)
        padded = (counts + MOE_BLOCK - 1) // MOE_BLOCK * MOE_BLOCK
        pad_end = jnp.cumsum(padded)
        pad_start = jnp.pad((pad_end - padded).astype(F32), (0, LANES - N_EXPERTS)).reshape(1, LANES)
        block_e = jnp.minimum(jnp.searchsorted(pad_end, jnp.arange(n_blocks) * MOE_BLOCK, side='right'),
                              N_EXPERTS - 1).astype(jnp.int32)

        dest_p, carry = _rank(code_p, jnp.zeros((8, LANES), F32), pad_start, tm_p)
        dest_s, _ = _rank(code_s, carry, pad_start, tm_s)
        dflat_p = dest_p[:, :TOP_K].reshape(t_p * TOP_K)
        dflat_s = dest_s[:, :TOP_K].reshape(t_s * TOP_K)

        x_rows = jnp.zeros((n_rows, d), F32)
        x_rows = _scatter(dflat_p, tok_p, x_rows, tg_p)
        x_rows = _scatter(dflat_s, tok_s, x_rows, tg_s)
        y_rows = _experts(block_e, x_rows, w_gu[l].astype(BF16), b_gu[l], w_down[l].astype(BF16), b_down[l])

        xp = _combine(dflat_p, y_rows, gate_p, x1_p, mod, ln2_g[l], ln2_b[l], 0, s_p, tg_p, alpha)
        xs = _combine(dflat_s, y_rows, gate_s, x1_s, mod, ln2_g[l], ln2_b[l], nb_p, s_s, tg_s, alpha)

    return (xp.reshape(nb_p, s_p, d), xs.reshape(nb_s, s_s, d),
            jnp.stack(outs[0]), jnp.stack(outs[1]), jnp.stack(outs[2]), jnp.stack(outs[3]),
            jnp.stack(outs[4]), jnp.stack(outs[5]), jnp.stack(outs[6]), jnp.stack(outs[7]))
```

```python
import functools

import jax
import jax.numpy as jnp
import numpy as np
from jax import lax
from jax.experimental import pallas as pl
from jax.experimental.pallas import tpu as pltpu

F32 = jnp.float32
BF16 = jnp.bfloat16

LANES = 128
LRU_HEADS = 8
CONV_WIDTH = 4
RG_C = 8.0
SB_HEADS = 4
SB_HEAD_DIM = 128
SB_SCALE = SB_HEAD_DIM ** -0.5
SB_TK = 512
LOG2E = 1.4426950408889634
N_EXPERTS = 32
TOP_K = 4
SWIGLU_LIMIT = 7.0
SWIGLU_ALPHA = 1.702
MOE_BLOCK = 256
LN_EPS = 1e-5
NEG_BIG = -1e30
VMEM_LIMIT = 56 * 1024 * 1024


def _cparams(*sem):
    return pltpu.CompilerParams(dimension_semantics=sem, vmem_limit_bytes=VMEM_LIMIT)


def _split3(a):
    hi = a.astype(BF16)
    r1 = a - hi.astype(F32)
    mid = r1.astype(BF16)
    lo = (r1 - mid.astype(F32)).astype(BF16)
    return hi, mid, lo


def _dot_f32(a, b):
    a0, a1, a2 = _split3(a)
    b0, b1, b2 = _split3(b)
    d = functools.partial(jnp.dot, preferred_element_type=F32)
    return (d(a0, b0) + (d(a0, b1) + d(a1, b0))
            + (d(a0, b2) + d(a1, b1) + d(a2, b0)))


def _layer_norm(y, g, b):
    mu = jnp.mean(y, axis=-1, keepdims=True)
    d = y - mu
    var = jnp.mean(d * d, axis=-1, keepdims=True)
    return d * lax.rsqrt(var + LN_EPS) * g + b


def _softplus(z):
    return jnp.maximum(z, 0.0) + jnp.log(1.0 + jnp.exp(-jnp.abs(z)))


def _gelu_tanh(x):
    c = np.sqrt(2.0 / np.pi).astype(np.float32)
    return 0.5 * x * (1.0 + jnp.tanh(c * (x + 0.044715 * (x * x * x))))


def _adaln_kernel(c_ref, w_ref, b_ref, o_ref):
    c = c_ref[...]
    s = c * jax.nn.sigmoid(c)
    o_ref[0] = _dot_f32(s, w_ref[0]) + b_ref[0]


def _adaln(c_all, w_ada, b_ada):
    depth, d, n = w_ada.shape
    rows = c_all.shape[0]
    tn = 1024
    return pl.pallas_call(
        _adaln_kernel,
        grid=(depth, n // tn),
        in_specs=[pl.BlockSpec((rows, d), lambda l, j: (0, 0)),
                  pl.BlockSpec((1, d, tn), lambda l, j: (l, 0, j)),
                  pl.BlockSpec((1, 1, tn), lambda l, j: (l, 0, j))],
        out_specs=pl.BlockSpec((1, rows, tn), lambda l, j: (l, 0, j)),
        out_shape=jax.ShapeDtypeStruct((depth, rows, n), F32),
        compiler_params=_cparams("parallel", "parallel"),
        name="adaln",
    )(c_all, w_ada, b_ada.reshape(depth, 1, n))


def _ln_kernel(x_ref, g_ref, b_ref, o_ref):
    o_ref[...] = _layer_norm(x_ref[...], g_ref[...], b_ref[...])


def _ln_in(x2, g, b, tm):
    t, d = x2.shape
    return pl.pallas_call(
        _ln_kernel,
        grid=(t // tm,),
        in_specs=[pl.BlockSpec((tm, d), lambda i: (i, 0)),
                  pl.BlockSpec((1, d), lambda i: (0, 0)),
                  pl.BlockSpec((1, d), lambda i: (0, 0))],
        out_specs=pl.BlockSpec((tm, d), lambda i: (i, 0)),
        out_shape=jax.ShapeDtypeStruct((t, d), F32),
        compiler_params=_cparams("parallel"),
        name="ln_in",
    )(x2, g.reshape(1, d), b.reshape(1, d))


def _inproj_kernel(x_ref, mod_ref, w_ref, xr_ref, gy_ref, q_ref, k_ref, v_ref, kb_ref, vb_ref):
    w = xr_ref.shape[-1]
    shift = mod_ref[0, 0:1, :]
    scale = mod_ref[0, 1:2, :]
    h = (x_ref[...] * (1.0 + scale) + shift).astype(BF16)

    def proj(j):
        return jnp.dot(h, w_ref[:, j * w:(j + 1) * w], preferred_element_type=F32)

    xr_ref[...] = proj(0)
    gy_ref[...] = _gelu_tanh(proj(1))
    q_ref[...] = (proj(2) * (SB_SCALE * LOG2E)).astype(BF16)
    k = proj(3)
    k_ref[...] = k
    kb_ref[...] = k.astype(BF16)
    v = proj(4)
    v_ref[...] = v
    vb_ref[...] = v.astype(BF16)


def _inproj(x2, mod, w_in_b, seq0, seq_len, tm):
    t, d = x2.shape
    w = w_in_b.shape[1] // 5
    tiles_per_seq = seq_len // tm
    row = pl.BlockSpec((tm, w), lambda i: (i, 0))
    f32o = jax.ShapeDtypeStruct((t, w), F32)
    bf16o = jax.ShapeDtypeStruct((t, w), BF16)
    return pl.pallas_call(
        _inproj_kernel,
        grid=(t // tm,),
        in_specs=[pl.BlockSpec((tm, d), lambda i: (i, 0)),
                  pl.BlockSpec((1, 6, d), lambda i: (seq0 + i // tiles_per_seq, 0, 0)),
                  pl.BlockSpec((d, 5 * w), lambda i: (0, 0))],
        out_specs=[row] * 7,
        out_shape=[f32o, f32o, bf16o, f32o, f32o, bf16o, bf16o],
        compiler_params=_cparams("parallel"),
        name="inproj",
    )(x2, mod, w_in_b)


def _rglru_kernel(xr_ref, gy_ref, cs_ref, h0_ref, cw_ref, cb_ref, wa_ref, ba_ref, wx_ref, bx_ref,
                  lam_ref, out_ref, nconv_ref, nh_ref, xpad_ref, h_ref):
    ts, w = xr_ref.shape
    j = pl.program_id(1)

    @pl.when(j == 0)
    def _():
        xpad_ref[0:8, :] = jnp.zeros((8, w), F32)
        xpad_ref[8 - (CONV_WIDTH - 1):8, :] = cs_ref[0]
        h_ref[...] = h0_ref[0]

    xpad_ref[8:8 + ts, :] = xr_ref[...]
    xc = cb_ref[...]
    for c in range(CONV_WIDTH):
        lo = 8 - (CONV_WIDTH - 1) + c
        xc = xc + xpad_ref[lo:lo + ts, :] * cw_ref[c:c + 1, :]
    nconv_ref[0] = xpad_ref[ts + 8 - (CONV_WIDTH - 1):ts + 8, :]
    xpad_ref[0:8, :] = xpad_ref[ts:ts + 8, :]

    xcb = xc.astype(BF16)
    r = jax.nn.sigmoid(jnp.dot(xcb, wa_ref[...], preferred_element_type=F32) + ba_ref[...])
    gi = jax.nn.sigmoid(jnp.dot(xcb, wx_ref[...], preferred_element_type=F32) + bx_ref[...])
    log_a = (RG_C * r) * (-_softplus(-lam_ref[...]))
    a = jnp.exp(log_a)
    b = jnp.sqrt(1.0 - a * a) * (gi * xc)

    rows = lax.broadcasted_iota(jnp.int32, (ts, w), 0)
    dist = 1
    while dist < ts:
        keep = rows >= dist
        a_prev = jnp.where(keep, pltpu.roll(a, dist, 0), 1.0)
        b_prev = jnp.where(keep, pltpu.roll(b, dist, 0), 0.0)
        b = a * b_prev + b
        a = a * a_prev
        dist *= 2
    hs = b + a * h_ref[...]
    h_last = hs[ts - 1:ts, :]
    h_ref[...] = h_last
    nh_ref[0] = h_last
    out_ref[...] = (hs * gy_ref[...]).astype(BF16)


def _rglru(xr, gy, conv_state, h0, cw, cb, wa, ba, wx, bx, lam, nseq, seq_len, ts):
    t, w = xr.shape
    n = seq_len // ts
    row = pl.BlockSpec((ts, w), lambda b, j: (b * n + j, 0))
    vec = pl.BlockSpec((1, w), lambda b, j: (0, 0))
    return pl.pallas_call(
        _rglru_kernel,
        grid=(nseq, n),
        in_specs=[row, row,
                  pl.BlockSpec((1, CONV_WIDTH - 1, w), lambda b, j: (b, 0, 0)),
                  pl.BlockSpec((1, 1, w), lambda b, j: (b, 0, 0)),
                  pl.BlockSpec((CONV_WIDTH, w), lambda b, j: (0, 0)),
                  vec,
                  pl.BlockSpec((w, w), lambda b, j: (0, 0)), vec,
                  pl.BlockSpec((w, w), lambda b, j: (0, 0)), vec,
                  vec],
        out_specs=[row,
                   pl.BlockSpec((1, CONV_WIDTH - 1, w), lambda b, j: (b, 0, 0)),
                   pl.BlockSpec((1, 1, w), lambda b, j: (b, 0, 0))],
        out_shape=[jax.ShapeDtypeStruct((t, w), BF16),
                   jax.ShapeDtypeStruct((nseq, CONV_WIDTH - 1, w), F32),
                   jax.ShapeDtypeStruct((nseq, 1, w), F32)],
        scratch_shapes=[pltpu.VMEM((ts + 8, w), F32), pltpu.VMEM((1, w), F32)],
        compiler_params=_cparams("parallel", "arbitrary"),
        name="rglru",
    )(xr, gy, conv_state, h0.reshape(nseq, 1, w), cw, cb.reshape(1, w), wa, ba.reshape(1, w),
      wx, bx.reshape(1, w), lam.reshape(1, w))


def _sb_scores(q, k):
    return lax.dot_general(q, k, (((1,), (1,)), ((), ())), preferred_element_type=F32)


def _sb_block(z, v, upper, off, acc, mask):
    half = SB_TK // 2
    neg_abs = lax.bitcast_convert_type(
        lax.bitcast_convert_type(z, jnp.int32) | jnp.int32(-2 ** 31), F32)
    sp = jnp.maximum(z, 0.0) + jnp.log2(1.0 + jnp.exp2(neg_abs))
    if mask is not None:
        sp = jnp.where(mask, sp, 0.0)
    spb = sp.astype(BF16)
    sum_new = jnp.sum(sp[:, half:], axis=-1, keepdims=True)
    sum_old = jnp.sum(sp[:, :half], axis=-1, keepdims=True)
    tail_old = jnp.dot(spb[:, :half], upper, preferred_element_type=F32) + (off + sum_new)
    tail_new = jnp.dot(spb[:, half:], upper, preferred_element_type=F32) + off
    later = jnp.concatenate([tail_old, tail_new], axis=1)
    wgt = jnp.exp2((z - sp) - later)
    if mask is not None:
        wgt = jnp.where(mask, wgt, 0.0)
    acc = acc + jnp.dot(wgt.astype(BF16), v, preferred_element_type=F32)
    return off + (sum_new + sum_old), acc


def _sb_kernel(q_ref, k_ref, v_ref, up_ref, o_ref, z_ref, *, tq, q_pos0, full_per_tile, full_base):
    i = pl.program_id(2)
    tk = SB_TK
    q = q_ref[...]
    upper = up_ref[...]
    n_full = full_base + i * full_per_tile
    off = jnp.zeros((tq, 1), F32)
    acc = jnp.zeros((tq, SB_HEAD_DIM), F32)

    def block_start(j):
        return pl.multiple_of(j * tk, tk)

    z_ref[...] = _sb_scores(q, k_ref[pl.ds(block_start(jnp.maximum(n_full - 1, 0)), tk), :])

    start = block_start(n_full)
    q_pos = q_pos0 + i * tq + lax.broadcasted_iota(jnp.int32, (tq, tk), 0)
    mask = (start + lax.broadcasted_iota(jnp.int32, (tq, tk), 1)) < q_pos
    off, acc = _sb_block(_sb_scores(q, k_ref[pl.ds(start, tk), :]), v_ref[pl.ds(start, tk), :],
                         upper, off, acc, mask)

    def body(t, carry):
        off, acc = carry
        j = n_full - 1 - t
        z = z_ref[...]
        z_ref[...] = _sb_scores(q, k_ref[pl.ds(block_start(jnp.maximum(j - 1, 0)), tk), :])
        return _sb_block(z, v_ref[pl.ds(block_start(j), tk), :], upper, off, acc, None)

    off, acc = lax.fori_loop(0, n_full, body, (off, acc))
    o_ref[...] = acc.astype(o_ref.dtype)


def _stick_breaking(q, k_all, v_all, upper, nseq, seq_len, tq, q_pos0):
    t, hw = q.shape
    kpad = k_all.shape[1]
    nq = seq_len // tq
    half = SB_TK // 2
    assert q_pos0 % SB_TK == 0 and kpad % SB_TK == 0
    assert tq == SB_TK or (nq == 1 and tq < SB_TK)
    assert kpad >= q_pos0 + nq * SB_TK
    kern = functools.partial(_sb_kernel, tq=tq, q_pos0=q_pos0, full_per_tile=tq // SB_TK,
                             full_base=q_pos0 // SB_TK)
    qspec = pl.BlockSpec((tq, SB_HEAD_DIM), lambda b, h, i: (b * nq + i, h))
    kvspec = pl.BlockSpec((None, kpad, SB_HEAD_DIM), lambda b, h, i: (b, 0, h))
    return pl.pallas_call(
        kern,
        grid=(nseq, SB_HEADS, nq),
        in_specs=[qspec, kvspec, kvspec, pl.BlockSpec((half, half), lambda b, h, i: (0, 0))],
        out_specs=qspec,
        out_shape=jax.ShapeDtypeStruct((t, hw), BF16),
        scratch_shapes=[pltpu.VMEM((tq, SB_TK), F32)],
        compiler_params=_cparams("parallel", "parallel", "arbitrary"),
        name="stick_breaking",
    )(q, k_all, v_all, upper)


def _outproj_kernel(lru_ref, att_ref, x_ref, mod_ref, wo_ref, g_ref, b_ref, rw_ref, rb_ref,
                    x1_ref, tok_ref, code_ref, gate_ref, cnt_ref, *, alpha):
    i = pl.program_id(0)
    w = lru_ref.shape[-1]
    out = (jnp.dot(lru_ref[...], wo_ref[0:w, :], preferred_element_type=F32)
           + jnp.dot(att_ref[...], wo_ref[w:2 * w, :], preferred_element_type=F32))
    gate1 = mod_ref[0, 2:3, :]
    x1 = _layer_norm(alpha * x_ref[...] + gate1 * out, g_ref[...], b_ref[...])
    x1_ref[...] = x1
    tok = x1 * (1.0 + mod_ref[0, 4:5, :]) + mod_ref[0, 3:4, :]
    tok_ref[...] = tok

    logits = _dot_f32(tok, rw_ref[...]) + rb_ref[...]
    lane = lax.broadcasted_iota(jnp.int32, logits.shape, 1).astype(F32)
    code = jnp.zeros(logits.shape, F32)
    gates = jnp.zeros(logits.shape, F32)
    denom = jnp.zeros((logits.shape[0], 1), F32)
    top0 = None
    for kk in range(TOP_K):
        m = jnp.max(logits, axis=-1, keepdims=True)
        idx = jnp.min(jnp.where(logits == m, lane, float(LANES)), axis=-1, keepdims=True)
        hit = lane == idx
        if kk == 0:
            top0 = m
        e = jnp.exp(m - top0)
        denom = denom + e
        code = jnp.where(hit, float(kk + 1), code)
        gates = jnp.where(lane == kk, e, gates)
        logits = jnp.where(hit, -jnp.inf, logits)
    code_ref[...] = code.astype(BF16)
    gate_ref[...] = gates / denom

    @pl.when(i == 0)
    def _():
        cnt_ref[...] = jnp.zeros(cnt_ref.shape, F32)

    cnt_ref[...] += jnp.sum((code > 0.0).astype(F32), axis=0, keepdims=True)


def _outproj(lru, att, x2, mod, wo_b, g, b, rw_pad, rb_pad, seq0, seq_len, tm, alpha):
    t, d = x2.shape
    w = lru.shape[1]
    tiles_per_seq = seq_len // tm
    rowd = pl.BlockSpec((tm, d), lambda i: (i, 0))
    roww = pl.BlockSpec((tm, w), lambda i: (i, 0))
    rowl = pl.BlockSpec((tm, LANES), lambda i: (i, 0))
    vec = pl.BlockSpec((1, d), lambda i: (0, 0))
    return pl.pallas_call(
        functools.partial(_outproj_kernel, alpha=alpha),
        grid=(t // tm,),
        in_specs=[roww, roww, rowd,
                  pl.BlockSpec((1, 6, d), lambda i: (seq0 + i // tiles_per_seq, 0, 0)),
                  pl.BlockSpec((2 * w, d), lambda i: (0, 0)),
                  vec, vec,
                  pl.BlockSpec((d, LANES), lambda i: (0, 0)),
                  pl.BlockSpec((1, LANES), lambda i: (0, 0))],
        out_specs=[rowd, rowd, rowl, rowl, pl.BlockSpec((8, LANES), lambda i: (0, 0))],
        out_shape=[jax.ShapeDtypeStruct((t, d), F32), jax.ShapeDtypeStruct((t, d), F32),
                   jax.ShapeDtypeStruct((t, LANES), BF16), jax.ShapeDtypeStruct((t, LANES), F32),
                   jax.ShapeDtypeStruct((8, LANES), F32)],
        compiler_params=_cparams("arbitrary"),
        name="outproj_router",
    )(lru, att, x2, mod, wo_b, g.reshape(1, d), b.reshape(1, d), rw_pad, rb_pad)


def _rank_kernel(code_ref, cin_ref, ps_ref, dest_ref, cout_ref, carry_ref, low_ref):
    i = pl.program_id(0)
    tr = code_ref.shape[0]

    @pl.when(i == 0)
    def _():
        carry_ref[...] = cin_ref[0:1, :]
        r = lax.broadcasted_iota(jnp.int32, (tr, tr), 0)
        c = lax.broadcasted_iota(jnp.int32, (tr, tr), 1)
        low_ref[...] = (c < r).astype(BF16)

    code = code_ref[...].astype(F32)
    member = (code > 0.0).astype(BF16)
    before = jnp.dot(low_ref[...], member, preferred_element_type=F32) + carry_ref[...]
    base = before + ps_ref[...]
    lane = lax.broadcasted_iota(jnp.int32, code.shape, 1)
    dest = jnp.zeros(code.shape, jnp.int32)
    for kk in range(TOP_K):
        d = jnp.sum(jnp.where(code == float(kk + 1), base, 0.0), axis=-1, keepdims=True)
        dest = jnp.where(lane == kk, d.astype(jnp.int32), dest)
    dest_ref[...] = dest
    carry = before[tr - 1:tr, :] + (code[tr - 1:tr, :] > 0.0).astype(F32)
    carry_ref[...] = carry
    cout_ref[...] = jnp.broadcast_to(carry, cout_ref.shape)


def _rank(code, carry_in, pad_start, tr):
    t = code.shape[0]
    rowl = pl.BlockSpec((tr, LANES), lambda i: (i, 0))
    small = pl.BlockSpec((8, LANES), lambda i: (0, 0))
    return pl.pallas_call(
        _rank_kernel,
        grid=(t // tr,),
        in_specs=[rowl, small, pl.BlockSpec((1, LANES), lambda i: (0, 0))],
        out_specs=[rowl, small],
        out_shape=[jax.ShapeDtypeStruct((t, LANES), jnp.int32),
                   jax.ShapeDtypeStruct((8, LANES), F32)],
        scratch_shapes=[pltpu.VMEM((1, LANES), F32), pltpu.VMEM((tr, tr), BF16)],
        compiler_params=_cparams("arbitrary"),
        name="moe_rank",
    )(code, carry_in, pad_start)


def _scatter_kernel(dest_ref, tok_ref, rows_in_ref, rows_ref, sem):
    del rows_in_ref
    tm = tok_ref.shape[0]

    def row_copy(t, kk):
        return pltpu.make_async_copy(tok_ref.at[pl.ds(t, 1), :],
                                     rows_ref.at[pl.ds(dest_ref[t * TOP_K + kk], 1), :], sem)

    def start(t, c):
        for kk in range(TOP_K):
            row_copy(t, kk).start()
        return c

    def wait(t, c):
        for kk in range(TOP_K):
            row_copy(t, kk).wait()
        return c

    lax.fori_loop(0, tm, start, 0, unroll=2)
    lax.fori_loop(0, tm, wait, 0, unroll=2)


def _scatter(dest_flat, tok, rows_buf, tm):
    t, d = tok.shape
    return pl.pallas_call(
        _scatter_kernel,
        grid=(t // tm,),
        in_specs=[pl.BlockSpec((tm * TOP_K,), lambda i: (i,), memory_space=pltpu.SMEM),
                  pl.BlockSpec((tm, d), lambda i: (i, 0)),
                  pl.BlockSpec(memory_space=pl.ANY)],
        out_specs=pl.BlockSpec(memory_space=pl.ANY),
        out_shape=jax.ShapeDtypeStruct(rows_buf.shape, rows_buf.dtype),
        scratch_shapes=[pltpu.SemaphoreType.DMA(())],
        input_output_aliases={2: 0},
        compiler_params=_cparams("arbitrary"),
        name="moe_scatter",
    )(dest_flat, tok, rows_buf)


def _expert_kernel(be_ref, x_ref, wgu_ref, bgu_ref, wd_ref, bd_ref, y_ref):
    del be_ref
    dff = wd_ref.shape[1]
    xb = x_ref[...].astype(BF16)
    gu = jnp.dot(xb, wgu_ref[0], preferred_element_type=F32) + bgu_ref[0]
    gate = jnp.minimum(gu[:, :dff], SWIGLU_LIMIT)
    up = jnp.clip(gu[:, dff:], -SWIGLU_LIMIT, SWIGLU_LIMIT)
    act = (up + 1.0) * gate * jax.nn.sigmoid(SWIGLU_ALPHA * gate)
    y_ref[...] = jnp.dot(act.astype(BF16), wd_ref[0], preferred_element_type=F32) + bd_ref[0]


def _experts(block_e, x_rows, wgu_b, bgu, wd_b, bd):
    rows, d = x_rows.shape
    ne, _, n_gu = wgu_b.shape
    dff = wd_b.shape[1]
    nb = rows // MOE_BLOCK
    return pl.pallas_call(
        _expert_kernel,
        grid_spec=pltpu.PrefetchScalarGridSpec(
            num_scalar_prefetch=1,
            grid=(nb,),
            in_specs=[pl.BlockSpec((MOE_BLOCK, d), lambda i, be: (i, 0)),
                      pl.BlockSpec((1, d, n_gu), lambda i, be: (be[i], 0, 0)),
                      pl.BlockSpec((1, 1, n_gu), lambda i, be: (be[i], 0, 0)),
                      pl.BlockSpec((1, dff, d), lambda i, be: (be[i], 0, 0)),
                      pl.BlockSpec((1, 1, d), lambda i, be: (be[i], 0, 0))],
            out_specs=pl.BlockSpec((MOE_BLOCK, d), lambda i, be: (i, 0))),
        out_shape=jax.ShapeDtypeStruct((rows, d), F32),
        compiler_params=_cparams("arbitrary"),
        name="moe_experts",
    )(block_e, x_rows, wgu_b, bgu.reshape(ne, 1, n_gu), wd_b, bd.reshape(ne, 1, d))


def _combine_kernel(dest_ref, y_hbm, gate_ref, x_ref, mod_ref, g_ref, b_ref, o_ref, ybuf, sem, *, alpha):
    tm = x_ref.shape[0]

    def row_copy(t, kk):
        return pltpu.make_async_copy(y_hbm.at[pl.ds(dest_ref[t * TOP_K + kk], 1), :],
                                     ybuf.at[kk, pl.ds(t, 1), :], sem)

    def start(t, c):
        for kk in range(TOP_K):
            row_copy(t, kk).start()
        return c

    def wait(t, c):
        for kk in range(TOP_K):
            row_copy(t, kk).wait()
        return c

    lax.fori_loop(0, tm, start, 0, unroll=2)
    lax.fori_loop(0, tm, wait, 0, unroll=2)

    gates = gate_ref[...]
    f = ybuf[0] * gates[:, 0:1]
    for kk in range(1, TOP_K):
        f = f + ybuf[kk] * gates[:, kk:kk + 1]
    gate2 = mod_ref[0, 5:6, :]
    o_ref[...] = _layer_norm(alpha * x_ref[...] + gate2 * f, g_ref[...], b_ref[...])


def _combine(dest_flat, y_rows, gates, x1, mod, g, b, seq0, seq_len, tm, alpha):
    t, d = x1.shape
    tiles_per_seq = seq_len // tm
    rowd = pl.BlockSpec((tm, d), lambda i: (i, 0))
    vec = pl.BlockSpec((1, d), lambda i: (0, 0))
    return pl.pallas_call(
        functools.partial(_combine_kernel, alpha=alpha),
        grid=(t // tm,),
        in_specs=[pl.BlockSpec((tm * TOP_K,), lambda i: (i,), memory_space=pltpu.SMEM),
                  pl.BlockSpec(memory_space=pl.ANY),
                  pl.BlockSpec((tm, LANES), lambda i: (i, 0)),
                  rowd,
                  pl.BlockSpec((1, 6, d), lambda i: (seq0 + i // tiles_per_seq, 0, 0)),
                  vec, vec],
        out_specs=rowd,
        out_shape=jax.ShapeDtypeStruct((t, d), F32),
        scratch_shapes=[pltpu.VMEM((TOP_K, tm, d), F32), pltpu.SemaphoreType.DMA(())],
        compiler_params=_cparams("arbitrary"),
        name="moe_combine",
    )(dest_flat, y_rows, gates, x1, mod, g.reshape(1, d), b.reshape(1, d))


def _block_diag(wh):
    heads, hd, _ = wh.shape
    eye = jnp.eye(heads, dtype=wh.dtype)
    return jnp.einsum('hij,hg->higj', wh, eye).reshape(heads * hd, heads * hd)


def kernel(x_prompt, x_sample, cache_k, cache_v, state_conv, state_lru, c_prompt, c_sample, ln_in_g, ln_in_b, w_ada, b_ada, w_in, conv_w, conv_b, gate_a_w, gate_a_b, gate_x_w, gate_x_b, lru_lambda, w_out, ln1_g, ln1_b, router_w, router_b, w_gu, b_gu, w_down, b_down, ln2_g, ln2_b):
    nb_p, s_p, d = x_prompt.shape
    nb_s, s_s, _ = x_sample.shape
    depth = w_in.shape[0]
    past = cache_k.shape[2]
    w = state_lru.shape[-1]
    hw = SB_HEADS * SB_HEAD_DIM
    t_p, t_s = nb_p * s_p, nb_s * s_s
    alpha = float((2 * depth) ** 0.25)

    tm_p, tm_s = min(512, s_p), min(512, s_s)
    ts_p, ts_s = min(256, s_p), min(256, s_s)
    tq_p, tq_s = min(SB_TK, s_p), min(SB_TK, s_s)
    tg_p, tg_s = min(256, s_p), min(256, s_s)

    n_seq = nb_p + nb_s
    c_all = jnp.concatenate([c_prompt, c_sample, jnp.zeros((-n_seq % 8, d), F32)], axis=0)
    mod_all = _adaln(c_all, w_ada, b_ada).reshape(depth, c_all.shape[0], 6, d)

    xp = _ln_in(x_prompt.reshape(t_p, d), ln_in_g, ln_in_b, tm_p)
    xs = _ln_in(x_sample.reshape(t_s, d), ln_in_g, ln_in_b, tm_s)

    half = SB_TK // 2
    upper = jnp.asarray(np.arange(half)[:, None] > np.arange(half)[None, :], BF16)
    kpad_s = -(-(past + s_s) // SB_TK) * SB_TK
    zero_conv = jnp.zeros((nb_p, CONV_WIDTH - 1, w), F32)
    zero_h = jnp.zeros((nb_p, w), F32)

    tk_total = (t_p + t_s) * TOP_K
    n_blocks = -(-(tk_total + N_EXPERTS * (MOE_BLOCK - 1)) // MOE_BLOCK)
    n_rows = n_blocks * MOE_BLOCK

    outs = [[] for _ in range(8)]
    for l in range(depth):
        mod = mod_all[l]
        w_in_b = w_in[l].astype(BF16)
        wo_b = w_out[l].astype(BF16)
        wa = _block_diag(gate_a_w[l]).astype(BF16)
        wx = _block_diag(gate_x_w[l]).astype(BF16)
        rw_pad = jnp.pad(router_w[l], ((0, 0), (0, LANES - N_EXPERTS)))
        rb_pad = jnp.pad(router_b[l], (0, LANES - N_EXPERTS), constant_values=NEG_BIG).reshape(1, LANES)
        lru_w = (conv_w[l], conv_b[l], wa, gate_a_b[l], wx, gate_x_b[l], lru_lambda[l])

        xr, gy, q, k, v, kb, vb = _inproj(xp, mod, w_in_b, 0, s_p, tm_p)
        lru_p, nconv_p, nh_p = _rglru(xr, gy, zero_conv, zero_h, *lru_w, nb_p, s_p, ts_p)
        att_p = _stick_breaking(q, kb.reshape(nb_p, s_p, hw), vb.reshape(nb_p, s_p, hw), upper,
                                nb_p, s_p, tq_p, 0)
        outs[0].append(k.reshape(nb_p, s_p, SB_HEADS, SB_HEAD_DIM))
        outs[1].append(v.reshape(nb_p, s_p, SB_HEADS, SB_HEAD_DIM))
        outs[2].append(nconv_p)
        outs[3].append(nh_p.reshape(nb_p, w))

        xr, gy, q, k, v, kb, vb = _inproj(xs, mod, w_in_b, nb_p, s_s, tm_s)
        lru_s, nconv_s, nh_s = _rglru(xr, gy, state_conv[l], state_lru[l], *lru_w, nb_s, s_s, ts_s)
        kv_pad = jnp.zeros((nb_s, kpad_s - past - s_s, hw), BF16)
        k_all = jnp.concatenate([cache_k[l].reshape(nb_s, past, hw).astype(BF16),
                                 kb.reshape(nb_s, s_s, hw), kv_pad], axis=1)
        v_all = jnp.concatenate([cache_v[l].reshape(nb_s, past, hw).astype(BF16),
                                 vb.reshape(nb_s, s_s, hw), kv_pad], axis=1)
        att_s = _stick_breaking(q, k_all, v_all, upper, nb_s, s_s, tq_s, past)
        outs[4].append(k.reshape(nb_s, s_s, SB_HEADS, SB_HEAD_DIM))
        outs[5].append(v.reshape(nb_s, s_s, SB_HEADS, SB_HEAD_DIM))
        outs[6].append(nconv_s)
        outs[7].append(nh_s.reshape(nb_s, w))

        x1_p, tok_p, code_p, gate_p, cnt_p = _outproj(lru_p, att_p, xp, mod, wo_b, ln1_g[l], ln1_b[l],
                                                      rw_pad, rb_pad, 0, s_p, tm_p, alpha)
        x1_s, tok_s, code_s, gate_s, cnt_s = _outproj(lru_s, att_s, xs, mod, wo_b, ln1_g[l], ln1_b[l],
                                                      rw_pad, rb_pad, nb_p, s_s, tm_s, alpha)

        counts = (cnt_p[0, :N_EXPERTS] + cnt_s[0, :N_EXPERTS]).astype(jnp.int32)
        padded = (counts + MOE_BLOCK - 1) // MOE_BLOCK * MOE_BLOCK
        pad_end = jnp.cumsum(padded)
        pad_start = jnp.pad((pad_end - padded).astype(F32), (0, LANES - N_EXPERTS)).reshape(1, LANES)
        block_row0 = jnp.arange(n_blocks, dtype=jnp.int32) * MOE_BLOCK
        block_e = jnp.minimum(jnp.sum(pad_end[None, :] <= block_row0[:, None], axis=1),
                              N_EXPERTS - 1).astype(jnp.int32)

        dest_p, carry = _rank(code_p, jnp.zeros((8, LANES), F32), pad_start, tm_p)
        dest_s, _ = _rank(code_s, carry, pad_start, tm_s)
        dflat_p = dest_p[:, :TOP_K].reshape(t_p * TOP_K)
        dflat_s = dest_s[:, :TOP_K].reshape(t_s * TOP_K)

        x_rows = jnp.zeros((n_rows, d), F32)
        x_rows = _scatter(dflat_p, tok_p, x_rows, tg_p)
        x_rows = _scatter(dflat_s, tok_s, x_rows, tg_s)
        y_rows = _experts(block_e, x_rows, w_gu[l].astype(BF16), b_gu[l], w_down[l].astype(BF16), b_down[l])

        xp = _combine(dflat_p, y_rows, gate_p, x1_p, mod, ln2_g[l], ln2_b[l], 0, s_p, tg_p, alpha)
        xs = _combine(dflat_s, y_rows, gate_s, x1_s, mod, ln2_g[l], ln2_b[l], nb_p, s_s, tg_s, alpha)

    return (xp.reshape(nb_p, s_p, d), xs.reshape(nb_s, s_s, d),
            jnp.stack(outs[0]), jnp.stack(outs[1]), jnp.stack(outs[2]), jnp.stack(outs[3]),
            jnp.stack(outs[4]), jnp.stack(outs[5]), jnp.stack(outs[6]), jnp.stack(outs[7]))
```

```python
import functools

import jax
import jax.numpy as jnp
import numpy as np
from jax import lax
from jax.experimental import pallas as pl
from jax.experimental.pallas import tpu as pltpu

F32 = jnp.float32
BF16 = jnp.bfloat16

LANES = 128
LRU_HEADS = 8
CONV_WIDTH = 4
RG_C = 8.0
SB_HEADS = 4
SB_HEAD_DIM = 128
SB_SCALE = SB_HEAD_DIM ** -0.5
SB_SUB = 256
SB_TQ = 512
SB_TK = 512
LOG2E = 1.4426950408889634
N_EXPERTS = 32
TOP_K = 4
SWIGLU_LIMIT = 7.0
SWIGLU_ALPHA = 1.702
MOE_BLOCK = 256
LN_EPS = 1e-5
NEG_BIG = -1e30
VMEM_LIMIT = 56 * 1024 * 1024


def _cparams(*sem):
    return pltpu.CompilerParams(dimension_semantics=sem, vmem_limit_bytes=VMEM_LIMIT)


def _split3(a):
    hi = a.astype(BF16)
    r1 = a - hi.astype(F32)
    mid = r1.astype(BF16)
    lo = (r1 - mid.astype(F32)).astype(BF16)
    return hi, mid, lo


def _dot_f32(a, b):
    a0, a1, a2 = _split3(a)
    b0, b1, b2 = _split3(b)
    d = functools.partial(jnp.dot, preferred_element_type=F32)
    return (d(a0, b0) + (d(a0, b1) + d(a1, b0))
            + (d(a0, b2) + d(a1, b1) + d(a2, b0)))


def _layer_norm(y, g, b):
    mu = jnp.mean(y, axis=-1, keepdims=True)
    d = y - mu
    var = jnp.mean(d * d, axis=-1, keepdims=True)
    return d * lax.rsqrt(var + LN_EPS) * g + b


def _softplus(z):
    return jnp.maximum(z, 0.0) + jnp.log(1.0 + jnp.exp(-jnp.abs(z)))


def _gelu_tanh(x):
    c = np.sqrt(2.0 / np.pi).astype(np.float32)
    return 0.5 * x * (1.0 + jnp.tanh(c * (x + 0.044715 * (x * x * x))))


def _adaln_kernel(c_ref, w_ref, b_ref, o_ref):
    c = c_ref[...]
    s = c * jax.nn.sigmoid(c)
    o_ref[0] = _dot_f32(s, w_ref[0]) + b_ref[0]


def _adaln(c_all, w_ada, b_ada):
    depth, d, n = w_ada.shape
    rows = c_all.shape[0]
    tn = 1024
    return pl.pallas_call(
        _adaln_kernel,
        grid=(depth, n // tn),
        in_specs=[pl.BlockSpec((rows, d), lambda l, j: (0, 0)),
                  pl.BlockSpec((1, d, tn), lambda l, j: (l, 0, j)),
                  pl.BlockSpec((1, 1, tn), lambda l, j: (l, 0, j))],
        out_specs=pl.BlockSpec((1, rows, tn), lambda l, j: (l, 0, j)),
        out_shape=jax.ShapeDtypeStruct((depth, rows, n), F32),
        compiler_params=_cparams("parallel", "parallel"),
        name="adaln",
    )(c_all, w_ada, b_ada.reshape(depth, 1, n))


def _ln_kernel(x_ref, g_ref, b_ref, o_ref):
    o_ref[...] = _layer_norm(x_ref[...], g_ref[...], b_ref[...])


def _ln_in(x2, g, b, tm):
    t, d = x2.shape
    return pl.pallas_call(
        _ln_kernel,
        grid=(t // tm,),
        in_specs=[pl.BlockSpec((tm, d), lambda i: (i, 0)),
                  pl.BlockSpec((1, d), lambda i: (0, 0)),
                  pl.BlockSpec((1, d), lambda i: (0, 0))],
        out_specs=pl.BlockSpec((tm, d), lambda i: (i, 0)),
        out_shape=jax.ShapeDtypeStruct((t, d), F32),
        compiler_params=_cparams("parallel"),
        name="ln_in",
    )(x2, g.reshape(1, d), b.reshape(1, d))


def _inproj_kernel(x_ref, mod_ref, w_ref, xr_ref, gy_ref, q_ref, k_ref, v_ref, kb_ref, vb_ref):
    w = xr_ref.shape[-1]
    shift = mod_ref[0, 0:1, :]
    scale = mod_ref[0, 1:2, :]
    h = (x_ref[...] * (1.0 + scale) + shift).astype(BF16)

    def proj(j):
        return jnp.dot(h, w_ref[:, j * w:(j + 1) * w], preferred_element_type=F32)

    xr_ref[...] = proj(0)
    gy_ref[...] = _gelu_tanh(proj(1))
    q_ref[...] = (proj(2) * (SB_SCALE * LOG2E)).astype(BF16)
    k = proj(3)
    k_ref[...] = k
    kb_ref[...] = k.astype(BF16)
    v = proj(4)
    v_ref[...] = v
    vb_ref[...] = v.astype(BF16)


def _inproj(x2, mod, w_in_b, seq0, seq_len, tm):
    t, d = x2.shape
    w = w_in_b.shape[1] // 5
    tiles_per_seq = seq_len // tm
    row = pl.BlockSpec((tm, w), lambda i: (i, 0))
    f32o = jax.ShapeDtypeStruct((t, w), F32)
    bf16o = jax.ShapeDtypeStruct((t, w), BF16)
    return pl.pallas_call(
        _inproj_kernel,
        grid=(t // tm,),
        in_specs=[pl.BlockSpec((tm, d), lambda i: (i, 0)),
                  pl.BlockSpec((1, 6, d), lambda i: (seq0 + i // tiles_per_seq, 0, 0)),
                  pl.BlockSpec((d, 5 * w), lambda i: (0, 0))],
        out_specs=[row] * 7,
        out_shape=[f32o, f32o, bf16o, f32o, f32o, bf16o, bf16o],
        compiler_params=_cparams("parallel"),
        name="inproj",
    )(x2, mod, w_in_b)


def _rglru_kernel(xr_ref, gy_ref, cs_ref, h0_ref, cw_ref, cb_ref, wa_ref, ba_ref, wx_ref, bx_ref,
                  lam_ref, out_ref, nconv_ref, nh_ref, xpad_ref, h_ref):
    ts, w = xr_ref.shape
    j = pl.program_id(1)

    @pl.when(j == 0)
    def _():
        xpad_ref[0:8, :] = jnp.zeros((8, w), F32)
        xpad_ref[8 - (CONV_WIDTH - 1):8, :] = cs_ref[0]
        h_ref[...] = h0_ref[0]

    xpad_ref[8:8 + ts, :] = xr_ref[...]
    xc = cb_ref[...]
    for c in range(CONV_WIDTH):
        lo = 8 - (CONV_WIDTH - 1) + c
        xc = xc + xpad_ref[lo:lo + ts, :] * cw_ref[c:c + 1, :]
    nconv_ref[0] = xpad_ref[ts + 8 - (CONV_WIDTH - 1):ts + 8, :]
    xpad_ref[0:8, :] = xpad_ref[ts:ts + 8, :]

    xcb = xc.astype(BF16)
    r = jax.nn.sigmoid(jnp.dot(xcb, wa_ref[...], preferred_element_type=F32) + ba_ref[...])
    gi = jax.nn.sigmoid(jnp.dot(xcb, wx_ref[...], preferred_element_type=F32) + bx_ref[...])
    log_a = (RG_C * r) * (-_softplus(-lam_ref[...]))
    a = jnp.exp(log_a)
    b = jnp.sqrt(1.0 - a * a) * (gi * xc)

    rows = lax.broadcasted_iota(jnp.int32, (ts, w), 0)
    dist = 1
    while dist < ts:
        keep = rows >= dist
        a_prev = jnp.where(keep, pltpu.roll(a, dist, 0), 1.0)
        b_prev = jnp.where(keep, pltpu.roll(b, dist, 0), 0.0)
        b = a * b_prev + b
        a = a * a_prev
        dist *= 2
    hs = b + a * h_ref[...]
    h_last = hs[ts - 1:ts, :]
    h_ref[...] = h_last
    nh_ref[0] = h_last
    out_ref[...] = (hs * gy_ref[...]).astype(BF16)


def _rglru(xr, gy, conv_state, h0, cw, cb, wa, ba, wx, bx, lam, nseq, seq_len, ts):
    t, w = xr.shape
    n = seq_len // ts
    row = pl.BlockSpec((ts, w), lambda b, j: (b * n + j, 0))
    vec = pl.BlockSpec((1, w), lambda b, j: (0, 0))
    return pl.pallas_call(
        _rglru_kernel,
        grid=(nseq, n),
        in_specs=[row, row,
                  pl.BlockSpec((1, CONV_WIDTH - 1, w), lambda b, j: (b, 0, 0)),
                  pl.BlockSpec((1, 1, w), lambda b, j: (b, 0, 0)),
                  pl.BlockSpec((CONV_WIDTH, w), lambda b, j: (0, 0)),
                  vec,
                  pl.BlockSpec((w, w), lambda b, j: (0, 0)), vec,
                  pl.BlockSpec((w, w), lambda b, j: (0, 0)), vec,
                  vec],
        out_specs=[row,
                   pl.BlockSpec((1, CONV_WIDTH - 1, w), lambda b, j: (b, 0, 0)),
                   pl.BlockSpec((1, 1, w), lambda b, j: (b, 0, 0))],
        out_shape=[jax.ShapeDtypeStruct((t, w), BF16),
                   jax.ShapeDtypeStruct((nseq, CONV_WIDTH - 1, w), F32),
                   jax.ShapeDtypeStruct((nseq, 1, w), F32)],
        scratch_shapes=[pltpu.VMEM((ts + 8, w), F32), pltpu.VMEM((1, w), F32)],
        compiler_params=_cparams("parallel", "arbitrary"),
        name="rglru",
    )(xr, gy, conv_state, h0.reshape(nseq, 1, w), cw, cb.reshape(1, w), wa, ba.reshape(1, w),
      wx, bx.reshape(1, w), lam.reshape(1, w))


def _sb_scores(q, k):
    return lax.dot_general(q, k, (((1,), (1,)), ((), ())), preferred_element_type=F32)


def _sb_block(z, v, upper, off, acc, mask):
    neg_abs = lax.bitcast_convert_type(
        lax.bitcast_convert_type(z, jnp.int32) | jnp.int32(-2 ** 31), F32)
    sp = jnp.maximum(z, 0.0) + jnp.log2(1.0 + jnp.exp2(neg_abs))
    if mask is not None:
        sp = jnp.where(mask, sp, 0.0)
    spb = sp.astype(BF16)
    later = []
    for lo in range(SB_TK - SB_SUB, -1, -SB_SUB):
        later.append(jnp.dot(spb[:, lo:lo + SB_SUB], upper, preferred_element_type=F32) + off)
        off = off + jnp.sum(sp[:, lo:lo + SB_SUB], axis=-1, keepdims=True)
    wgt = jnp.exp2((z - sp) - jnp.concatenate(later[::-1], axis=1))
    if mask is not None:
        wgt = jnp.where(mask, wgt, 0.0)
    acc = acc + jnp.dot(wgt.astype(BF16), v, preferred_element_type=F32)
    return off, acc


def _sb_kernel(q_ref, k_ref, v_ref, up_ref, o_ref, z_ref, *, tq, q_pos0):
    i = pl.program_id(2)
    tk = SB_TK
    q = q_ref[...]
    upper = up_ref[...]
    n_full = (q_pos0 + i * tq) // tk
    off = jnp.zeros((tq, 1), F32)
    acc = jnp.zeros((tq, SB_HEAD_DIM), F32)

    def block_start(j):
        return pl.multiple_of(j * tk, tk)

    z_ref[...] = _sb_scores(q, k_ref[pl.ds(block_start(jnp.maximum(n_full - 1, 0)), tk), :])

    start = block_start(n_full)
    q_pos = q_pos0 + i * tq + lax.broadcasted_iota(jnp.int32, (tq, tk), 0)
    mask = (start + lax.broadcasted_iota(jnp.int32, (tq, tk), 1)) < q_pos
    off, acc = _sb_block(_sb_scores(q, k_ref[pl.ds(start, tk), :]), v_ref[pl.ds(start, tk), :],
                         upper, off, acc, mask)

    def body(t, carry):
        off, acc = carry
        j = n_full - 1 - t
        z = z_ref[...]
        z_ref[...] = _sb_scores(q, k_ref[pl.ds(block_start(jnp.maximum(j - 1, 0)), tk), :])
        return _sb_block(z, v_ref[pl.ds(block_start(j), tk), :], upper, off, acc, None)

    off, acc = lax.fori_loop(0, n_full, body, (off, acc))
    o_ref[...] = acc.astype(o_ref.dtype)


def _stick_breaking(q, k_all, v_all, upper, nseq, seq_len, tq, q_pos0):
    t, hw = q.shape
    kpad = k_all.shape[1]
    nq = seq_len // tq
    half = SB_SUB
    assert SB_TK % tq == 0 and q_pos0 % tq == 0 and kpad % SB_TK == 0
    assert kpad >= ((q_pos0 + seq_len - 1) // SB_TK + 1) * SB_TK
    kern = functools.partial(_sb_kernel, tq=tq, q_pos0=q_pos0)
    qspec = pl.BlockSpec((tq, SB_HEAD_DIM), lambda b, h, i: (b * nq + i, h))
    kvspec = pl.BlockSpec((None, kpad, SB_HEAD_DIM), lambda b, h, i: (b, 0, h))
    return pl.pallas_call(
        kern,
        grid=(nseq, SB_HEADS, nq),
        in_specs=[qspec, kvspec, kvspec, pl.BlockSpec((half, half), lambda b, h, i: (0, 0))],
        out_specs=qspec,
        out_shape=jax.ShapeDtypeStruct((t, hw), BF16),
        scratch_shapes=[pltpu.VMEM((tq, SB_TK), F32)],
        compiler_params=_cparams("parallel", "parallel", "arbitrary"),
        name="stick_breaking",
    )(q, k_all, v_all, upper)


def _outproj_kernel(lru_ref, att_ref, x_ref, mod_ref, wo_ref, g_ref, b_ref, rw_ref, rb_ref,
                    x1_ref, tok_ref, code_ref, gate_ref, cnt_ref, *, alpha):
    i = pl.program_id(0)
    w = lru_ref.shape[-1]
    out = (jnp.dot(lru_ref[...], wo_ref[0:w, :], preferred_element_type=F32)
           + jnp.dot(att_ref[...], wo_ref[w:2 * w, :], preferred_element_type=F32))
    gate1 = mod_ref[0, 2:3, :]
    x1 = _layer_norm(alpha * x_ref[...] + gate1 * out, g_ref[...], b_ref[...])
    x1_ref[...] = x1
    tok = x1 * (1.0 + mod_ref[0, 4:5, :]) + mod_ref[0, 3:4, :]
    tok_ref[...] = tok

    logits = _dot_f32(tok, rw_ref[...]) + rb_ref[...]
    lane = lax.broadcasted_iota(jnp.int32, logits.shape, 1).astype(F32)
    code = jnp.zeros(logits.shape, F32)
    gates = jnp.zeros(logits.shape, F32)
    denom = jnp.zeros((logits.shape[0], 1), F32)
    top0 = None
    for kk in range(TOP_K):
        m = jnp.max(logits, axis=-1, keepdims=True)
        idx = jnp.min(jnp.where(logits == m, lane, float(LANES)), axis=-1, keepdims=True)
        hit = lane == idx
        if kk == 0:
            top0 = m
        e = jnp.exp(m - top0)
        denom = denom + e
        code = jnp.where(hit, float(kk + 1), code)
        gates = jnp.where(lane == kk, e, gates)
        logits = jnp.where(hit, -jnp.inf, logits)
    code_ref[...] = code.astype(BF16)
    gate_ref[...] = gates / denom

    @pl.when(i == 0)
    def _():
        cnt_ref[...] = jnp.zeros(cnt_ref.shape, F32)

    cnt_ref[...] += jnp.sum((code > 0.0).astype(F32), axis=0, keepdims=True)


def _outproj(lru, att, x2, mod, wo_b, g, b, rw_pad, rb_pad, seq0, seq_len, tm, alpha):
    t, d = x2.shape
    w = lru.shape[1]
    tiles_per_seq = seq_len // tm
    rowd = pl.BlockSpec((tm, d), lambda i: (i, 0))
    roww = pl.BlockSpec((tm, w), lambda i: (i, 0))
    rowl = pl.BlockSpec((tm, LANES), lambda i: (i, 0))
    vec = pl.BlockSpec((1, d), lambda i: (0, 0))
    return pl.pallas_call(
        functools.partial(_outproj_kernel, alpha=alpha),
        grid=(t // tm,),
        in_specs=[roww, roww, rowd,
                  pl.BlockSpec((1, 6, d), lambda i: (seq0 + i // tiles_per_seq, 0, 0)),
                  pl.BlockSpec((2 * w, d), lambda i: (0, 0)),
                  vec, vec,
                  pl.BlockSpec((d, LANES), lambda i: (0, 0)),
                  pl.BlockSpec((1, LANES), lambda i: (0, 0))],
        out_specs=[rowd, rowd, rowl, rowl, pl.BlockSpec((8, LANES), lambda i: (0, 0))],
        out_shape=[jax.ShapeDtypeStruct((t, d), F32), jax.ShapeDtypeStruct((t, d), F32),
                   jax.ShapeDtypeStruct((t, LANES), BF16), jax.ShapeDtypeStruct((t, LANES), F32),
                   jax.ShapeDtypeStruct((8, LANES), F32)],
        compiler_params=_cparams("arbitrary"),
        name="outproj_router",
    )(lru, att, x2, mod, wo_b, g.reshape(1, d), b.reshape(1, d), rw_pad, rb_pad)


def _rank_kernel(code_ref, cin_ref, ps_ref, dest_ref, cout_ref, carry_ref, low_ref):
    i = pl.program_id(0)
    tr = code_ref.shape[0]

    @pl.when(i == 0)
    def _():
        carry_ref[...] = cin_ref[0:1, :]
        r = lax.broadcasted_iota(jnp.int32, (tr, tr), 0)
        c = lax.broadcasted_iota(jnp.int32, (tr, tr), 1)
        low_ref[...] = (c < r).astype(BF16)

    code = code_ref[...].astype(F32)
    member = (code > 0.0).astype(BF16)
    before = jnp.dot(low_ref[...], member, preferred_element_type=F32) + carry_ref[...]
    base = before + ps_ref[...]
    lane = lax.broadcasted_iota(jnp.int32, code.shape, 1)
    dest = jnp.zeros(code.shape, F32)
    for kk in range(TOP_K):
        d = jnp.sum(jnp.where(code == float(kk + 1), base, 0.0), axis=-1, keepdims=True)
        dest = jnp.where(lane == kk, d, dest)
    dest_ref[...] = jnp.transpose(dest)[0:8, :].astype(jnp.int32)
    carry = before[tr - 1:tr, :] + (code[tr - 1:tr, :] > 0.0).astype(F32)
    carry_ref[...] = carry
    cout_ref[...] = jnp.broadcast_to(carry, cout_ref.shape)


def _rank(code, carry_in, pad_start, tr):
    t = code.shape[0]
    rowl = pl.BlockSpec((tr, LANES), lambda i: (i, 0))
    small = pl.BlockSpec((8, LANES), lambda i: (0, 0))
    return pl.pallas_call(
        _rank_kernel,
        grid=(t // tr,),
        in_specs=[rowl, small, pl.BlockSpec((1, LANES), lambda i: (0, 0))],
        out_specs=[pl.BlockSpec((8, tr), lambda i: (0, i)), small],
        out_shape=[jax.ShapeDtypeStruct((8, t), jnp.int32),
                   jax.ShapeDtypeStruct((8, LANES), F32)],
        scratch_shapes=[pltpu.VMEM((1, LANES), F32), pltpu.VMEM((tr, tr), BF16)],
        compiler_params=_cparams("arbitrary"),
        name="moe_rank",
    )(code, carry_in, pad_start)


def _dest_window(tm):
    width = max(tm, LANES)
    return width, width // tm


def _scatter_kernel(pe_ref, pd_ref, dest_ref, tok_ref, *rest, tm, per, zero_fill):
    if zero_fill:
        rows_ref, sem, zbuf, zsem = rest
    else:
        _, rows_ref, sem = rest
    i = pl.program_id(0)
    base = (i % per) * tm

    if zero_fill:
        @pl.when(i == 0)
        def _():
            zbuf[...] = jnp.zeros(zbuf.shape, zbuf.dtype)
            for e in range(N_EXPERTS):
                @pl.when(pd_ref[e] > 0)
                def _():
                    last = pl.multiple_of(pe_ref[e] - MOE_BLOCK, MOE_BLOCK)
                    cp = pltpu.make_async_copy(zbuf, rows_ref.at[pl.ds(last, MOE_BLOCK)], zsem)
                    cp.start()
                    cp.wait()

    def row_copy(t, kk):
        return pltpu.make_async_copy(tok_ref.at[pl.ds(t, 1), :],
                                     rows_ref.at[pl.ds(dest_ref[kk, base + t], 1), :], sem)

    def start(t, c):
        for kk in range(TOP_K):
            row_copy(t, kk).start()
        return c

    def wait(t, c):
        for kk in range(TOP_K):
            row_copy(t, kk).wait()
        return c

    lax.fori_loop(0, tm, start, 0, unroll=2)
    lax.fori_loop(0, tm, wait, 0, unroll=2)


def _scatter(pad_end, padded, dest, tok, rows_buf, n_rows, tm):
    t, d = tok.shape
    width, per = _dest_window(tm)
    zero_fill = rows_buf is None
    in_specs = [pl.BlockSpec((8, width), lambda i, pe, pd: (0, i // per), memory_space=pltpu.SMEM),
                pl.BlockSpec((tm, d), lambda i, pe, pd: (i, 0))]
    scratch = [pltpu.SemaphoreType.DMA(())]
    args = [pad_end, padded, dest, tok]
    if zero_fill:
        scratch += [pltpu.VMEM((MOE_BLOCK, d), F32), pltpu.SemaphoreType.DMA(())]
    else:
        in_specs.append(pl.BlockSpec(memory_space=pl.ANY))
        args.append(rows_buf)
    return pl.pallas_call(
        functools.partial(_scatter_kernel, tm=tm, per=per, zero_fill=zero_fill),
        grid_spec=pltpu.PrefetchScalarGridSpec(
            num_scalar_prefetch=2,
            grid=(t // tm,),
            in_specs=in_specs,
            out_specs=pl.BlockSpec(memory_space=pl.ANY),
            scratch_shapes=scratch),
        out_shape=jax.ShapeDtypeStruct((n_rows, d), F32),
        input_output_aliases={} if zero_fill else {4: 0},
        compiler_params=_cparams("arbitrary"),
        name="moe_scatter",
    )(*args)


def _expert_kernel(be_ref, nu_ref, x_ref, wgu_ref, bgu_ref, wd_ref, bd_ref, y_ref):
    del be_ref
    dff = wd_ref.shape[1]

    @pl.when(pl.program_id(0) < nu_ref[0])
    def _():
        xb = x_ref[...].astype(BF16)
        gu = jnp.dot(xb, wgu_ref[0], preferred_element_type=F32) + bgu_ref[0]
        gate = jnp.minimum(gu[:, :dff], SWIGLU_LIMIT)
        up = jnp.clip(gu[:, dff:], -SWIGLU_LIMIT, SWIGLU_LIMIT)
        act = (up + 1.0) * gate * jax.nn.sigmoid(SWIGLU_ALPHA * gate)
        y_ref[...] = jnp.dot(act.astype(BF16), wd_ref[0], preferred_element_type=F32) + bd_ref[0]


def _experts(block_e, n_used, x_rows, wgu_b, bgu, wd_b, bd):
    rows, d = x_rows.shape
    ne, _, n_gu = wgu_b.shape
    dff = wd_b.shape[1]
    nb = rows // MOE_BLOCK

    def blk(i, nu):
        return jnp.minimum(i, nu[0] - 1)

    rowspec = pl.BlockSpec((MOE_BLOCK, d), lambda i, be, nu: (blk(i, nu), 0))
    return pl.pallas_call(
        _expert_kernel,
        grid_spec=pltpu.PrefetchScalarGridSpec(
            num_scalar_prefetch=2,
            grid=(nb,),
            in_specs=[rowspec,
                      pl.BlockSpec((1, d, n_gu), lambda i, be, nu: (be[blk(i, nu)], 0, 0)),
                      pl.BlockSpec((1, 1, n_gu), lambda i, be, nu: (be[blk(i, nu)], 0, 0)),
                      pl.BlockSpec((1, dff, d), lambda i, be, nu: (be[blk(i, nu)], 0, 0)),
                      pl.BlockSpec((1, 1, d), lambda i, be, nu: (be[blk(i, nu)], 0, 0))],
            out_specs=rowspec),
        out_shape=jax.ShapeDtypeStruct((rows, d), F32),
        compiler_params=_cparams("arbitrary"),
        name="moe_experts",
    )(block_e, n_used, x_rows, wgu_b, bgu.reshape(ne, 1, n_gu), wd_b, bd.reshape(ne, 1, d))


def _combine_kernel(dest_ref, y_hbm, gate_ref, x_ref, mod_ref, g_ref, b_ref, o_ref, ybuf, sem,
                    *, alpha, per):
    tm = x_ref.shape[0]
    base = (pl.program_id(0) % per) * tm

    def row_copy(t, kk):
        return pltpu.make_async_copy(y_hbm.at[pl.ds(dest_ref[kk, base + t], 1), :],
                                     ybuf.at[kk, pl.ds(t, 1), :], sem)

    def start(t, c):
        for kk in range(TOP_K):
            row_copy(t, kk).start()
        return c

    def wait(t, c):
        for kk in range(TOP_K):
            row_copy(t, kk).wait()
        return c

    lax.fori_loop(0, tm, start, 0, unroll=2)
    lax.fori_loop(0, tm, wait, 0, unroll=2)

    gates = gate_ref[...]
    f = ybuf[0] * gates[:, 0:1]
    for kk in range(1, TOP_K):
        f = f + ybuf[kk] * gates[:, kk:kk + 1]
    gate2 = mod_ref[0, 5:6, :]
    o_ref[...] = _layer_norm(alpha * x_ref[...] + gate2 * f, g_ref[...], b_ref[...])


def _combine(dest, y_rows, gates, x1, mod, g, b, seq0, seq_len, tm, alpha):
    t, d = x1.shape
    tiles_per_seq = seq_len // tm
    width, per = _dest_window(tm)
    rowd = pl.BlockSpec((tm, d), lambda i: (i, 0))
    vec = pl.BlockSpec((1, d), lambda i: (0, 0))
    return pl.pallas_call(
        functools.partial(_combine_kernel, alpha=alpha, per=per),
        grid=(t // tm,),
        in_specs=[pl.BlockSpec((8, width), lambda i: (0, i // per), memory_space=pltpu.SMEM),
                  pl.BlockSpec(memory_space=pl.ANY),
                  pl.BlockSpec((tm, LANES), lambda i: (i, 0)),
                  rowd,
                  pl.BlockSpec((1, 6, d), lambda i: (seq0 + i // tiles_per_seq, 0, 0)),
                  vec, vec],
        out_specs=rowd,
        out_shape=jax.ShapeDtypeStruct((t, d), F32),
        scratch_shapes=[pltpu.VMEM((TOP_K, tm, d), F32), pltpu.SemaphoreType.DMA(())],
        compiler_params=_cparams("arbitrary"),
        name="moe_combine",
    )(dest, y_rows, gates, x1, mod, g.reshape(1, d), b.reshape(1, d))


def _block_diag(wh):
    heads, hd, _ = wh.shape
    eye = jnp.eye(heads, dtype=wh.dtype)
    return jnp.einsum('hij,hg->higj', wh, eye).reshape(heads * hd, heads * hd)


def kernel(x_prompt, x_sample, cache_k, cache_v, state_conv, state_lru, c_prompt, c_sample, ln_in_g, ln_in_b, w_ada, b_ada, w_in, conv_w, conv_b, gate_a_w, gate_a_b, gate_x_w, gate_x_b, lru_lambda, w_out, ln1_g, ln1_b, router_w, router_b, w_gu, b_gu, w_down, b_down, ln2_g, ln2_b):
    nb_p, s_p, d = x_prompt.shape
    nb_s, s_s, _ = x_sample.shape
    depth = w_in.shape[0]
    past = cache_k.shape[2]
    w = state_lru.shape[-1]
    hw = SB_HEADS * SB_HEAD_DIM
    t_p, t_s = nb_p * s_p, nb_s * s_s
    alpha = float((2 * depth) ** 0.25)

    tm_p, tm_s = min(512, s_p), min(512, s_s)
    ts_p, ts_s = min(256, s_p), min(256, s_s)
    tq_p, tq_s = min(SB_TQ, s_p), min(SB_TQ, s_s)
    tg_p, tg_s = min(256, s_p), min(256, s_s)
    tr_p, tr_s = min(512, t_p), min(512, t_s)

    n_seq = nb_p + nb_s
    c_all = jnp.concatenate([c_prompt, c_sample, jnp.zeros((-n_seq % 8, d), F32)], axis=0)
    mod_all = _adaln(c_all, w_ada, b_ada).reshape(depth, c_all.shape[0], 6, d)

    xp = _ln_in(x_prompt.reshape(t_p, d), ln_in_g, ln_in_b, tm_p)
    xs = _ln_in(x_sample.reshape(t_s, d), ln_in_g, ln_in_b, tm_s)

    upper = jnp.asarray(np.arange(SB_SUB)[:, None] > np.arange(SB_SUB)[None, :], BF16)
    kpad_s = -(-(past + s_s) // SB_TK) * SB_TK
    zero_conv = jnp.zeros((nb_p, CONV_WIDTH - 1, w), F32)
    zero_h = jnp.zeros((nb_p, w), F32)

    tk_total = (t_p + t_s) * TOP_K
    n_blocks = -(-(tk_total + N_EXPERTS * (MOE_BLOCK - 1)) // MOE_BLOCK)
    n_rows = n_blocks * MOE_BLOCK

    outs = [[] for _ in range(8)]
    for l in range(depth):
        mod = mod_all[l]
        w_in_b = w_in[l].astype(BF16)
        wo_b = w_out[l].astype(BF16)
        wa = _block_diag(gate_a_w[l]).astype(BF16)
        wx = _block_diag(gate_x_w[l]).astype(BF16)
        rw_pad = jnp.pad(router_w[l], ((0, 0), (0, LANES - N_EXPERTS)))
        rb_pad = jnp.pad(router_b[l], (0, LANES - N_EXPERTS), constant_values=NEG_BIG).reshape(1, LANES)
        lru_w = (conv_w[l], conv_b[l], wa, gate_a_b[l], wx, gate_x_b[l], lru_lambda[l])

        xr, gy, q, k, v, kb, vb = _inproj(xp, mod, w_in_b, 0, s_p, tm_p)
        lru_p, nconv_p, nh_p = _rglru(xr, gy, zero_conv, zero_h, *lru_w, nb_p, s_p, ts_p)
        att_p = _stick_breaking(q, kb.reshape(nb_p, s_p, hw), vb.reshape(nb_p, s_p, hw), upper,
                                nb_p, s_p, tq_p, 0)
        outs[0].append(k.reshape(nb_p, s_p, SB_HEADS, SB_HEAD_DIM))
        outs[1].append(v.reshape(nb_p, s_p, SB_HEADS, SB_HEAD_DIM))
        outs[2].append(nconv_p)
        outs[3].append(nh_p.reshape(nb_p, w))

        xr, gy, q, k, v, kb, vb = _inproj(xs, mod, w_in_b, nb_p, s_s, tm_s)
        lru_s, nconv_s, nh_s = _rglru(xr, gy, state_conv[l], state_lru[l], *lru_w, nb_s, s_s, ts_s)
        kv_pad = jnp.zeros((nb_s, kpad_s - past - s_s, hw), BF16)
        k_all = jnp.concatenate([cache_k[l].reshape(nb_s, past, hw).astype(BF16),
                                 kb.reshape(nb_s, s_s, hw), kv_pad], axis=1)
        v_all = jnp.concatenate([cache_v[l].reshape(nb_s, past, hw).astype(BF16),
                                 vb.reshape(nb_s, s_s, hw), kv_pad], axis=1)
        att_s = _stick_breaking(q, k_all, v_all, upper, nb_s, s_s, tq_s, past)
        outs[4].append(k.reshape(nb_s, s_s, SB_HEADS, SB_HEAD_DIM))
        outs[5].append(v.reshape(nb_s, s_s, SB_HEADS, SB_HEAD_DIM))
        outs[6].append(nconv_s)
        outs[7].append(nh_s.reshape(nb_s, w))

        x1_p, tok_p, code_p, gate_p, cnt_p = _outproj(lru_p, att_p, xp, mod, wo_b, ln1_g[l], ln1_b[l],
                                                      rw_pad, rb_pad, 0, s_p, tm_p, alpha)
        x1_s, tok_s, code_s, gate_s, cnt_s = _outproj(lru_s, att_s, xs, mod, wo_b, ln1_g[l], ln1_b[l],
                                                      rw_pad, rb_pad, nb_p, s_s, tm_s, alpha)

        counts = (cnt_p[0, :N_EXPERTS] + cnt_s[0, :N_EXPERTS]).astype(jnp.int32)
        padded = (counts + MOE_BLOCK - 1) // MOE_BLOCK * MOE_BLOCK
        pad_end = jnp.cumsum(padded)
        pad_start = jnp.pad((pad_end - padded).astype(F32), (0, LANES - N_EXPERTS)).reshape(1, LANES)
        block_row0 = jnp.arange(n_blocks, dtype=jnp.int32) * MOE_BLOCK
        block_e = jnp.minimum(jnp.sum(pad_end[None, :] <= block_row0[:, None], axis=1),
                              N_EXPERTS - 1).astype(jnp.int32)

        n_used = (pad_end[N_EXPERTS - 1:] // MOE_BLOCK).astype(jnp.int32)

        dest_p, carry = _rank(code_p, jnp.zeros((8, LANES), F32), pad_start, tr_p)
        dest_s, _ = _rank(code_s, carry, pad_start, tr_s)

        x_rows = _scatter(pad_end, padded, dest_p, tok_p, None, n_rows, tg_p)
        x_rows = _scatter(pad_end, padded, dest_s, tok_s, x_rows, n_rows, tg_s)
        y_rows = _experts(block_e, n_used, x_rows, w_gu[l].astype(BF16), b_gu[l],
                          w_down[l].astype(BF16), b_down[l])

        xp = _combine(dest_p, y_rows, gate_p, x1_p, mod, ln2_g[l], ln2_b[l], 0, s_p, tg_p, alpha)
        xs = _combine(dest_s, y_rows, gate_s, x1_s, mod, ln2_g[l], ln2_b[l], nb_p, s_s, tg_s, alpha)

    return (xp.reshape(nb_p, s_p, d), xs.reshape(nb_s, s_s, d),
            jnp.stack(outs[0]), jnp.stack(outs[1]), jnp.stack(outs[2]), jnp.stack(outs[3]),
            jnp.stack(outs[4]), jnp.stack(outs[5]), jnp.stack(outs[6]), jnp.stack(outs[7]))
```

```python
import functools

import jax
import jax.numpy as jnp
import numpy as np
from jax import lax
from jax.experimental import pallas as pl
from jax.experimental.pallas import tpu as pltpu

F32 = jnp.float32
BF16 = jnp.bfloat16

LANES = 128
LRU_HEADS = 8
CONV_WIDTH = 4
RG_C = 8.0
SB_HEADS = 4
SB_HEAD_DIM = 128
SB_SCALE = SB_HEAD_DIM ** -0.5
SB_SUB = 256
SB_TQ = 512
SB_TK = 512
SB_DEAD_LOG2 = 160.0
LOG2E = 1.4426950408889634
N_EXPERTS = 32
TOP_K = 4
SWIGLU_LIMIT = 7.0
SWIGLU_ALPHA = 1.702
MOE_BLOCK = 256
LN_EPS = 1e-5
NEG_BIG = -1e30
VMEM_LIMIT = 56 * 1024 * 1024


def _cparams(*sem):
    return pltpu.CompilerParams(dimension_semantics=sem, vmem_limit_bytes=VMEM_LIMIT)


def _split3(a):
    hi = a.astype(BF16)
    r1 = a - hi.astype(F32)
    mid = r1.astype(BF16)
    lo = (r1 - mid.astype(F32)).astype(BF16)
    return hi, mid, lo


def _dot_f32(a, b):
    a0, a1, a2 = _split3(a)
    b0, b1, b2 = _split3(b)
    d = functools.partial(jnp.dot, preferred_element_type=F32)
    return (d(a0, b0) + (d(a0, b1) + d(a1, b0))
            + (d(a0, b2) + d(a1, b1) + d(a2, b0)))


def _layer_norm(y, g, b):
    mu = jnp.mean(y, axis=-1, keepdims=True)
    d = y - mu
    var = jnp.mean(d * d, axis=-1, keepdims=True)
    return d * lax.rsqrt(var + LN_EPS) * g + b


def _softplus(z):
    return jnp.maximum(z, 0.0) + jnp.log(1.0 + jnp.exp(-jnp.abs(z)))


def _gelu_tanh(x):
    c = np.sqrt(2.0 / np.pi).astype(np.float32)
    return 0.5 * x * (1.0 + jnp.tanh(c * (x + 0.044715 * (x * x * x))))


def _adaln_kernel(c_ref, w_ref, b_ref, o_ref):
    c = c_ref[...]
    s = c * jax.nn.sigmoid(c)
    o_ref[0] = _dot_f32(s, w_ref[0]) + b_ref[0]


def _adaln(c_all, w_ada, b_ada):
    depth, d, n = w_ada.shape
    rows = c_all.shape[0]
    tn = 1024
    return pl.pallas_call(
        _adaln_kernel,
        grid=(depth, n // tn),
        in_specs=[pl.BlockSpec((rows, d), lambda l, j: (0, 0)),
                  pl.BlockSpec((1, d, tn), lambda l, j: (l, 0, j)),
                  pl.BlockSpec((1, 1, tn), lambda l, j: (l, 0, j))],
        out_specs=pl.BlockSpec((1, rows, tn), lambda l, j: (l, 0, j)),
        out_shape=jax.ShapeDtypeStruct((depth, rows, n), F32),
        compiler_params=_cparams("parallel", "parallel"),
        name="adaln",
    )(c_all, w_ada, b_ada.reshape(depth, 1, n))


def _ln_kernel(x_ref, g_ref, b_ref, o_ref):
    o_ref[...] = _layer_norm(x_ref[...], g_ref[...], b_ref[...])


def _ln_in(x2, g, b, tm):
    t, d = x2.shape
    return pl.pallas_call(
        _ln_kernel,
        grid=(t // tm,),
        in_specs=[pl.BlockSpec((tm, d), lambda i: (i, 0)),
                  pl.BlockSpec((1, d), lambda i: (0, 0)),
                  pl.BlockSpec((1, d), lambda i: (0, 0))],
        out_specs=pl.BlockSpec((tm, d), lambda i: (i, 0)),
        out_shape=jax.ShapeDtypeStruct((t, d), F32),
        compiler_params=_cparams("parallel"),
        name="ln_in",
    )(x2, g.reshape(1, d), b.reshape(1, d))


def _inproj_kernel(x_ref, mod_ref, w_ref, xr_ref, gy_ref, q_ref, k_ref, v_ref, kb_ref, vb_ref):
    w = xr_ref.shape[-1]
    shift = mod_ref[0, 0:1, :]
    scale = mod_ref[0, 1:2, :]
    h = (x_ref[...] * (1.0 + scale) + shift).astype(BF16)

    def proj(j):
        return jnp.dot(h, w_ref[:, j * w:(j + 1) * w], preferred_element_type=F32)

    xr_ref[...] = proj(0)
    gy_ref[...] = _gelu_tanh(proj(1))
    q_ref[...] = (proj(2) * (SB_SCALE * LOG2E)).astype(BF16)
    k = proj(3)
    k_ref[...] = k
    kb_ref[...] = k.astype(BF16)
    v = proj(4)
    v_ref[...] = v
    vb_ref[...] = v.astype(BF16)


def _inproj(x2, mod, w_in_b, seq0, seq_len, tm):
    t, d = x2.shape
    w = w_in_b.shape[1] // 5
    tiles_per_seq = seq_len // tm
    row = pl.BlockSpec((tm, w), lambda i: (i, 0))
    f32o = jax.ShapeDtypeStruct((t, w), F32)
    bf16o = jax.ShapeDtypeStruct((t, w), BF16)
    return pl.pallas_call(
        _inproj_kernel,
        grid=(t // tm,),
        in_specs=[pl.BlockSpec((tm, d), lambda i: (i, 0)),
                  pl.BlockSpec((1, 6, d), lambda i: (seq0 + i // tiles_per_seq, 0, 0)),
                  pl.BlockSpec((d, 5 * w), lambda i: (0, 0))],
        out_specs=[row] * 7,
        out_shape=[f32o, f32o, bf16o, f32o, f32o, bf16o, bf16o],
        compiler_params=_cparams("parallel"),
        name="inproj",
    )(x2, mod, w_in_b)


def _rglru_kernel(xr_ref, gy_ref, cs_ref, h0_ref, cw_ref, cb_ref, wa_ref, ba_ref, wx_ref, bx_ref,
                  lam_ref, out_ref, nconv_ref, nh_ref, xpad_ref, h_ref):
    ts, w = xr_ref.shape
    j = pl.program_id(1)

    @pl.when(j == 0)
    def _():
        xpad_ref[0:8, :] = jnp.zeros((8, w), F32)
        xpad_ref[8 - (CONV_WIDTH - 1):8, :] = cs_ref[0]
        h_ref[...] = h0_ref[0]

    xpad_ref[8:8 + ts, :] = xr_ref[...]
    xc = cb_ref[...]
    for c in range(CONV_WIDTH):
        lo = 8 - (CONV_WIDTH - 1) + c
        xc = xc + xpad_ref[lo:lo + ts, :] * cw_ref[c:c + 1, :]
    nconv_ref[0] = xpad_ref[ts + 8 - (CONV_WIDTH - 1):ts + 8, :]
    xpad_ref[0:8, :] = xpad_ref[ts:ts + 8, :]

    xcb = xc.astype(BF16)
    r = jax.nn.sigmoid(jnp.dot(xcb, wa_ref[...], preferred_element_type=F32) + ba_ref[...])
    gi = jax.nn.sigmoid(jnp.dot(xcb, wx_ref[...], preferred_element_type=F32) + bx_ref[...])
    log_a = (RG_C * r) * (-_softplus(-lam_ref[...]))
    a = jnp.exp(log_a)
    b = jnp.sqrt(1.0 - a * a) * (gi * xc)

    rows = lax.broadcasted_iota(jnp.int32, (ts, w), 0)
    dist = 1
    while dist < ts:
        keep = rows >= dist
        a_prev = jnp.where(keep, pltpu.roll(a, dist, 0), 1.0)
        b_prev = jnp.where(keep, pltpu.roll(b, dist, 0), 0.0)
        b = a * b_prev + b
        a = a * a_prev
        dist *= 2
    hs = b + a * h_ref[...]
    h_last = hs[ts - 1:ts, :]
    h_ref[...] = h_last
    nh_ref[0] = h_last
    out_ref[...] = (hs * gy_ref[...]).astype(BF16)


def _rglru(xr, gy, conv_state, h0, cw, cb, wa, ba, wx, bx, lam, nseq, seq_len, ts):
    t, w = xr.shape
    n = seq_len // ts
    row = pl.BlockSpec((ts, w), lambda b, j: (b * n + j, 0))
    vec = pl.BlockSpec((1, w), lambda b, j: (0, 0))
    return pl.pallas_call(
        _rglru_kernel,
        grid=(nseq, n),
        in_specs=[row, row,
                  pl.BlockSpec((1, CONV_WIDTH - 1, w), lambda b, j: (b, 0, 0)),
                  pl.BlockSpec((1, 1, w), lambda b, j: (b, 0, 0)),
                  pl.BlockSpec((CONV_WIDTH, w), lambda b, j: (0, 0)),
                  vec,
                  pl.BlockSpec((w, w), lambda b, j: (0, 0)), vec,
                  pl.BlockSpec((w, w), lambda b, j: (0, 0)), vec,
                  vec],
        out_specs=[row,
                   pl.BlockSpec((1, CONV_WIDTH - 1, w), lambda b, j: (b, 0, 0)),
                   pl.BlockSpec((1, 1, w), lambda b, j: (b, 0, 0))],
        out_shape=[jax.ShapeDtypeStruct((t, w), BF16),
                   jax.ShapeDtypeStruct((nseq, CONV_WIDTH - 1, w), F32),
                   jax.ShapeDtypeStruct((nseq, 1, w), F32)],
        scratch_shapes=[pltpu.VMEM((ts + 8, w), F32), pltpu.VMEM((1, w), F32)],
        compiler_params=_cparams("parallel", "arbitrary"),
        name="rglru",
    )(xr, gy, conv_state, h0.reshape(nseq, 1, w), cw, cb.reshape(1, w), wa, ba.reshape(1, w),
      wx, bx.reshape(1, w), lam.reshape(1, w))


def _sb_scores(q, k):
    return lax.dot_general(q, k, (((1,), (1,)), ((), ())), preferred_element_type=F32)


def _sb_block(z, v, upper, off, acc, mask):
    neg_abs = lax.bitcast_convert_type(
        lax.bitcast_convert_type(z, jnp.int32) | jnp.int32(-2 ** 31), F32)
    sp = jnp.maximum(z, 0.0) + jnp.log2(1.0 + jnp.exp2(neg_abs))
    if mask is not None:
        sp = jnp.where(mask, sp, 0.0)
    spb = sp.astype(BF16)
    later = []
    for lo in range(SB_TK - SB_SUB, -1, -SB_SUB):
        later.append(jnp.dot(spb[:, lo:lo + SB_SUB], upper, preferred_element_type=F32) + off)
        off = off + jnp.sum(sp[:, lo:lo + SB_SUB], axis=-1, keepdims=True)
    wgt = jnp.exp2((z - sp) - jnp.concatenate(later[::-1], axis=1))
    if mask is not None:
        wgt = jnp.where(mask, wgt, 0.0)
    acc = acc + jnp.dot(wgt.astype(BF16), v, preferred_element_type=F32)
    return off, acc


def _sb_kernel(q_ref, k_ref, v_ref, up_ref, o_ref, z_ref, *, tq, q_pos0):
    i = pl.program_id(2)
    tk = SB_TK
    q = q_ref[...]
    upper = up_ref[...]
    n_full = (q_pos0 + i * tq) // tk
    off = jnp.zeros((tq, 1), F32)
    acc = jnp.zeros((tq, SB_HEAD_DIM), F32)

    def block_start(j):
        return pl.multiple_of(j * tk, tk)

    z_ref[...] = _sb_scores(q, k_ref[pl.ds(block_start(jnp.maximum(n_full - 1, 0)), tk), :])

    start = block_start(n_full)
    q_pos = q_pos0 + i * tq + lax.broadcasted_iota(jnp.int32, (tq, tk), 0)
    mask = (start + lax.broadcasted_iota(jnp.int32, (tq, tk), 1)) < q_pos
    off, acc = _sb_block(_sb_scores(q, k_ref[pl.ds(start, tk), :]), v_ref[pl.ds(start, tk), :],
                         upper, off, acc, mask)

    def live(off):
        return (jnp.min(off, axis=0, keepdims=True)[0, 0] < SB_DEAD_LOG2).astype(jnp.int32)

    def cond(carry):
        t, alive, _, _ = carry
        return jnp.logical_and(t < n_full, alive > 0)

    def body(carry):
        t, _, off, acc = carry
        j = n_full - 1 - t
        z = z_ref[...]
        z_ref[...] = _sb_scores(q, k_ref[pl.ds(block_start(jnp.maximum(j - 1, 0)), tk), :])
        off, acc = _sb_block(z, v_ref[pl.ds(block_start(j), tk), :], upper, off, acc, None)
        return t + 1, live(off), off, acc

    _, _, off, acc = lax.while_loop(cond, body, (jnp.int32(0), live(off), off, acc))
    o_ref[...] = acc.astype(o_ref.dtype)


def _stick_breaking(q, k_all, v_all, upper, nseq, seq_len, tq, q_pos0):
    t, hw = q.shape
    kpad = k_all.shape[1]
    nq = seq_len // tq
    half = SB_SUB
    assert SB_TK % tq == 0 and q_pos0 % tq == 0 and kpad % SB_TK == 0
    assert kpad >= ((q_pos0 + seq_len - 1) // SB_TK + 1) * SB_TK
    kern = functools.partial(_sb_kernel, tq=tq, q_pos0=q_pos0)
    qspec = pl.BlockSpec((tq, SB_HEAD_DIM), lambda b, h, i: (b * nq + i, h))
    kvspec = pl.BlockSpec((None, kpad, SB_HEAD_DIM), lambda b, h, i: (b, 0, h))
    return pl.pallas_call(
        kern,
        grid=(nseq, SB_HEADS, nq),
        in_specs=[qspec, kvspec, kvspec, pl.BlockSpec((half, half), lambda b, h, i: (0, 0))],
        out_specs=qspec,
        out_shape=jax.ShapeDtypeStruct((t, hw), BF16),
        scratch_shapes=[pltpu.VMEM((tq, SB_TK), F32)],
        compiler_params=_cparams("parallel", "parallel", "arbitrary"),
        name="stick_breaking",
    )(q, k_all, v_all, upper)


def _outproj_kernel(lru_ref, att_ref, x_ref, mod_ref, wo_ref, g_ref, b_ref, rw_ref, rb_ref,
                    x1_ref, tok_ref, code_ref, gate_ref, cnt_ref, *, alpha):
    i = pl.program_id(0)
    w = lru_ref.shape[-1]
    out = (jnp.dot(lru_ref[...], wo_ref[0:w, :], preferred_element_type=F32)
           + jnp.dot(att_ref[...], wo_ref[w:2 * w, :], preferred_element_type=F32))
    gate1 = mod_ref[0, 2:3, :]
    x1 = _layer_norm(alpha * x_ref[...] + gate1 * out, g_ref[...], b_ref[...])
    x1_ref[...] = x1
    tok = x1 * (1.0 + mod_ref[0, 4:5, :]) + mod_ref[0, 3:4, :]
    tok_ref[...] = tok

    logits = _dot_f32(tok, rw_ref[...]) + rb_ref[...]
    lane = lax.broadcasted_iota(jnp.int32, logits.shape, 1).astype(F32)
    code = jnp.zeros(logits.shape, F32)
    gates = jnp.zeros(logits.shape, F32)
    denom = jnp.zeros((logits.shape[0], 1), F32)
    top0 = None
    for kk in range(TOP_K):
        m = jnp.max(logits, axis=-1, keepdims=True)
        idx = jnp.min(jnp.where(logits == m, lane, float(LANES)), axis=-1, keepdims=True)
        hit = lane == idx
        if kk == 0:
            top0 = m
        e = jnp.exp(m - top0)
        denom = denom + e
        code = jnp.where(hit, float(kk + 1), code)
        gates = jnp.where(lane == kk, e, gates)
        logits = jnp.where(hit, -jnp.inf, logits)
    code_ref[...] = code.astype(BF16)
    gate_ref[...] = gates / denom

    @pl.when(i == 0)
    def _():
        cnt_ref[...] = jnp.zeros(cnt_ref.shape, F32)

    cnt_ref[...] += jnp.sum((code > 0.0).astype(F32), axis=0, keepdims=True)


def _outproj(lru, att, x2, mod, wo_b, g, b, rw_pad, rb_pad, seq0, seq_len, tm, alpha):
    t, d = x2.shape
    w = lru.shape[1]
    tiles_per_seq = seq_len // tm
    rowd = pl.BlockSpec((tm, d), lambda i: (i, 0))
    roww = pl.BlockSpec((tm, w), lambda i: (i, 0))
    rowl = pl.BlockSpec((tm, LANES), lambda i: (i, 0))
    vec = pl.BlockSpec((1, d), lambda i: (0, 0))
    return pl.pallas_call(
        functools.partial(_outproj_kernel, alpha=alpha),
        grid=(t // tm,),
        in_specs=[roww, roww, rowd,
                  pl.BlockSpec((1, 6, d), lambda i: (seq0 + i // tiles_per_seq, 0, 0)),
                  pl.BlockSpec((2 * w, d), lambda i: (0, 0)),
                  vec, vec,
                  pl.BlockSpec((d, LANES), lambda i: (0, 0)),
                  pl.BlockSpec((1, LANES), lambda i: (0, 0))],
        out_specs=[rowd, rowd, rowl, rowl, pl.BlockSpec((8, LANES), lambda i: (0, 0))],
        out_shape=[jax.ShapeDtypeStruct((t, d), F32), jax.ShapeDtypeStruct((t, d), F32),
                   jax.ShapeDtypeStruct((t, LANES), BF16), jax.ShapeDtypeStruct((t, LANES), F32),
                   jax.ShapeDtypeStruct((8, LANES), F32)],
        compiler_params=_cparams("arbitrary"),
        name="outproj_router",
    )(lru, att, x2, mod, wo_b, g.reshape(1, d), b.reshape(1, d), rw_pad, rb_pad)


def _rank_kernel(code_ref, cin_ref, ps_ref, dest_ref, cout_ref, carry_ref, low_ref):
    i = pl.program_id(0)
    tr = code_ref.shape[0]

    @pl.when(i == 0)
    def _():
        carry_ref[...] = cin_ref[0:1, :]
        r = lax.broadcasted_iota(jnp.int32, (tr, tr), 0)
        c = lax.broadcasted_iota(jnp.int32, (tr, tr), 1)
        low_ref[...] = (c < r).astype(BF16)

    code = code_ref[...].astype(F32)
    member = (code > 0.0).astype(BF16)
    before = jnp.dot(low_ref[...], member, preferred_element_type=F32) + carry_ref[...]
    base = before + ps_ref[...]
    lane = lax.broadcasted_iota(jnp.int32, code.shape, 1)
    dest = jnp.zeros(code.shape, F32)
    for kk in range(TOP_K):
        d = jnp.sum(jnp.where(code == float(kk + 1), base, 0.0), axis=-1, keepdims=True)
        dest = jnp.where(lane == kk, d, dest)
    dest_ref[...] = jnp.transpose(dest)[0:8, :].astype(jnp.int32)
    carry = before[tr - 1:tr, :] + (code[tr - 1:tr, :] > 0.0).astype(F32)
    carry_ref[...] = carry
    cout_ref[...] = jnp.broadcast_to(carry, cout_ref.shape)


def _rank(code, carry_in, pad_start, tr):
    t = code.shape[0]
    rowl = pl.BlockSpec((tr, LANES), lambda i: (i, 0))
    small = pl.BlockSpec((8, LANES), lambda i: (0, 0))
    return pl.pallas_call(
        _rank_kernel,
        grid=(t // tr,),
        in_specs=[rowl, small, pl.BlockSpec((1, LANES), lambda i: (0, 0))],
        out_specs=[pl.BlockSpec((8, tr), lambda i: (0, i)), small],
        out_shape=[jax.ShapeDtypeStruct((8, t), jnp.int32),
                   jax.ShapeDtypeStruct((8, LANES), F32)],
        scratch_shapes=[pltpu.VMEM((1, LANES), F32), pltpu.VMEM((tr, tr), BF16)],
        compiler_params=_cparams("arbitrary"),
        name="moe_rank",
    )(code, carry_in, pad_start)


def _dest_window(tm):
    width = max(tm, LANES)
    return width, width // tm


def _scatter_kernel(pe_ref, pd_ref, dest_ref, tok_ref, *rest, tm, per, zero_fill):
    if zero_fill:
        rows_ref, sem, zbuf, zsem = rest
    else:
        _, rows_ref, sem = rest
    i = pl.program_id(0)
    base = (i % per) * tm

    if zero_fill:
        @pl.when(i == 0)
        def _():
            zbuf[...] = jnp.zeros(zbuf.shape, zbuf.dtype)
            for e in range(N_EXPERTS):
                @pl.when(pd_ref[e] > 0)
                def _():
                    last = pl.multiple_of(pe_ref[e] - MOE_BLOCK, MOE_BLOCK)
                    cp = pltpu.make_async_copy(zbuf, rows_ref.at[pl.ds(last, MOE_BLOCK)], zsem)
                    cp.start()
                    cp.wait()

    def row_copy(t, kk):
        return pltpu.make_async_copy(tok_ref.at[pl.ds(t, 1), :],
                                     rows_ref.at[pl.ds(dest_ref[kk, base + t], 1), :], sem)

    def start(t, c):
        for kk in range(TOP_K):
            row_copy(t, kk).start()
        return c

    def wait(t, c):
        for kk in range(TOP_K):
            row_copy(t, kk).wait()
        return c

    lax.fori_loop(0, tm, start, 0, unroll=2)
    lax.fori_loop(0, tm, wait, 0, unroll=2)


def _scatter(pad_end, padded, dest, tok, rows_buf, n_rows, tm):
    t, d = tok.shape
    width, per = _dest_window(tm)
    zero_fill = rows_buf is None
    in_specs = [pl.BlockSpec((8, width), lambda i, pe, pd: (0, i // per), memory_space=pltpu.SMEM),
                pl.BlockSpec((tm, d), lambda i, pe, pd: (i, 0))]
    scratch = [pltpu.SemaphoreType.DMA(())]
    args = [pad_end, padded, dest, tok]
    if zero_fill:
        scratch += [pltpu.VMEM((MOE_BLOCK, d), F32), pltpu.SemaphoreType.DMA(())]
    else:
        in_specs.append(pl.BlockSpec(memory_space=pl.ANY))
        args.append(rows_buf)
    return pl.pallas_call(
        functools.partial(_scatter_kernel, tm=tm, per=per, zero_fill=zero_fill),
        grid_spec=pltpu.PrefetchScalarGridSpec(
            num_scalar_prefetch=2,
            grid=(t // tm,),
            in_specs=in_specs,
            out_specs=pl.BlockSpec(memory_space=pl.ANY),
            scratch_shapes=scratch),
        out_shape=jax.ShapeDtypeStruct((n_rows, d), F32),
        input_output_aliases={} if zero_fill else {4: 0},
        compiler_params=_cparams("arbitrary"),
        name="moe_scatter",
    )(*args)


def _expert_kernel(be_ref, nu_ref, x_ref, wgu_ref, bgu_ref, wd_ref, bd_ref, y_ref):
    del be_ref
    dff = wd_ref.shape[1]

    @pl.when(pl.program_id(0) < nu_ref[0])
    def _():
        xb = x_ref[...].astype(BF16)
        gu = jnp.dot(xb, wgu_ref[0], preferred_element_type=F32) + bgu_ref[0]
        gate = jnp.minimum(gu[:, :dff], SWIGLU_LIMIT)
        up = jnp.clip(gu[:, dff:], -SWIGLU_LIMIT, SWIGLU_LIMIT)
        act = (up + 1.0) * gate * jax.nn.sigmoid(SWIGLU_ALPHA * gate)
        y_ref[...] = jnp.dot(act.astype(BF16), wd_ref[0], preferred_element_type=F32) + bd_ref[0]


def _experts(block_e, n_used, x_rows, wgu_b, bgu, wd_b, bd):
    rows, d = x_rows.shape
    ne, _, n_gu = wgu_b.shape
    dff = wd_b.shape[1]
    nb = rows // MOE_BLOCK

    def blk(i, nu):
        return jnp.minimum(i, nu[0] - 1)

    rowspec = pl.BlockSpec((MOE_BLOCK, d), lambda i, be, nu: (blk(i, nu), 0))
    return pl.pallas_call(
        _expert_kernel,
        grid_spec=pltpu.PrefetchScalarGridSpec(
            num_scalar_prefetch=2,
            grid=(nb,),
            in_specs=[rowspec,
                      pl.BlockSpec((1, d, n_gu), lambda i, be, nu: (be[blk(i, nu)], 0, 0)),
                      pl.BlockSpec((1, 1, n_gu), lambda i, be, nu: (be[blk(i, nu)], 0, 0)),
                      pl.BlockSpec((1, dff, d), lambda i, be, nu: (be[blk(i, nu)], 0, 0)),
                      pl.BlockSpec((1, 1, d), lambda i, be, nu: (be[blk(i, nu)], 0, 0))],
            out_specs=rowspec),
        out_shape=jax.ShapeDtypeStruct((rows, d), F32),
        compiler_params=_cparams("arbitrary"),
        name="moe_experts",
    )(block_e, n_used, x_rows, wgu_b, bgu.reshape(ne, 1, n_gu), wd_b, bd.reshape(ne, 1, d))


def _combine_kernel(dest_ref, y_hbm, gate_ref, x_ref, mod_ref, g_ref, b_ref, o_ref, ybuf, sem,
                    *, alpha, per):
    tm = x_ref.shape[0]
    base = (pl.program_id(0) % per) * tm

    def row_copy(t, kk):
        return pltpu.make_async_copy(y_hbm.at[pl.ds(dest_ref[kk, base + t], 1), :],
                                     ybuf.at[kk, pl.ds(t, 1), :], sem)

    def start(t, c):
        for kk in range(TOP_K):
            row_copy(t, kk).start()
        return c

    def wait(t, c):
        for kk in range(TOP_K):
            row_copy(t, kk).wait()
        return c

    lax.fori_loop(0, tm, start, 0, unroll=2)
    lax.fori_loop(0, tm, wait, 0, unroll=2)

    gates = gate_ref[...]
    f = ybuf[0] * gates[:, 0:1]
    for kk in range(1, TOP_K):
        f = f + ybuf[kk] * gates[:, kk:kk + 1]
    gate2 = mod_ref[0, 5:6, :]
    o_ref[...] = _layer_norm(alpha * x_ref[...] + gate2 * f, g_ref[...], b_ref[...])


def _combine(dest, y_rows, gates, x1, mod, g, b, seq0, seq_len, tm, alpha):
    t, d = x1.shape
    tiles_per_seq = seq_len // tm
    width, per = _dest_window(tm)
    rowd = pl.BlockSpec((tm, d), lambda i: (i, 0))
    vec = pl.BlockSpec((1, d), lambda i: (0, 0))
    return pl.pallas_call(
        functools.partial(_combine_kernel, alpha=alpha, per=per),
        grid=(t // tm,),
        in_specs=[pl.BlockSpec((8, width), lambda i: (0, i // per), memory_space=pltpu.SMEM),
                  pl.BlockSpec(memory_space=pl.ANY),
                  pl.BlockSpec((tm, LANES), lambda i: (i, 0)),
                  rowd,
                  pl.BlockSpec((1, 6, d), lambda i: (seq0 + i // tiles_per_seq, 0, 0)),
                  vec, vec],
        out_specs=rowd,
        out_shape=jax.ShapeDtypeStruct((t, d), F32),
        scratch_shapes=[pltpu.VMEM((TOP_K, tm, d), F32), pltpu.SemaphoreType.DMA(())],
        compiler_params=_cparams("arbitrary"),
        name="moe_combine",
    )(dest, y_rows, gates, x1, mod, g.reshape(1, d), b.reshape(1, d))


def _block_diag(wh):
    heads, hd, _ = wh.shape
    eye = jnp.eye(heads, dtype=wh.dtype)
    return jnp.einsum('hij,hg->higj', wh, eye).reshape(heads * hd, heads * hd)


def kernel(x_prompt, x_sample, cache_k, cache_v, state_conv, state_lru, c_prompt, c_sample, ln_in_g, ln_in_b, w_ada, b_ada, w_in, conv_w, conv_b, gate_a_w, gate_a_b, gate_x_w, gate_x_b, lru_lambda, w_out, ln1_g, ln1_b, router_w, router_b, w_gu, b_gu, w_down, b_down, ln2_g, ln2_b):
    nb_p, s_p, d = x_prompt.shape
    nb_s, s_s, _ = x_sample.shape
    depth = w_in.shape[0]
    past = cache_k.shape[2]
    w = state_lru.shape[-1]
    hw = SB_HEADS * SB_HEAD_DIM
    t_p, t_s = nb_p * s_p, nb_s * s_s
    alpha = float((2 * depth) ** 0.25)

    tm_p, tm_s = min(512, s_p), min(512, s_s)
    ts_p, ts_s = min(256, s_p), min(256, s_s)
    tq_p, tq_s = min(SB_TQ, s_p), min(SB_TQ, s_s)
    tg_p, tg_s = min(256, s_p), min(256, s_s)
    tr_p, tr_s = min(512, t_p), min(512, t_s)

    n_seq = nb_p + nb_s
    c_all = jnp.concatenate([c_prompt, c_sample, jnp.zeros((-n_seq % 8, d), F32)], axis=0)
    mod_all = _adaln(c_all, w_ada, b_ada).reshape(depth, c_all.shape[0], 6, d)

    xp = _ln_in(x_prompt.reshape(t_p, d), ln_in_g, ln_in_b, tm_p)
    xs = _ln_in(x_sample.reshape(t_s, d), ln_in_g, ln_in_b, tm_s)

    upper = jnp.asarray(np.arange(SB_SUB)[:, None] > np.arange(SB_SUB)[None, :], BF16)
    kpad_s = -(-(past + s_s) // SB_TK) * SB_TK
    zero_conv = jnp.zeros((nb_p, CONV_WIDTH - 1, w), F32)
    zero_h = jnp.zeros((nb_p, w), F32)

    tk_total = (t_p + t_s) * TOP_K
    n_blocks = -(-(tk_total + N_EXPERTS * (MOE_BLOCK - 1)) // MOE_BLOCK)
    n_rows = n_blocks * MOE_BLOCK

    outs = [[] for _ in range(8)]
    for l in range(depth):
        mod = mod_all[l]
        w_in_b = w_in[l].astype(BF16)
        wo_b = w_out[l].astype(BF16)
        wa = _block_diag(gate_a_w[l]).astype(BF16)
        wx = _block_diag(gate_x_w[l]).astype(BF16)
        rw_pad = jnp.pad(router_w[l], ((0, 0), (0, LANES - N_EXPERTS)))
        rb_pad = jnp.pad(router_b[l], (0, LANES - N_EXPERTS), constant_values=NEG_BIG).reshape(1, LANES)
        lru_w = (conv_w[l], conv_b[l], wa, gate_a_b[l], wx, gate_x_b[l], lru_lambda[l])

        xr, gy, q, k, v, kb, vb = _inproj(xp, mod, w_in_b, 0, s_p, tm_p)
        lru_p, nconv_p, nh_p = _rglru(xr, gy, zero_conv, zero_h, *lru_w, nb_p, s_p, ts_p)
        att_p = _stick_breaking(q, kb.reshape(nb_p, s_p, hw), vb.reshape(nb_p, s_p, hw), upper,
                                nb_p, s_p, tq_p, 0)
        outs[0].append(k.reshape(nb_p, s_p, SB_HEADS, SB_HEAD_DIM))
        outs[1].append(v.reshape(nb_p, s_p, SB_HEADS, SB_HEAD_DIM))
        outs[2].append(nconv_p)
        outs[3].append(nh_p.reshape(nb_p, w))

        xr, gy, q, k, v, kb, vb = _inproj(xs, mod, w_in_b, nb_p, s_s, tm_s)
        lru_s, nconv_s, nh_s = _rglru(xr, gy, state_conv[l], state_lru[l], *lru_w, nb_s, s_s, ts_s)
        kv_pad = jnp.zeros((nb_s, kpad_s - past - s_s, hw), BF16)
        k_all = jnp.concatenate([cache_k[l].reshape(nb_s, past, hw).astype(BF16),
                                 kb.reshape(nb_s, s_s, hw), kv_pad], axis=1)
        v_all = jnp.concatenate([cache_v[l].reshape(nb_s, past, hw).astype(BF16),
                                 vb.reshape(nb_s, s_s, hw), kv_pad], axis=1)
        att_s = _stick_breaking(q, k_all, v_all, upper, nb_s, s_s, tq_s, past)
        outs[4].append(k.reshape(nb_s, s_s, SB_HEADS, SB_HEAD_DIM))
        outs[5].append(v.reshape(nb_s, s_s, SB_HEADS, SB_HEAD_DIM))
        outs[6].append(nconv_s)
        outs[7].append(nh_s.reshape(nb_s, w))

        x1_p, tok_p, code_p, gate_p, cnt_p = _outproj(lru_p, att_p, xp, mod, wo_b, ln1_g[l], ln1_b[l],
                                                      rw_pad, rb_pad, 0, s_p, tm_p, alpha)
        x1_s, tok_s, code_s, gate_s, cnt_s = _outproj(lru_s, att_s, xs, mod, wo_b, ln1_g[l], ln1_b[l],
                                                      rw_pad, rb_pad, nb_p, s_s, tm_s, alpha)

        counts = (cnt_p[0, :N_EXPERTS] + cnt_s[0, :N_EXPERTS]).astype(jnp.int32)
        padded = (counts + MOE_BLOCK - 1) // MOE_BLOCK * MOE_BLOCK
        pad_end = jnp.cumsum(padded)
        pad_start = jnp.pad((pad_end - padded).astype(F32), (0, LANES - N_EXPERTS)).reshape(1, LANES)
        block_row0 = jnp.arange(n_blocks, dtype=jnp.int32) * MOE_BLOCK
        block_e = jnp.minimum(jnp.sum(pad_end[None, :] <= block_row0[:, None], axis=1),
                              N_EXPERTS - 1).astype(jnp.int32)

        n_used = (pad_end[N_EXPERTS - 1:] // MOE_BLOCK).astype(jnp.int32)

        dest_p, carry = _rank(code_p, jnp.zeros((8, LANES), F32), pad_start, tr_p)
        dest_s, _ = _rank(code_s, carry, pad_start, tr_s)

        x_rows = _scatter(pad_end, padded, dest_p, tok_p, None, n_rows, tg_p)
        x_rows = _scatter(pad_end, padded, dest_s, tok_s, x_rows, n_rows, tg_s)
        y_rows = _experts(block_e, n_used, x_rows, w_gu[l].astype(BF16), b_gu[l],
                          w_down[l].astype(BF16), b_down[l])

        xp = _combine(dest_p, y_rows, gate_p, x1_p, mod, ln2_g[l], ln2_b[l], 0, s_p, tg_p, alpha)
        xs = _combine(dest_s, y_rows, gate_s, x1_s, mod, ln2_g[l], ln2_b[l], nb_p, s_s, tg_s, alpha)

    return (xp.reshape(nb_p, s_p, d), xs.reshape(nb_s, s_s, d),
            jnp.stack(outs[0]), jnp.stack(outs[1]), jnp.stack(outs[2]), jnp.stack(outs[3]),
            jnp.stack(outs[4]), jnp.stack(outs[5]), jnp.stack(outs[6]), jnp.stack(outs[7]))
```

```python
import functools

import jax
import jax.numpy as jnp
import numpy as np
from jax import lax
from jax.experimental import pallas as pl
from jax.experimental.pallas import tpu as pltpu

F32 = jnp.float32
BF16 = jnp.bfloat16

LANES = 128
SUBLANES = 8
LRU_HEADS = 8
CONV_WIDTH = 4
RG_C = 8.0
SB_HEADS = 4
SB_HEAD_DIM = 128
SB_SCALE = SB_HEAD_DIM ** -0.5
SB_SUB = 256
SB_TQ = 512
SB_TK = 512
SB_DEAD_LOG2 = 160.0
LOG2E = 1.4426950408889634
N_EXPERTS = 32
TOP_K = 4
SWIGLU_LIMIT = 7.0
SWIGLU_ALPHA = 1.702
MOE_BLOCK = 512
LN_EPS = 1e-5
NEG_BIG = -1e30
VMEM_LIMIT = 56 * 1024 * 1024


def _cparams(*sem):
    return pltpu.CompilerParams(dimension_semantics=sem, vmem_limit_bytes=VMEM_LIMIT)


def _split3(a):
    hi = a.astype(BF16)
    r1 = a - hi.astype(F32)
    mid = r1.astype(BF16)
    lo = (r1 - mid.astype(F32)).astype(BF16)
    return hi, mid, lo


def _dot_f32(a, b):
    a0, a1, a2 = _split3(a)
    b0, b1, b2 = _split3(b)
    d = functools.partial(jnp.dot, preferred_element_type=F32)
    return (d(a0, b0) + (d(a0, b1) + d(a1, b0))
            + (d(a0, b2) + d(a1, b1) + d(a2, b0)))


def _layer_norm(y, g, b):
    mu = jnp.mean(y, axis=-1, keepdims=True)
    d = y - mu
    var = jnp.mean(d * d, axis=-1, keepdims=True)
    return d * lax.rsqrt(var + LN_EPS) * g + b


def _softplus(z):
    return jnp.maximum(z, 0.0) + jnp.log(1.0 + jnp.exp(-jnp.abs(z)))


def _gelu_tanh(x):
    c = np.sqrt(2.0 / np.pi).astype(np.float32)
    return 0.5 * x * (1.0 + jnp.tanh(c * (x + 0.044715 * (x * x * x))))


def _adaln_kernel(c_ref, w_ref, b_ref, o_ref):
    c = c_ref[...]
    s = c * jax.nn.sigmoid(c)
    o_ref[0] = _dot_f32(s, w_ref[0]) + b_ref[0]


def _adaln(c_all, w_ada, b_ada):
    depth, d, n = w_ada.shape
    rows = c_all.shape[0]
    tn = 1024
    return pl.pallas_call(
        _adaln_kernel,
        grid=(depth, n // tn),
        in_specs=[pl.BlockSpec((rows, d), lambda l, j: (0, 0)),
                  pl.BlockSpec((1, d, tn), lambda l, j: (l, 0, j)),
                  pl.BlockSpec((1, 1, tn), lambda l, j: (l, 0, j))],
        out_specs=pl.BlockSpec((1, rows, tn), lambda l, j: (l, 0, j)),
        out_shape=jax.ShapeDtypeStruct((depth, rows, n), F32),
        compiler_params=_cparams("parallel", "parallel"),
        name="adaln",
    )(c_all, w_ada, b_ada.reshape(depth, 1, n))


def _ln_kernel(x_ref, g_ref, b_ref, o_ref):
    o_ref[...] = _layer_norm(x_ref[...], g_ref[...], b_ref[...])


def _ln_in(x2, g, b, tm):
    t, d = x2.shape
    return pl.pallas_call(
        _ln_kernel,
        grid=(t // tm,),
        in_specs=[pl.BlockSpec((tm, d), lambda i: (i, 0)),
                  pl.BlockSpec((1, d), lambda i: (0, 0)),
                  pl.BlockSpec((1, d), lambda i: (0, 0))],
        out_specs=pl.BlockSpec((tm, d), lambda i: (i, 0)),
        out_shape=jax.ShapeDtypeStruct((t, d), F32),
        compiler_params=_cparams("parallel"),
        name="ln_in",
    )(x2, g.reshape(1, d), b.reshape(1, d))


def _inproj_kernel(x_ref, mod_ref, w_ref, xr_ref, gy_ref, q_ref, k_ref, v_ref, kb_ref, vb_ref):
    w = xr_ref.shape[-1]
    shift = mod_ref[0, 0:1, :]
    scale = mod_ref[0, 1:2, :]
    h = (x_ref[...] * (1.0 + scale) + shift).astype(BF16)

    def proj(j):
        return jnp.dot(h, w_ref[:, j * w:(j + 1) * w], preferred_element_type=F32)

    xr_ref[...] = proj(0)
    gy_ref[...] = _gelu_tanh(proj(1))
    q_ref[...] = (proj(2) * (SB_SCALE * LOG2E)).astype(BF16)
    k = proj(3)
    k_ref[...] = k
    kb_ref[...] = k.astype(BF16)
    v = proj(4)
    v_ref[...] = v
    vb_ref[...] = v.astype(BF16)


def _inproj(x2, mod, w_in_b, seq0, seq_len, tm):
    t, d = x2.shape
    w = w_in_b.shape[1] // 5
    tiles_per_seq = seq_len // tm
    row = pl.BlockSpec((tm, w), lambda i: (i, 0))
    f32o = jax.ShapeDtypeStruct((t, w), F32)
    bf16o = jax.ShapeDtypeStruct((t, w), BF16)
    return pl.pallas_call(
        _inproj_kernel,
        grid=(t // tm,),
        in_specs=[pl.BlockSpec((tm, d), lambda i: (i, 0)),
                  pl.BlockSpec((1, 6, d), lambda i: (seq0 + i // tiles_per_seq, 0, 0)),
                  pl.BlockSpec((d, 5 * w), lambda i: (0, 0))],
        out_specs=[row] * 7,
        out_shape=[f32o, f32o, bf16o, f32o, f32o, bf16o, bf16o],
        compiler_params=_cparams("parallel"),
        name="inproj",
    )(x2, mod, w_in_b)


def _rglru_kernel(xr_ref, gy_ref, cs_ref, h0_ref, cw_ref, cb_ref, wa_ref, ba_ref, wx_ref, bx_ref,
                  lam_ref, out_ref, nconv_ref, nh_ref, tail_ref, h_ref):
    ts, w = xr_ref.shape
    j = pl.program_id(1)

    @pl.when(j == 0)
    def _():
        tail_ref[...] = jnp.zeros((SUBLANES, w), F32)
        tail_ref[SUBLANES - (CONV_WIDTH - 1):SUBLANES, :] = cs_ref[0]
        h_ref[...] = h0_ref[0]

    xr = xr_ref[...]
    prev = tail_ref[...]
    row8 = lax.broadcasted_iota(jnp.int32, (SUBLANES, w), 0)
    xc = cb_ref[...] + xr * cw_ref[CONV_WIDTH - 1:CONV_WIDTH, :]
    for dly in range(1, CONV_WIDTH):
        sh = pltpu.roll(xr, dly, 0)
        head = jnp.where(row8 < dly, pltpu.roll(prev, dly, 0), sh[0:SUBLANES, :])
        sh = jnp.concatenate([head, sh[SUBLANES:, :]], axis=0)
        xc = xc + sh * cw_ref[CONV_WIDTH - 1 - dly:CONV_WIDTH - dly, :]
    nconv_ref[0] = xr[ts - (CONV_WIDTH - 1):ts, :]
    tail_ref[...] = xr[ts - SUBLANES:ts, :]

    xcb = xc.astype(BF16)
    r = jax.nn.sigmoid(jnp.dot(xcb, wa_ref[...], preferred_element_type=F32) + ba_ref[...])
    gi = jax.nn.sigmoid(jnp.dot(xcb, wx_ref[...], preferred_element_type=F32) + bx_ref[...])
    log_a = (RG_C * r) * (-_softplus(-lam_ref[...]))
    a = jnp.exp(log_a)
    one_m = 1.0 - a * a
    b = jnp.where(one_m > 0.0, one_m * lax.rsqrt(one_m), 0.0) * (gi * xc)

    sub = lax.broadcasted_iota(jnp.int32, (ts, w), 0) & (SUBLANES - 1)
    dist = 1
    while dist < SUBLANES:
        keep = sub >= dist
        a_prev = jnp.where(keep, pltpu.roll(a, dist, 0), 1.0)
        b_prev = jnp.where(keep, pltpu.roll(b, dist, 0), 0.0)
        b = a * b_prev + b
        a = a * a_prev
        dist *= 2
    carry = h_ref[...]
    groups = []
    for g0 in range(0, ts, SUBLANES):
        hg = b[g0:g0 + SUBLANES, :] + a[g0:g0 + SUBLANES, :] * carry
        groups.append(hg)
        carry = hg[SUBLANES - 1:SUBLANES, :]
    hs = jnp.concatenate(groups, axis=0)
    h_last = carry
    h_ref[...] = h_last
    nh_ref[0] = h_last
    out_ref[...] = (hs * gy_ref[...]).astype(BF16)


def _rglru(xr, gy, conv_state, h0, cw, cb, wa, ba, wx, bx, lam, nseq, seq_len, ts):
    t, w = xr.shape
    n = seq_len // ts
    row = pl.BlockSpec((ts, w), lambda b, j: (b * n + j, 0))
    vec = pl.BlockSpec((1, w), lambda b, j: (0, 0))
    return pl.pallas_call(
        _rglru_kernel,
        grid=(nseq, n),
        in_specs=[row, row,
                  pl.BlockSpec((1, CONV_WIDTH - 1, w), lambda b, j: (b, 0, 0)),
                  pl.BlockSpec((1, 1, w), lambda b, j: (b, 0, 0)),
                  pl.BlockSpec((CONV_WIDTH, w), lambda b, j: (0, 0)),
                  vec,
                  pl.BlockSpec((w, w), lambda b, j: (0, 0)), vec,
                  pl.BlockSpec((w, w), lambda b, j: (0, 0)), vec,
                  vec],
        out_specs=[row,
                   pl.BlockSpec((1, CONV_WIDTH - 1, w), lambda b, j: (b, 0, 0)),
                   pl.BlockSpec((1, 1, w), lambda b, j: (b, 0, 0))],
        out_shape=[jax.ShapeDtypeStruct((t, w), BF16),
                   jax.ShapeDtypeStruct((nseq, CONV_WIDTH - 1, w), F32),
                   jax.ShapeDtypeStruct((nseq, 1, w), F32)],
        scratch_shapes=[pltpu.VMEM((SUBLANES, w), F32), pltpu.VMEM((1, w), F32)],
        compiler_params=_cparams("parallel", "arbitrary"),
        name="rglru",
    )(xr, gy, conv_state, h0.reshape(nseq, 1, w), cw, cb.reshape(1, w), wa, ba.reshape(1, w),
      wx, bx.reshape(1, w), lam.reshape(1, w))


def _sb_scores(q, k):
    return lax.dot_general(q, k, (((1,), (1,)), ((), ())), preferred_element_type=F32)


def _sb_block(z, v, upper, off, acc, mask):
    neg_abs = lax.bitcast_convert_type(
        lax.bitcast_convert_type(z, jnp.int32) | jnp.int32(-2 ** 31), F32)
    sp = jnp.maximum(z, 0.0) + jnp.log2(1.0 + jnp.exp2(neg_abs))
    if mask is not None:
        sp = jnp.where(mask, sp, 0.0)
    spb = sp.astype(BF16)
    later = []
    for lo in range(SB_TK - SB_SUB, -1, -SB_SUB):
        later.append(jnp.dot(spb[:, lo:lo + SB_SUB], upper, preferred_element_type=F32) + off)
        off = off + jnp.sum(sp[:, lo:lo + SB_SUB], axis=-1, keepdims=True)
    wgt = jnp.exp2((z - sp) - jnp.concatenate(later[::-1], axis=1))
    if mask is not None:
        wgt = jnp.where(mask, wgt, 0.0)
    acc = acc + jnp.dot(wgt.astype(BF16), v, preferred_element_type=F32)
    return off, acc


def _sb_kernel(q_ref, k_ref, v_ref, up_ref, o_ref, z_ref, *, tq, q_pos0):
    i = pl.program_id(2)
    tk = SB_TK
    q = q_ref[...]
    upper = up_ref[...]
    n_full = (q_pos0 + i * tq) // tk
    off = jnp.zeros((tq, 1), F32)
    acc = jnp.zeros((tq, SB_HEAD_DIM), F32)

    def block_start(j):
        return pl.multiple_of(j * tk, tk)

    z_ref[...] = _sb_scores(q, k_ref[pl.ds(block_start(jnp.maximum(n_full - 1, 0)), tk), :])

    start = block_start(n_full)
    q_pos = q_pos0 + i * tq + lax.broadcasted_iota(jnp.int32, (tq, tk), 0)
    mask = (start + lax.broadcasted_iota(jnp.int32, (tq, tk), 1)) < q_pos
    off, acc = _sb_block(_sb_scores(q, k_ref[pl.ds(start, tk), :]), v_ref[pl.ds(start, tk), :],
                         upper, off, acc, mask)

    def live(off):
        return (jnp.min(off, axis=0, keepdims=True)[0, 0] < SB_DEAD_LOG2).astype(jnp.int32)

    def cond(carry):
        t, alive, _, _ = carry
        return jnp.logical_and(t < n_full, alive > 0)

    def body(carry):
        t, _, off, acc = carry
        j = n_full - 1 - t
        z = z_ref[...]
        z_ref[...] = _sb_scores(q, k_ref[pl.ds(block_start(jnp.maximum(j - 1, 0)), tk), :])
        off, acc = _sb_block(z, v_ref[pl.ds(block_start(j), tk), :], upper, off, acc, None)
        return t + 1, live(off), off, acc

    _, _, off, acc = lax.while_loop(cond, body, (jnp.int32(0), live(off), off, acc))
    o_ref[...] = acc.astype(o_ref.dtype)


def _stick_breaking(q, k_all, v_all, upper, nseq, seq_len, tq, q_pos0):
    t, hw = q.shape
    kpad = k_all.shape[1]
    nq = seq_len // tq
    half = SB_SUB
    assert SB_TK % tq == 0 and q_pos0 % tq == 0 and kpad % SB_TK == 0
    assert kpad >= ((q_pos0 + seq_len - 1) // SB_TK + 1) * SB_TK
    kern = functools.partial(_sb_kernel, tq=tq, q_pos0=q_pos0)
    qspec = pl.BlockSpec((tq, SB_HEAD_DIM), lambda b, h, i: (b * nq + i, h))
    kvspec = pl.BlockSpec((None, kpad, SB_HEAD_DIM), lambda b, h, i: (b, 0, h))
    return pl.pallas_call(
        kern,
        grid=(nseq, SB_HEADS, nq),
        in_specs=[qspec, kvspec, kvspec, pl.BlockSpec((half, half), lambda b, h, i: (0, 0))],
        out_specs=qspec,
        out_shape=jax.ShapeDtypeStruct((t, hw), BF16),
        scratch_shapes=[pltpu.VMEM((tq, SB_TK), F32)],
        compiler_params=_cparams("parallel", "parallel", "arbitrary"),
        name="stick_breaking",
    )(q, k_all, v_all, upper)


def _outproj_kernel(lru_ref, att_ref, x_ref, mod_ref, wo_ref, g_ref, b_ref, rw_ref, rb_ref,
                    x1_ref, tok_ref, code_ref, gate_ref, cnt_ref, *, alpha):
    i = pl.program_id(0)
    w = lru_ref.shape[-1]
    out = (jnp.dot(lru_ref[...], wo_ref[0:w, :], preferred_element_type=F32)
           + jnp.dot(att_ref[...], wo_ref[w:2 * w, :], preferred_element_type=F32))
    gate1 = mod_ref[0, 2:3, :]
    x1 = _layer_norm(alpha * x_ref[...] + gate1 * out, g_ref[...], b_ref[...])
    x1_ref[...] = x1
    tok = x1 * (1.0 + mod_ref[0, 4:5, :]) + mod_ref[0, 3:4, :]
    tok_ref[...] = tok

    logits = _dot_f32(tok, rw_ref[...]) + rb_ref[...]
    lane = lax.broadcasted_iota(jnp.int32, logits.shape, 1).astype(F32)
    code = jnp.zeros(logits.shape, F32)
    gates = jnp.zeros(logits.shape, F32)
    denom = jnp.zeros((logits.shape[0], 1), F32)
    top0 = None
    for kk in range(TOP_K):
        m = jnp.max(logits, axis=-1, keepdims=True)
        idx = jnp.min(jnp.where(logits == m, lane, float(LANES)), axis=-1, keepdims=True)
        hit = lane == idx
        if kk == 0:
            top0 = m
        e = jnp.exp(m - top0)
        denom = denom + e
        code = jnp.where(hit, float(kk + 1), code)
        gates = jnp.where(lane == kk, e, gates)
        logits = jnp.where(hit, -jnp.inf, logits)
    code_ref[...] = code.astype(BF16)
    gate_ref[...] = gates / denom

    @pl.when(i == 0)
    def _():
        cnt_ref[...] = jnp.zeros(cnt_ref.shape, F32)

    cnt_ref[...] += jnp.sum((code > 0.0).astype(F32), axis=0, keepdims=True)


def _outproj(lru, att, x2, mod, wo_b, g, b, rw_pad, rb_pad, seq0, seq_len, tm, alpha):
    t, d = x2.shape
    w = lru.shape[1]
    tiles_per_seq = seq_len // tm
    rowd = pl.BlockSpec((tm, d), lambda i: (i, 0))
    roww = pl.BlockSpec((tm, w), lambda i: (i, 0))
    rowl = pl.BlockSpec((tm, LANES), lambda i: (i, 0))
    vec = pl.BlockSpec((1, d), lambda i: (0, 0))
    return pl.pallas_call(
        functools.partial(_outproj_kernel, alpha=alpha),
        grid=(t // tm,),
        in_specs=[roww, roww, rowd,
                  pl.BlockSpec((1, 6, d), lambda i: (seq0 + i // tiles_per_seq, 0, 0)),
                  pl.BlockSpec((2 * w, d), lambda i: (0, 0)),
                  vec, vec,
                  pl.BlockSpec((d, LANES), lambda i: (0, 0)),
                  pl.BlockSpec((1, LANES), lambda i: (0, 0))],
        out_specs=[rowd, rowd, rowl, rowl, pl.BlockSpec((8, LANES), lambda i: (0, 0))],
        out_shape=[jax.ShapeDtypeStruct((t, d), F32), jax.ShapeDtypeStruct((t, d), F32),
                   jax.ShapeDtypeStruct((t, LANES), BF16), jax.ShapeDtypeStruct((t, LANES), F32),
                   jax.ShapeDtypeStruct((8, LANES), F32)],
        compiler_params=_cparams("arbitrary"),
        name="outproj_router",
    )(lru, att, x2, mod, wo_b, g.reshape(1, d), b.reshape(1, d), rw_pad, rb_pad)


def _rank_kernel(code_ref, cin_ref, ps_ref, dest_ref, cout_ref, carry_ref, low_ref):
    i = pl.program_id(0)
    tr = code_ref.shape[0]

    @pl.when(i == 0)
    def _():
        carry_ref[...] = cin_ref[0:1, :]
        r = lax.broadcasted_iota(jnp.int32, (tr, tr), 0)
        c = lax.broadcasted_iota(jnp.int32, (tr, tr), 1)
        low_ref[...] = (c < r).astype(BF16)

    code = code_ref[...].astype(F32)
    member = (code > 0.0).astype(BF16)
    before = jnp.dot(low_ref[...], member, preferred_element_type=F32) + carry_ref[...]
    base = before + ps_ref[...]
    lane = lax.broadcasted_iota(jnp.int32, code.shape, 1)
    dest = jnp.zeros(code.shape, F32)
    for kk in range(TOP_K):
        d = jnp.sum(jnp.where(code == float(kk + 1), base, 0.0), axis=-1, keepdims=True)
        dest = jnp.where(lane == kk, d, dest)
    dest_ref[...] = jnp.transpose(dest)[0:8, :].astype(jnp.int32)
    carry = before[tr - 1:tr, :] + (code[tr - 1:tr, :] > 0.0).astype(F32)
    carry_ref[...] = carry
    cout_ref[...] = jnp.broadcast_to(carry, cout_ref.shape)


def _rank(code, carry_in, pad_start, tr):
    t = code.shape[0]
    rowl = pl.BlockSpec((tr, LANES), lambda i: (i, 0))
    small = pl.BlockSpec((8, LANES), lambda i: (0, 0))
    return pl.pallas_call(
        _rank_kernel,
        grid=(t // tr,),
        in_specs=[rowl, small, pl.BlockSpec((1, LANES), lambda i: (0, 0))],
        out_specs=[pl.BlockSpec((8, tr), lambda i: (0, i)), small],
        out_shape=[jax.ShapeDtypeStruct((8, t), jnp.int32),
                   jax.ShapeDtypeStruct((8, LANES), F32)],
        scratch_shapes=[pltpu.VMEM((1, LANES), F32), pltpu.VMEM((tr, tr), BF16)],
        compiler_params=_cparams("arbitrary"),
        name="moe_rank",
    )(code, carry_in, pad_start)


def _dest_window(tm):
    width = max(tm, LANES)
    return width, width // tm


def _scatter_kernel(pe_ref, pd_ref, dest_ref, tok_ref, *rest, tm, per, zero_fill):
    if zero_fill:
        rows_ref, sem, zbuf, zsem = rest
    else:
        _, rows_ref, sem = rest
    i = pl.program_id(0)
    base = (i % per) * tm

    if zero_fill:
        @pl.when(i == 0)
        def _():
            zbuf[...] = jnp.zeros(zbuf.shape, zbuf.dtype)
            for e in range(N_EXPERTS):
                @pl.when(pd_ref[e] > 0)
                def _():
                    last = pl.multiple_of(pe_ref[e] - MOE_BLOCK, MOE_BLOCK)
                    cp = pltpu.make_async_copy(zbuf, rows_ref.at[pl.ds(last, MOE_BLOCK)], zsem)
                    cp.start()
                    cp.wait()

    def row_copy(t, kk):
        return pltpu.make_async_copy(tok_ref.at[pl.ds(t, 1), :],
                                     rows_ref.at[pl.ds(dest_ref[kk, base + t], 1), :], sem)

    def start(t, c):
        for kk in range(TOP_K):
            row_copy(t, kk).start()
        return c

    def wait(t, c):
        for kk in range(TOP_K):
            row_copy(t, kk).wait()
        return c

    lax.fori_loop(0, tm, start, 0, unroll=2)
    lax.fori_loop(0, tm, wait, 0, unroll=2)


def _scatter(pad_end, padded, dest, tok, rows_buf, n_rows, tm):
    t, d = tok.shape
    width, per = _dest_window(tm)
    zero_fill = rows_buf is None
    in_specs = [pl.BlockSpec((8, width), lambda i, pe, pd: (0, i // per), memory_space=pltpu.SMEM),
                pl.BlockSpec((tm, d), lambda i, pe, pd: (i, 0))]
    scratch = [pltpu.SemaphoreType.DMA(())]
    args = [pad_end, padded, dest, tok]
    if zero_fill:
        scratch += [pltpu.VMEM((MOE_BLOCK, d), F32), pltpu.SemaphoreType.DMA(())]
    else:
        in_specs.append(pl.BlockSpec(memory_space=pl.ANY))
        args.append(rows_buf)
    return pl.pallas_call(
        functools.partial(_scatter_kernel, tm=tm, per=per, zero_fill=zero_fill),
        grid_spec=pltpu.PrefetchScalarGridSpec(
            num_scalar_prefetch=2,
            grid=(t // tm,),
            in_specs=in_specs,
            out_specs=pl.BlockSpec(memory_space=pl.ANY),
            scratch_shapes=scratch),
        out_shape=jax.ShapeDtypeStruct((n_rows, d), F32),
        input_output_aliases={} if zero_fill else {4: 0},
        compiler_params=_cparams("arbitrary"),
        name="moe_scatter",
    )(*args)


def _expert_kernel(be_ref, nu_ref, x_ref, wgu_ref, bgu_ref, wd_ref, bd_ref, y_ref, wgu_b, wd_b):
    dff = wd_ref.shape[1]
    i = pl.program_id(0)

    @pl.when(i < nu_ref[0])
    def _():
        @pl.when(jnp.logical_or(i == 0, be_ref[i] != be_ref[jnp.maximum(i - 1, 0)]))
        def _():
            wgu_b[...] = wgu_ref[0].astype(BF16)
            wd_b[...] = wd_ref[0].astype(BF16)

        xb = x_ref[...].astype(BF16)
        gu = jnp.dot(xb, wgu_b[...], preferred_element_type=F32) + bgu_ref[0]
        gate = jnp.minimum(gu[:, :dff], SWIGLU_LIMIT)
        up = jnp.clip(gu[:, dff:], -SWIGLU_LIMIT, SWIGLU_LIMIT)
        act = (up + 1.0) * gate * jax.nn.sigmoid(SWIGLU_ALPHA * gate)
        y_ref[...] = jnp.dot(act.astype(BF16), wd_b[...], preferred_element_type=F32) + bd_ref[0]


def _experts(block_e, n_used, x_rows, wgu, bgu, wd, bd):
    rows, d = x_rows.shape
    ne, _, n_gu = wgu.shape
    dff = wd.shape[1]
    nb = rows // MOE_BLOCK

    def blk(i, nu):
        return jnp.minimum(i, nu[0] - 1)

    rowspec = pl.BlockSpec((MOE_BLOCK, d), lambda i, be, nu: (blk(i, nu), 0))
    return pl.pallas_call(
        _expert_kernel,
        grid_spec=pltpu.PrefetchScalarGridSpec(
            num_scalar_prefetch=2,
            grid=(nb,),
            in_specs=[rowspec,
                      pl.BlockSpec((1, d, n_gu), lambda i, be, nu: (be[blk(i, nu)], 0, 0)),
                      pl.BlockSpec((1, 1, n_gu), lambda i, be, nu: (be[blk(i, nu)], 0, 0)),
                      pl.BlockSpec((1, dff, d), lambda i, be, nu: (be[blk(i, nu)], 0, 0)),
                      pl.BlockSpec((1, 1, d), lambda i, be, nu: (be[blk(i, nu)], 0, 0))],
            out_specs=rowspec,
            scratch_shapes=[pltpu.VMEM((d, n_gu), BF16), pltpu.VMEM((dff, d), BF16)]),
        out_shape=jax.ShapeDtypeStruct((rows, d), F32),
        compiler_params=_cparams("arbitrary"),
        name="moe_experts",
    )(block_e, n_used, x_rows, wgu, bgu.reshape(ne, 1, n_gu), wd, bd.reshape(ne, 1, d))


def _combine_kernel(dest_ref, y_hbm, gate_ref, x_ref, mod_ref, g_ref, b_ref, o_ref, ybuf, sem,
                    *, alpha, per):
    tm = x_ref.shape[0]
    base = (pl.program_id(0) % per) * tm

    def row_copy(t, kk):
        return pltpu.make_async_copy(y_hbm.at[pl.ds(dest_ref[kk, base + t], 1), :],
                                     ybuf.at[kk, pl.ds(t, 1), :], sem)

    def start(t, c):
        for kk in range(TOP_K):
            row_copy(t, kk).start()
        return c

    def wait(t, c):
        for kk in range(TOP_K):
            row_copy(t, kk).wait()
        return c

    lax.fori_loop(0, tm, start, 0, unroll=2)
    lax.fori_loop(0, tm, wait, 0, unroll=2)

    gates = gate_ref[...]
    f = ybuf[0] * gates[:, 0:1]
    for kk in range(1, TOP_K):
        f = f + ybuf[kk] * gates[:, kk:kk + 1]
    gate2 = mod_ref[0, 5:6, :]
    o_ref[...] = _layer_norm(alpha * x_ref[...] + gate2 * f, g_ref[...], b_ref[...])


def _combine(dest, y_rows, gates, x1, mod, g, b, seq0, seq_len, tm, alpha):
    t, d = x1.shape
    tiles_per_seq = seq_len // tm
    width, per = _dest_window(tm)
    rowd = pl.BlockSpec((tm, d), lambda i: (i, 0))
    vec = pl.BlockSpec((1, d), lambda i: (0, 0))
    return pl.pallas_call(
        functools.partial(_combine_kernel, alpha=alpha, per=per),
        grid=(t // tm,),
        in_specs=[pl.BlockSpec((8, width), lambda i: (0, i // per), memory_space=pltpu.SMEM),
                  pl.BlockSpec(memory_space=pl.ANY),
                  pl.BlockSpec((tm, LANES), lambda i: (i, 0)),
                  rowd,
                  pl.BlockSpec((1, 6, d), lambda i: (seq0 + i // tiles_per_seq, 0, 0)),
                  vec, vec],
        out_specs=rowd,
        out_shape=jax.ShapeDtypeStruct((t, d), F32),
        scratch_shapes=[pltpu.VMEM((TOP_K, tm, d), F32), pltpu.SemaphoreType.DMA(())],
        compiler_params=_cparams("arbitrary"),
        name="moe_combine",
    )(dest, y_rows, gates, x1, mod, g.reshape(1, d), b.reshape(1, d))


def _block_diag(wh):
    heads, hd, _ = wh.shape
    eye = jnp.eye(heads, dtype=wh.dtype)
    return jnp.einsum('hij,hg->higj', wh, eye).reshape(heads * hd, heads * hd)


def kernel(x_prompt, x_sample, cache_k, cache_v, state_conv, state_lru, c_prompt, c_sample, ln_in_g, ln_in_b, w_ada, b_ada, w_in, conv_w, conv_b, gate_a_w, gate_a_b, gate_x_w, gate_x_b, lru_lambda, w_out, ln1_g, ln1_b, router_w, router_b, w_gu, b_gu, w_down, b_down, ln2_g, ln2_b):
    nb_p, s_p, d = x_prompt.shape
    nb_s, s_s, _ = x_sample.shape
    depth = w_in.shape[0]
    past = cache_k.shape[2]
    w = state_lru.shape[-1]
    hw = SB_HEADS * SB_HEAD_DIM
    t_p, t_s = nb_p * s_p, nb_s * s_s
    alpha = float((2 * depth) ** 0.25)

    tm_p, tm_s = min(512, s_p), min(512, s_s)
    ts_p, ts_s = min(256, s_p), min(256, s_s)
    tq_p, tq_s = min(SB_TQ, s_p), min(SB_TQ, s_s)
    tg_p, tg_s = min(256, s_p), min(256, s_s)
    tr_p, tr_s = min(512, t_p), min(512, t_s)

    n_seq = nb_p + nb_s
    c_all = jnp.concatenate([c_prompt, c_sample, jnp.zeros((-n_seq % 8, d), F32)], axis=0)
    mod_all = _adaln(c_all, w_ada, b_ada).reshape(depth, c_all.shape[0], 6, d)

    xp = _ln_in(x_prompt.reshape(t_p, d), ln_in_g, ln_in_b, tm_p)
    xs = _ln_in(x_sample.reshape(t_s, d), ln_in_g, ln_in_b, tm_s)

    upper = jnp.asarray(np.arange(SB_SUB)[:, None] > np.arange(SB_SUB)[None, :], BF16)
    kpad_s = -(-(past + s_s) // SB_TK) * SB_TK
    zero_conv = jnp.zeros((nb_p, CONV_WIDTH - 1, w), F32)
    zero_h = jnp.zeros((nb_p, w), F32)

    tk_total = (t_p + t_s) * TOP_K
    n_blocks = -(-(tk_total + N_EXPERTS * (MOE_BLOCK - 1)) // MOE_BLOCK)
    n_rows = n_blocks * MOE_BLOCK

    outs = [[] for _ in range(8)]
    for l in range(depth):
        mod = mod_all[l]
        w_in_b = w_in[l].astype(BF16)
        wo_b = w_out[l].astype(BF16)
        wa = _block_diag(gate_a_w[l]).astype(BF16)
        wx = _block_diag(gate_x_w[l]).astype(BF16)
        rw_pad = jnp.pad(router_w[l], ((0, 0), (0, LANES - N_EXPERTS)))
        rb_pad = jnp.pad(router_b[l], (0, LANES - N_EXPERTS), constant_values=NEG_BIG).reshape(1, LANES)
        lru_w = (conv_w[l], conv_b[l], wa, gate_a_b[l], wx, gate_x_b[l], lru_lambda[l])

        xr, gy, q, k, v, kb, vb = _inproj(xp, mod, w_in_b, 0, s_p, tm_p)
        lru_p, nconv_p, nh_p = _rglru(xr, gy, zero_conv, zero_h, *lru_w, nb_p, s_p, ts_p)
        att_p = _stick_breaking(q, kb.reshape(nb_p, s_p, hw), vb.reshape(nb_p, s_p, hw), upper,
                                nb_p, s_p, tq_p, 0)
        outs[0].append(k.reshape(nb_p, s_p, SB_HEADS, SB_HEAD_DIM))
        outs[1].append(v.reshape(nb_p, s_p, SB_HEADS, SB_HEAD_DIM))
        outs[2].append(nconv_p)
        outs[3].append(nh_p.reshape(nb_p, w))

        xr, gy, q, k, v, kb, vb = _inproj(xs, mod, w_in_b, nb_p, s_s, tm_s)
        lru_s, nconv_s, nh_s = _rglru(xr, gy, state_conv[l], state_lru[l], *lru_w, nb_s, s_s, ts_s)
        kv_pad = jnp.zeros((nb_s, kpad_s - past - s_s, hw), BF16)
        k_all = jnp.concatenate([cache_k[l].reshape(nb_s, past, hw).astype(BF16),
                                 kb.reshape(nb_s, s_s, hw), kv_pad], axis=1)
        v_all = jnp.concatenate([cache_v[l].reshape(nb_s, past, hw).astype(BF16),
                                 vb.reshape(nb_s, s_s, hw), kv_pad], axis=1)
        att_s = _stick_breaking(q, k_all, v_all, upper, nb_s, s_s, tq_s, past)
        outs[4].append(k.reshape(nb_s, s_s, SB_HEADS, SB_HEAD_DIM))
        outs[5].append(v.reshape(nb_s, s_s, SB_HEADS, SB_HEAD_DIM))
        outs[6].append(nconv_s)
        outs[7].append(nh_s.reshape(nb_s, w))

        x1_p, tok_p, code_p, gate_p, cnt_p = _outproj(lru_p, att_p, xp, mod, wo_b, ln1_g[l], ln1_b[l],
                                                      rw_pad, rb_pad, 0, s_p, tm_p, alpha)
        x1_s, tok_s, code_s, gate_s, cnt_s = _outproj(lru_s, att_s, xs, mod, wo_b, ln1_g[l], ln1_b[l],
                                                      rw_pad, rb_pad, nb_p, s_s, tm_s, alpha)

        counts = (cnt_p[0, :N_EXPERTS] + cnt_s[0, :N_EXPERTS]).astype(jnp.int32)
        padded = (counts + MOE_BLOCK - 1) // MOE_BLOCK * MOE_BLOCK
        pad_end = jnp.cumsum(padded)
        pad_start = jnp.pad((pad_end - padded).astype(F32), (0, LANES - N_EXPERTS)).reshape(1, LANES)
        block_row0 = jnp.arange(n_blocks, dtype=jnp.int32) * MOE_BLOCK
        block_e = jnp.minimum(jnp.sum(pad_end[None, :] <= block_row0[:, None], axis=1),
                              N_EXPERTS - 1).astype(jnp.int32)

        n_used = (pad_end[N_EXPERTS - 1:] // MOE_BLOCK).astype(jnp.int32)

        dest_p, carry = _rank(code_p, jnp.zeros((8, LANES), F32), pad_start, tr_p)
        dest_s, _ = _rank(code_s, carry, pad_start, tr_s)

        x_rows = _scatter(pad_end, padded, dest_p, tok_p, None, n_rows, tg_p)
        x_rows = _scatter(pad_end, padded, dest_s, tok_s, x_rows, n_rows, tg_s)
        y_rows = _experts(block_e, n_used, x_rows, w_gu[l], b_gu[l], w_down[l], b_down[l])

        xp = _combine(dest_p, y_rows, gate_p, x1_p, mod, ln2_g[l], ln2_b[l], 0, s_p, tg_p, alpha)
        xs = _combine(dest_s, y_rows, gate_s, x1_s, mod, ln2_g[l], ln2_b[l], nb_p, s_s, tg_s, alpha)

    return (xp.reshape(nb_p, s_p, d), xs.reshape(nb_s, s_s, d),
            jnp.stack(outs[0]), jnp.stack(outs[1]), jnp.stack(outs[2]), jnp.stack(outs[3]),
            jnp.stack(outs[4]), jnp.stack(outs[5]), jnp.stack(outs[6]), jnp.stack(outs[7]))
```

```python
import functools

import jax
import jax.numpy as jnp
import numpy as np
from jax import lax
from jax.experimental import pallas as pl
from jax.experimental.pallas import tpu as pltpu
from jax.experimental.pallas import tpu_sc as plsc

F32 = jnp.float32
BF16 = jnp.bfloat16

LANES = 128
SUBLANES = 8
LRU_HEADS = 8
CONV_WIDTH = 4
RG_C = 8.0
SB_HEADS = 4
SB_HEAD_DIM = 128
SB_SCALE = SB_HEAD_DIM ** -0.5
SB_SUB = 256
SB_TQ = 512
SB_TK = 512
SB_DEAD_LOG2 = 160.0
LOG2E = 1.4426950408889634
N_EXPERTS = 32
TOP_K = 4
SWIGLU_LIMIT = 7.0
SWIGLU_ALPHA = 1.702
MOE_BLOCK = 512
SC_CORES, SC_SUBCORES = 2, 16
SC_WINDOW = 32
LN_EPS = 1e-5
NEG_BIG = -1e30
VMEM_LIMIT = 56 * 1024 * 1024


def _cparams(*sem):
    return pltpu.CompilerParams(dimension_semantics=sem, vmem_limit_bytes=VMEM_LIMIT)


def _split3(a):
    hi = a.astype(BF16)
    r1 = a - hi.astype(F32)
    mid = r1.astype(BF16)
    lo = (r1 - mid.astype(F32)).astype(BF16)
    return hi, mid, lo


def _dot_f32(a, b):
    a0, a1, a2 = _split3(a)
    b0, b1, b2 = _split3(b)
    d = functools.partial(jnp.dot, preferred_element_type=F32)
    return (d(a0, b0) + (d(a0, b1) + d(a1, b0))
            + (d(a0, b2) + d(a1, b1) + d(a2, b0)))


def _layer_norm(y, g, b):
    mu = jnp.mean(y, axis=-1, keepdims=True)
    d = y - mu
    var = jnp.mean(d * d, axis=-1, keepdims=True)
    return d * lax.rsqrt(var + LN_EPS) * g + b


def _softplus(z):
    return jnp.maximum(z, 0.0) + jnp.log(1.0 + jnp.exp(-jnp.abs(z)))


def _gelu_tanh(x):
    c = np.sqrt(2.0 / np.pi).astype(np.float32)
    return 0.5 * x * (1.0 + jnp.tanh(c * (x + 0.044715 * (x * x * x))))


def _adaln_kernel(c_ref, w_ref, b_ref, o_ref):
    c = c_ref[...]
    s = c * jax.nn.sigmoid(c)
    o_ref[0] = _dot_f32(s, w_ref[0]) + b_ref[0]


def _adaln(c_all, w_ada, b_ada):
    depth, d, n = w_ada.shape
    rows = c_all.shape[0]
    tn = 1024
    return pl.pallas_call(
        _adaln_kernel,
        grid=(depth, n // tn),
        in_specs=[pl.BlockSpec((rows, d), lambda l, j: (0, 0)),
                  pl.BlockSpec((1, d, tn), lambda l, j: (l, 0, j)),
                  pl.BlockSpec((1, 1, tn), lambda l, j: (l, 0, j))],
        out_specs=pl.BlockSpec((1, rows, tn), lambda l, j: (l, 0, j)),
        out_shape=jax.ShapeDtypeStruct((depth, rows, n), F32),
        compiler_params=_cparams("parallel", "parallel"),
        name="adaln",
    )(c_all, w_ada, b_ada.reshape(depth, 1, n))


def _ln_kernel(x_ref, g_ref, b_ref, o_ref):
    o_ref[...] = _layer_norm(x_ref[...], g_ref[...], b_ref[...])


def _ln_in(x2, g, b, tm):
    t, d = x2.shape
    return pl.pallas_call(
        _ln_kernel,
        grid=(t // tm,),
        in_specs=[pl.BlockSpec((tm, d), lambda i: (i, 0)),
                  pl.BlockSpec((1, d), lambda i: (0, 0)),
                  pl.BlockSpec((1, d), lambda i: (0, 0))],
        out_specs=pl.BlockSpec((tm, d), lambda i: (i, 0)),
        out_shape=jax.ShapeDtypeStruct((t, d), F32),
        compiler_params=_cparams("parallel"),
        name="ln_in",
    )(x2, g.reshape(1, d), b.reshape(1, d))


def _inproj_kernel(x_ref, mod_ref, w_ref, xr_ref, gy_ref, q_ref, k_ref, v_ref, kb_ref, vb_ref):
    w = xr_ref.shape[-1]
    shift = mod_ref[0, 0:1, :]
    scale = mod_ref[0, 1:2, :]
    h = (x_ref[...] * (1.0 + scale) + shift).astype(BF16)

    def proj(j):
        return jnp.dot(h, w_ref[:, j * w:(j + 1) * w], preferred_element_type=F32)

    xr_ref[...] = proj(0)
    gy_ref[...] = _gelu_tanh(proj(1))
    q_ref[...] = (proj(2) * (SB_SCALE * LOG2E)).astype(BF16)
    k = proj(3)
    k_ref[...] = k
    kb_ref[...] = k.astype(BF16)
    v = proj(4)
    v_ref[...] = v
    vb_ref[...] = v.astype(BF16)


def _inproj(x2, mod, w_in_b, seq0, seq_len, tm):
    t, d = x2.shape
    w = w_in_b.shape[1] // 5
    tiles_per_seq = seq_len // tm
    row = pl.BlockSpec((tm, w), lambda i: (i, 0))
    f32o = jax.ShapeDtypeStruct((t, w), F32)
    bf16o = jax.ShapeDtypeStruct((t, w), BF16)
    return pl.pallas_call(
        _inproj_kernel,
        grid=(t // tm,),
        in_specs=[pl.BlockSpec((tm, d), lambda i: (i, 0)),
                  pl.BlockSpec((1, 6, d), lambda i: (seq0 + i // tiles_per_seq, 0, 0)),
                  pl.BlockSpec((d, 5 * w), lambda i: (0, 0))],
        out_specs=[row] * 7,
        out_shape=[f32o, f32o, bf16o, f32o, f32o, bf16o, bf16o],
        compiler_params=_cparams("parallel"),
        name="inproj",
    )(x2, mod, w_in_b)


def _rglru_kernel(xr_ref, gy_ref, cs_ref, h0_ref, cw_ref, cb_ref, wa_ref, ba_ref, wx_ref, bx_ref,
                  lam_ref, out_ref, nconv_ref, nh_ref, tail_ref, h_ref):
    ts, w = xr_ref.shape
    j = pl.program_id(1)

    @pl.when(j == 0)
    def _():
        tail_ref[...] = jnp.zeros((SUBLANES, w), F32)
        tail_ref[SUBLANES - (CONV_WIDTH - 1):SUBLANES, :] = cs_ref[0]
        h_ref[...] = h0_ref[0]

    xr = xr_ref[...]
    prev = tail_ref[...]
    row8 = lax.broadcasted_iota(jnp.int32, (SUBLANES, w), 0)
    xc = cb_ref[...] + xr * cw_ref[CONV_WIDTH - 1:CONV_WIDTH, :]
    for dly in range(1, CONV_WIDTH):
        sh = pltpu.roll(xr, dly, 0)
        head = jnp.where(row8 < dly, pltpu.roll(prev, dly, 0), sh[0:SUBLANES, :])
        sh = jnp.concatenate([head, sh[SUBLANES:, :]], axis=0)
        xc = xc + sh * cw_ref[CONV_WIDTH - 1 - dly:CONV_WIDTH - dly, :]
    nconv_ref[0] = xr[ts - (CONV_WIDTH - 1):ts, :]
    tail_ref[...] = xr[ts - SUBLANES:ts, :]

    xcb = xc.astype(BF16)
    r = jax.nn.sigmoid(jnp.dot(xcb, wa_ref[...], preferred_element_type=F32) + ba_ref[...])
    gi = jax.nn.sigmoid(jnp.dot(xcb, wx_ref[...], preferred_element_type=F32) + bx_ref[...])
    log_a = (RG_C * r) * (-_softplus(-lam_ref[...]))
    a = jnp.exp(log_a)
    one_m = 1.0 - a * a
    b = jnp.where(one_m > 0.0, one_m * lax.rsqrt(one_m), 0.0) * (gi * xc)

    sub = lax.broadcasted_iota(jnp.int32, (ts, w), 0) & (SUBLANES - 1)
    dist = 1
    while dist < SUBLANES:
        keep = sub >= dist
        a_prev = jnp.where(keep, pltpu.roll(a, dist, 0), 1.0)
        b_prev = jnp.where(keep, pltpu.roll(b, dist, 0), 0.0)
        b = a * b_prev + b
        a = a * a_prev
        dist *= 2
    carry = h_ref[...]
    groups = []
    for g0 in range(0, ts, SUBLANES):
        hg = b[g0:g0 + SUBLANES, :] + a[g0:g0 + SUBLANES, :] * carry
        groups.append(hg)
        carry = hg[SUBLANES - 1:SUBLANES, :]
    hs = jnp.concatenate(groups, axis=0)
    h_last = carry
    h_ref[...] = h_last
    nh_ref[0] = h_last
    out_ref[...] = (hs * gy_ref[...]).astype(BF16)


def _rglru(xr, gy, conv_state, h0, cw, cb, wa, ba, wx, bx, lam, nseq, seq_len, ts):
    t, w = xr.shape
    n = seq_len // ts
    row = pl.BlockSpec((ts, w), lambda b, j: (b * n + j, 0))
    vec = pl.BlockSpec((1, w), lambda b, j: (0, 0))
    return pl.pallas_call(
        _rglru_kernel,
        grid=(nseq, n),
        in_specs=[row, row,
                  pl.BlockSpec((1, CONV_WIDTH - 1, w), lambda b, j: (b, 0, 0)),
                  pl.BlockSpec((1, 1, w), lambda b, j: (b, 0, 0)),
                  pl.BlockSpec((CONV_WIDTH, w), lambda b, j: (0, 0)),
                  vec,
                  pl.BlockSpec((w, w), lambda b, j: (0, 0)), vec,
                  pl.BlockSpec((w, w), lambda b, j: (0, 0)), vec,
                  vec],
        out_specs=[row,
                   pl.BlockSpec((1, CONV_WIDTH - 1, w), lambda b, j: (b, 0, 0)),
                   pl.BlockSpec((1, 1, w), lambda b, j: (b, 0, 0))],
        out_shape=[jax.ShapeDtypeStruct((t, w), BF16),
                   jax.ShapeDtypeStruct((nseq, CONV_WIDTH - 1, w), F32),
                   jax.ShapeDtypeStruct((nseq, 1, w), F32)],
        scratch_shapes=[pltpu.VMEM((SUBLANES, w), F32), pltpu.VMEM((1, w), F32)],
        compiler_params=_cparams("parallel", "arbitrary"),
        name="rglru",
    )(xr, gy, conv_state, h0.reshape(nseq, 1, w), cw, cb.reshape(1, w), wa, ba.reshape(1, w),
      wx, bx.reshape(1, w), lam.reshape(1, w))


def _sb_scores(q, k):
    return lax.dot_general(q, k, (((1,), (1,)), ((), ())), preferred_element_type=F32)


def _sb_block(z, v, upper, off, acc, mask):
    neg_abs = lax.bitcast_convert_type(
        lax.bitcast_convert_type(z, jnp.int32) | jnp.int32(-2 ** 31), F32)
    sp = jnp.maximum(z, 0.0) + jnp.log2(1.0 + jnp.exp2(neg_abs))
    if mask is not None:
        sp = jnp.where(mask, sp, 0.0)
    spb = sp.astype(BF16)
    later = []
    for lo in range(SB_TK - SB_SUB, -1, -SB_SUB):
        later.append(jnp.dot(spb[:, lo:lo + SB_SUB], upper, preferred_element_type=F32) + off)
        off = off + jnp.sum(sp[:, lo:lo + SB_SUB], axis=-1, keepdims=True)
    wgt = jnp.exp2((z - sp) - jnp.concatenate(later[::-1], axis=1))
    if mask is not None:
        wgt = jnp.where(mask, wgt, 0.0)
    acc = acc + jnp.dot(wgt.astype(BF16), v, preferred_element_type=F32)
    return off, acc


def _sb_kernel(q_ref, k_ref, v_ref, up_ref, o_ref, z_ref, *, tq, q_pos0):
    i = pl.program_id(2)
    tk = SB_TK
    q = q_ref[...]
    upper = up_ref[...]
    n_full = (q_pos0 + i * tq) // tk
    off = jnp.zeros((tq, 1), F32)
    acc = jnp.zeros((tq, SB_HEAD_DIM), F32)

    def block_start(j):
        return pl.multiple_of(j * tk, tk)

    z_ref[...] = _sb_scores(q, k_ref[pl.ds(block_start(jnp.maximum(n_full - 1, 0)), tk), :])

    start = block_start(n_full)
    q_pos = q_pos0 + i * tq + lax.broadcasted_iota(jnp.int32, (tq, tk), 0)
    mask = (start + lax.broadcasted_iota(jnp.int32, (tq, tk), 1)) < q_pos
    off, acc = _sb_block(_sb_scores(q, k_ref[pl.ds(start, tk), :]), v_ref[pl.ds(start, tk), :],
                         upper, off, acc, mask)

    def live(off):
        return (jnp.min(off, axis=0, keepdims=True)[0, 0] < SB_DEAD_LOG2).astype(jnp.int32)

    def cond(carry):
        t, alive, _, _ = carry
        return jnp.logical_and(t < n_full, alive > 0)

    def body(carry):
        t, _, off, acc = carry
        j = n_full - 1 - t
        z = z_ref[...]
        z_ref[...] = _sb_scores(q, k_ref[pl.ds(block_start(jnp.maximum(j - 1, 0)), tk), :])
        off, acc = _sb_block(z, v_ref[pl.ds(block_start(j), tk), :], upper, off, acc, None)
        return t + 1, live(off), off, acc

    _, _, off, acc = lax.while_loop(cond, body, (jnp.int32(0), live(off), off, acc))
    o_ref[...] = acc.astype(o_ref.dtype)


def _stick_breaking(q, k_all, v_all, upper, nseq, seq_len, tq, q_pos0):
    t, hw = q.shape
    kpad = k_all.shape[1]
    nq = seq_len // tq
    half = SB_SUB
    assert SB_TK % tq == 0 and q_pos0 % tq == 0 and kpad % SB_TK == 0
    assert kpad >= ((q_pos0 + seq_len - 1) // SB_TK + 1) * SB_TK
    kern = functools.partial(_sb_kernel, tq=tq, q_pos0=q_pos0)
    qspec = pl.BlockSpec((tq, SB_HEAD_DIM), lambda b, h, i: (b * nq + i, h))
    kvspec = pl.BlockSpec((None, kpad, SB_HEAD_DIM), lambda b, h, i: (b, 0, h))
    return pl.pallas_call(
        kern,
        grid=(nseq, SB_HEADS, nq),
        in_specs=[qspec, kvspec, kvspec, pl.BlockSpec((half, half), lambda b, h, i: (0, 0))],
        out_specs=qspec,
        out_shape=jax.ShapeDtypeStruct((t, hw), BF16),
        scratch_shapes=[pltpu.VMEM((tq, SB_TK), F32)],
        compiler_params=_cparams("parallel", "parallel", "arbitrary"),
        name="stick_breaking",
    )(q, k_all, v_all, upper)


def _outproj_kernel(lru_ref, att_ref, x_ref, mod_ref, wo_ref, g_ref, b_ref, rw_ref, rb_ref,
                    x1_ref, tok_ref, code_ref, gate_ref, cnt_ref, *, alpha):
    i = pl.program_id(0)
    w = lru_ref.shape[-1]
    out = (jnp.dot(lru_ref[...], wo_ref[0:w, :], preferred_element_type=F32)
           + jnp.dot(att_ref[...], wo_ref[w:2 * w, :], preferred_element_type=F32))
    gate1 = mod_ref[0, 2:3, :]
    x1 = _layer_norm(alpha * x_ref[...] + gate1 * out, g_ref[...], b_ref[...])
    x1_ref[...] = x1
    tok = x1 * (1.0 + mod_ref[0, 4:5, :]) + mod_ref[0, 3:4, :]
    tok_ref[...] = tok

    logits = _dot_f32(tok, rw_ref[...]) + rb_ref[...]
    lane = lax.broadcasted_iota(jnp.int32, logits.shape, 1).astype(F32)
    code = jnp.zeros(logits.shape, F32)
    gates = jnp.zeros(logits.shape, F32)
    denom = jnp.zeros((logits.shape[0], 1), F32)
    top0 = None
    for kk in range(TOP_K):
        m = jnp.max(logits, axis=-1, keepdims=True)
        idx = jnp.min(jnp.where(logits == m, lane, float(LANES)), axis=-1, keepdims=True)
        hit = lane == idx
        if kk == 0:
            top0 = m
        e = jnp.exp(m - top0)
        denom = denom + e
        code = jnp.where(hit, float(kk + 1), code)
        gates = jnp.where(lane == kk, e, gates)
        logits = jnp.where(hit, -jnp.inf, logits)
    code_ref[...] = code.astype(BF16)
    gate_ref[...] = gates / denom

    @pl.when(i == 0)
    def _():
        cnt_ref[...] = jnp.zeros(cnt_ref.shape, F32)

    cnt_ref[...] += jnp.sum((code > 0.0).astype(F32), axis=0, keepdims=True)


def _outproj(lru, att, x2, mod, wo_b, g, b, rw_pad, rb_pad, seq0, seq_len, tm, alpha):
    t, d = x2.shape
    w = lru.shape[1]
    tiles_per_seq = seq_len // tm
    rowd = pl.BlockSpec((tm, d), lambda i: (i, 0))
    roww = pl.BlockSpec((tm, w), lambda i: (i, 0))
    rowl = pl.BlockSpec((tm, LANES), lambda i: (i, 0))
    vec = pl.BlockSpec((1, d), lambda i: (0, 0))
    return pl.pallas_call(
        functools.partial(_outproj_kernel, alpha=alpha),
        grid=(t // tm,),
        in_specs=[roww, roww, rowd,
                  pl.BlockSpec((1, 6, d), lambda i: (seq0 + i // tiles_per_seq, 0, 0)),
                  pl.BlockSpec((2 * w, d), lambda i: (0, 0)),
                  vec, vec,
                  pl.BlockSpec((d, LANES), lambda i: (0, 0)),
                  pl.BlockSpec((1, LANES), lambda i: (0, 0))],
        out_specs=[rowd, rowd, rowl, rowl, pl.BlockSpec((8, LANES), lambda i: (0, 0))],
        out_shape=[jax.ShapeDtypeStruct((t, d), F32), jax.ShapeDtypeStruct((t, d), F32),
                   jax.ShapeDtypeStruct((t, LANES), BF16), jax.ShapeDtypeStruct((t, LANES), F32),
                   jax.ShapeDtypeStruct((8, LANES), F32)],
        compiler_params=_cparams("arbitrary"),
        name="outproj_router",
    )(lru, att, x2, mod, wo_b, g.reshape(1, d), b.reshape(1, d), rw_pad, rb_pad)


def _rank_kernel(code_ref, cin_ref, ps_ref, dest_ref, cout_ref, carry_ref, low_ref):
    i = pl.program_id(0)
    tr = code_ref.shape[0]

    @pl.when(i == 0)
    def _():
        carry_ref[...] = cin_ref[0:1, :]
        r = lax.broadcasted_iota(jnp.int32, (tr, tr), 0)
        c = lax.broadcasted_iota(jnp.int32, (tr, tr), 1)
        low_ref[...] = (c < r).astype(BF16)

    code = code_ref[...].astype(F32)
    member = (code > 0.0).astype(BF16)
    before = jnp.dot(low_ref[...], member, preferred_element_type=F32) + carry_ref[...]
    base = before + ps_ref[...]
    lane = lax.broadcasted_iota(jnp.int32, code.shape, 1)
    dest = jnp.zeros(code.shape, F32)
    for kk in range(TOP_K):
        d = jnp.sum(jnp.where(code == float(kk + 1), base, 0.0), axis=-1, keepdims=True)
        dest = jnp.where(lane == kk, d, dest)
    dest_ref[...] = jnp.transpose(dest)[0:8, :].astype(jnp.int32)
    carry = before[tr - 1:tr, :] + (code[tr - 1:tr, :] > 0.0).astype(F32)
    carry_ref[...] = carry
    cout_ref[...] = jnp.broadcast_to(carry, cout_ref.shape)


def _rank(code, carry_in, pad_start, tr):
    t = code.shape[0]
    rowl = pl.BlockSpec((tr, LANES), lambda i: (i, 0))
    small = pl.BlockSpec((8, LANES), lambda i: (0, 0))
    return pl.pallas_call(
        _rank_kernel,
        grid=(t // tr,),
        in_specs=[rowl, small, pl.BlockSpec((1, LANES), lambda i: (0, 0))],
        out_specs=[pl.BlockSpec((8, tr), lambda i: (0, i)), small],
        out_shape=[jax.ShapeDtypeStruct((8, t), jnp.int32),
                   jax.ShapeDtypeStruct((8, LANES), F32)],
        scratch_shapes=[pltpu.VMEM((1, LANES), F32), pltpu.VMEM((tr, tr), BF16)],
        compiler_params=_cparams("arbitrary"),
        name="moe_rank",
    )(code, carry_in, pad_start)


def _dest_window(tm):
    width = max(tm, LANES)
    return width, width // tm


def _scatter_kernel(pe_ref, pd_ref, dest_ref, tok_ref, *rest, tm, per, zero_fill):
    if zero_fill:
        rows_ref, sem, zbuf, zsem = rest
    else:
        _, rows_ref, sem = rest
    i = pl.program_id(0)
    base = (i % per) * tm

    if zero_fill:
        @pl.when(i == 0)
        def _():
            zbuf[...] = jnp.zeros(zbuf.shape, zbuf.dtype)
            for e in range(N_EXPERTS):
                @pl.when(pd_ref[e] > 0)
                def _():
                    last = pl.multiple_of(pe_ref[e] - MOE_BLOCK, MOE_BLOCK)
                    cp = pltpu.make_async_copy(zbuf, rows_ref.at[pl.ds(last, MOE_BLOCK)], zsem)
                    cp.start()
                    cp.wait()

    def row_copy(t, kk):
        return pltpu.make_async_copy(tok_ref.at[pl.ds(t, 1), :],
                                     rows_ref.at[pl.ds(dest_ref[kk, base + t], 1), :], sem)

    def start(g, c):
        t0 = pl.multiple_of(g * SUBLANES, SUBLANES)
        for s in range(SUBLANES):
            for kk in range(TOP_K):
                row_copy(t0 + s, kk).start()
        return c

    def wait(g, c):
        t0 = pl.multiple_of(g * SUBLANES, SUBLANES)
        for s in range(SUBLANES):
            for kk in range(TOP_K):
                row_copy(t0 + s, kk).wait()
        return c

    lax.fori_loop(0, tm // SUBLANES, start, 0)
    lax.fori_loop(0, tm // SUBLANES, wait, 0)


def _scatter(pad_end, padded, dest, tok, rows_buf, n_rows, tm):
    t, d = tok.shape
    width, per = _dest_window(tm)
    zero_fill = rows_buf is None
    in_specs = [pl.BlockSpec((8, width), lambda i, pe, pd: (0, i // per), memory_space=pltpu.SMEM),
                pl.BlockSpec((tm, d), lambda i, pe, pd: (i, 0))]
    scratch = [pltpu.SemaphoreType.DMA(())]
    args = [pad_end, padded, dest, tok]
    if zero_fill:
        scratch += [pltpu.VMEM((MOE_BLOCK, d), F32), pltpu.SemaphoreType.DMA(())]
    else:
        in_specs.append(pl.BlockSpec(memory_space=pl.ANY))
        args.append(rows_buf)
    return pl.pallas_call(
        functools.partial(_scatter_kernel, tm=tm, per=per, zero_fill=zero_fill),
        grid_spec=pltpu.PrefetchScalarGridSpec(
            num_scalar_prefetch=2,
            grid=(t // tm,),
            in_specs=in_specs,
            out_specs=pl.BlockSpec(memory_space=pl.ANY),
            scratch_shapes=scratch),
        out_shape=jax.ShapeDtypeStruct((n_rows, d), F32),
        input_output_aliases={} if zero_fill else {4: 0},
        compiler_params=_cparams("arbitrary"),
        name="moe_scatter",
    )(*args)


def _expert_kernel(be_ref, nu_ref, x_ref, wgu_ref, bgu_ref, wd_ref, bd_ref, y_ref, wgu_b, wd_b):
    dff = wd_ref.shape[1]
    i = pl.program_id(0)

    @pl.when(i < nu_ref[0])
    def _():
        @pl.when(jnp.logical_or(i == 0, be_ref[i] != be_ref[jnp.maximum(i - 1, 0)]))
        def _():
            wgu_b[...] = wgu_ref[0].astype(BF16)
            wd_b[...] = wd_ref[0].astype(BF16)

        xb = x_ref[...].astype(BF16)
        gu = jnp.dot(xb, wgu_b[...], preferred_element_type=F32) + bgu_ref[0]
        gate = jnp.minimum(gu[:, :dff], SWIGLU_LIMIT)
        up = jnp.clip(gu[:, dff:], -SWIGLU_LIMIT, SWIGLU_LIMIT)
        act = (up + 1.0) * gate * jax.nn.sigmoid(SWIGLU_ALPHA * gate)
        y_ref[...] = jnp.dot(act.astype(BF16), wd_b[...], preferred_element_type=F32) + bd_ref[0]


def _experts(block_e, n_used, x_rows, wgu, bgu, wd, bd, layer):
    rows, d = x_rows.shape
    depth, ne, _, n_gu = wgu.shape
    dff = wd.shape[2]
    nb = rows // MOE_BLOCK

    def expert(i, be, nu):
        return be[jnp.minimum(i, nu[0] - 1)]

    rowspec = pl.BlockSpec((MOE_BLOCK, d), lambda i, be, nu: (jnp.minimum(i, nu[0] - 1), 0))
    return pl.pallas_call(
        _expert_kernel,
        grid_spec=pltpu.PrefetchScalarGridSpec(
            num_scalar_prefetch=2,
            grid=(nb,),
            in_specs=[rowspec,
                      pl.BlockSpec((None, 1, d, n_gu), lambda i, be, nu: (layer, expert(i, be, nu), 0, 0)),
                      pl.BlockSpec((None, 1, 1, n_gu), lambda i, be, nu: (layer, expert(i, be, nu), 0, 0)),
                      pl.BlockSpec((None, 1, dff, d), lambda i, be, nu: (layer, expert(i, be, nu), 0, 0)),
                      pl.BlockSpec((None, 1, 1, d), lambda i, be, nu: (layer, expert(i, be, nu), 0, 0))],
            out_specs=rowspec,
            scratch_shapes=[pltpu.VMEM((d, n_gu), BF16), pltpu.VMEM((dff, d), BF16)]),
        out_shape=jax.ShapeDtypeStruct((rows, d), F32),
        compiler_params=_cparams("arbitrary"),
        name="moe_experts",
    )(block_e, n_used, x_rows, wgu, bgu.reshape(depth, ne, 1, n_gu), wd, bd.reshape(depth, ne, 1, d))


def _sc_gather(rows, idx):
    n = idx.shape[0]
    d = rows.shape[1]
    mesh = plsc.VectorSubcoreMesh(core_axis_name="c", subcore_axis_name="s",
                                  num_cores=SC_CORES, num_subcores=SC_SUBCORES)
    n_workers = mesh.num_cores * mesh.num_subcores
    per_worker = n // n_workers
    assert per_worker * n_workers == n and per_worker % LANES == 0

    @functools.partial(pl.kernel, out_type=jax.ShapeDtypeStruct((n, d), rows.dtype), mesh=mesh,
                       scratch_types=[pltpu.VMEM((LANES,), jnp.int32),
                                      pltpu.VMEM((SC_WINDOW, d), rows.dtype),
                                      pltpu.VMEM((SC_WINDOW, d), rows.dtype),
                                      pltpu.SemaphoreType.DMA((2,)),
                                      pltpu.SemaphoreType.DMA((2,))],
                       compiler_params=pltpu.CompilerParams(use_tc_tiling_on_sc=True),
                       name="moe_gather")
    def gather(rows_hbm, idx_hbm, out_hbm, idx_vmem, buf0, buf1, gsem, wsem):
        worker = lax.axis_index("c") * mesh.num_subcores + lax.axis_index("s")
        bufs = (buf0, buf1)
        n_win = LANES // SC_WINDOW

        @pl.loop(0, per_worker // LANES)
        def _(j):
            chunk = worker * (per_worker // LANES) + j
            pltpu.sync_copy(idx_hbm.at[chunk], idx_vmem)

            def write(q):
                return pltpu.async_copy(
                    bufs[q % 2], out_hbm.at[pl.ds(chunk * LANES + q * SC_WINDOW, SC_WINDOW)],
                    wsem.at[q % 2])

            gathers, writes = [], []
            for q in range(n_win):
                if q >= 2:
                    writes[q - 2].wait()
                gathers.append(pltpu.async_copy(
                    rows_hbm.at[idx_vmem.at[pl.ds(q * SC_WINDOW, SC_WINDOW)]], bufs[q % 2],
                    gsem.at[q % 2]))
                if q >= 1:
                    gathers[q - 1].wait()
                    writes.append(write(q - 1))
            gathers[n_win - 1].wait()
            writes.append(write(n_win - 1))
            writes[n_win - 2].wait()
            writes[n_win - 1].wait()

    return gather(rows, idx.reshape(n // LANES, LANES))


def _combine_dense_kernel(y0_ref, y1_ref, y2_ref, y3_ref, gate_ref, x_ref, mod_ref, g_ref, b_ref,
                          o_ref, *, alpha):
    gates = gate_ref[...]
    f = y0_ref[...] * gates[:, 0:1]
    for kk, y_ref in enumerate((y1_ref, y2_ref, y3_ref), start=1):
        f = f + y_ref[...] * gates[:, kk:kk + 1]
    gate2 = mod_ref[0, 5:6, :]
    o_ref[...] = _layer_norm(alpha * x_ref[...] + gate2 * f, g_ref[...], b_ref[...])


def _combine_dense(y_slots, gates, x1, mod, g, b, seq0, seq_len, tm, alpha):
    t, d = x1.shape
    nt = t // tm
    tiles_per_seq = seq_len // tm
    rowd = pl.BlockSpec((tm, d), lambda i: (i, 0))
    vec = pl.BlockSpec((1, d), lambda i: (0, 0))
    slot_specs = [pl.BlockSpec((tm, d), functools.partial(lambda i, kk: (kk * nt + i, 0), kk=kk))
                  for kk in range(TOP_K)]
    return pl.pallas_call(
        functools.partial(_combine_dense_kernel, alpha=alpha),
        grid=(nt,),
        in_specs=slot_specs + [pl.BlockSpec((tm, LANES), lambda i: (i, 0)), rowd,
                               pl.BlockSpec((1, 6, d), lambda i: (seq0 + i // tiles_per_seq, 0, 0)),
                               vec, vec],
        out_specs=rowd,
        out_shape=jax.ShapeDtypeStruct((t, d), F32),
        compiler_params=_cparams("parallel"),
        name="moe_combine",
    )(y_slots, y_slots, y_slots, y_slots, gates, x1, mod, g.reshape(1, d), b.reshape(1, d))


def _combine_kernel(dest_ref, y_hbm, gate_ref, x_ref, mod_ref, g_ref, b_ref, o_ref, ybuf, sem,
                    *, alpha, per):
    tm = x_ref.shape[0]
    base = (pl.program_id(0) % per) * tm

    def row_copy(t, kk):
        return pltpu.make_async_copy(y_hbm.at[pl.ds(dest_ref[kk, base + t], 1), :],
                                     ybuf.at[kk, pl.ds(t, 1), :], sem)

    def start(g, c):
        t0 = pl.multiple_of(g * SUBLANES, SUBLANES)
        for s in range(SUBLANES):
            for kk in range(TOP_K):
                row_copy(t0 + s, kk).start()
        return c

    def wait(g, c):
        t0 = pl.multiple_of(g * SUBLANES, SUBLANES)
        for s in range(SUBLANES):
            for kk in range(TOP_K):
                row_copy(t0 + s, kk).wait()
        return c

    lax.fori_loop(0, tm // SUBLANES, start, 0)
    lax.fori_loop(0, tm // SUBLANES, wait, 0)

    gates = gate_ref[...]
    f = ybuf[0] * gates[:, 0:1]
    for kk in range(1, TOP_K):
        f = f + ybuf[kk] * gates[:, kk:kk + 1]
    gate2 = mod_ref[0, 5:6, :]
    o_ref[...] = _layer_norm(alpha * x_ref[...] + gate2 * f, g_ref[...], b_ref[...])


def _combine(dest, y_rows, gates, x1, mod, g, b, seq0, seq_len, tm, alpha):
    t, d = x1.shape
    tiles_per_seq = seq_len // tm
    width, per = _dest_window(tm)
    rowd = pl.BlockSpec((tm, d), lambda i: (i, 0))
    vec = pl.BlockSpec((1, d), lambda i: (0, 0))
    return pl.pallas_call(
        functools.partial(_combine_kernel, alpha=alpha, per=per),
        grid=(t // tm,),
        in_specs=[pl.BlockSpec((8, width), lambda i: (0, i // per), memory_space=pltpu.SMEM),
                  pl.BlockSpec(memory_space=pl.ANY),
                  pl.BlockSpec((tm, LANES), lambda i: (i, 0)),
                  rowd,
                  pl.BlockSpec((1, 6, d), lambda i: (seq0 + i // tiles_per_seq, 0, 0)),
                  vec, vec],
        out_specs=rowd,
        out_shape=jax.ShapeDtypeStruct((t, d), F32),
        scratch_shapes=[pltpu.VMEM((TOP_K, tm, d), F32), pltpu.SemaphoreType.DMA(())],
        compiler_params=_cparams("arbitrary"),
        name="moe_combine",
    )(dest, y_rows, gates, x1, mod, g.reshape(1, d), b.reshape(1, d))


def _block_diag(wh):
    heads, hd, _ = wh.shape
    eye = jnp.eye(heads, dtype=wh.dtype)
    return jnp.einsum('hij,hg->higj', wh, eye).reshape(heads * hd, heads * hd)


def kernel(x_prompt, x_sample, cache_k, cache_v, state_conv, state_lru, c_prompt, c_sample, ln_in_g, ln_in_b, w_ada, b_ada, w_in, conv_w, conv_b, gate_a_w, gate_a_b, gate_x_w, gate_x_b, lru_lambda, w_out, ln1_g, ln1_b, router_w, router_b, w_gu, b_gu, w_down, b_down, ln2_g, ln2_b):
    nb_p, s_p, d = x_prompt.shape
    nb_s, s_s, _ = x_sample.shape
    depth = w_in.shape[0]
    past = cache_k.shape[2]
    w = state_lru.shape[-1]
    hw = SB_HEADS * SB_HEAD_DIM
    t_p, t_s = nb_p * s_p, nb_s * s_s
    alpha = float((2 * depth) ** 0.25)

    tm_p, tm_s = min(512, s_p), min(512, s_s)
    ts_p, ts_s = min(256, s_p), min(256, s_s)
    tq_p, tq_s = min(SB_TQ, s_p), min(SB_TQ, s_s)
    tg_p, tg_s = min(256, s_p), min(256, s_s)
    tr_p, tr_s = min(512, t_p), min(512, t_s)

    n_seq = nb_p + nb_s
    c_all = jnp.concatenate([c_prompt, c_sample, jnp.zeros((-n_seq % 8, d), F32)], axis=0)
    mod_all = _adaln(c_all, w_ada, b_ada).reshape(depth, c_all.shape[0], 6, d)

    xp = _ln_in(x_prompt.reshape(t_p, d), ln_in_g, ln_in_b, tm_p)
    xs = _ln_in(x_sample.reshape(t_s, d), ln_in_g, ln_in_b, tm_s)

    upper = jnp.asarray(np.arange(SB_SUB)[:, None] > np.arange(SB_SUB)[None, :], BF16)
    kpad_s = -(-(past + s_s) // SB_TK) * SB_TK
    zero_conv = jnp.zeros((nb_p, CONV_WIDTH - 1, w), F32)
    zero_h = jnp.zeros((nb_p, w), F32)

    tk_total = (t_p + t_s) * TOP_K
    n_blocks = -(-(tk_total + N_EXPERTS * (MOE_BLOCK - 1)) // MOE_BLOCK)
    n_rows = n_blocks * MOE_BLOCK

    outs = [[] for _ in range(8)]
    for l in range(depth):
        mod = mod_all[l]
        w_in_b = w_in[l].astype(BF16)
        wo_b = w_out[l].astype(BF16)
        wa = _block_diag(gate_a_w[l]).astype(BF16)
        wx = _block_diag(gate_x_w[l]).astype(BF16)
        rw_pad = jnp.pad(router_w[l], ((0, 0), (0, LANES - N_EXPERTS)))
        rb_pad = jnp.pad(router_b[l], (0, LANES - N_EXPERTS), constant_values=NEG_BIG).reshape(1, LANES)
        lru_w = (conv_w[l], conv_b[l], wa, gate_a_b[l], wx, gate_x_b[l], lru_lambda[l])

        xr, gy, q, k, v, kb, vb = _inproj(xp, mod, w_in_b, 0, s_p, tm_p)
        lru_p, nconv_p, nh_p = _rglru(xr, gy, zero_conv, zero_h, *lru_w, nb_p, s_p, ts_p)
        att_p = _stick_breaking(q, kb.reshape(nb_p, s_p, hw), vb.reshape(nb_p, s_p, hw), upper,
                                nb_p, s_p, tq_p, 0)
        outs[0].append(k.reshape(nb_p, s_p, SB_HEADS, SB_HEAD_DIM))
        outs[1].append(v.reshape(nb_p, s_p, SB_HEADS, SB_HEAD_DIM))
        outs[2].append(nconv_p)
        outs[3].append(nh_p.reshape(nb_p, w))

        xr, gy, q, k, v, kb, vb = _inproj(xs, mod, w_in_b, nb_p, s_s, tm_s)
        lru_s, nconv_s, nh_s = _rglru(xr, gy, state_conv[l], state_lru[l], *lru_w, nb_s, s_s, ts_s)
        kv_pad = jnp.zeros((nb_s, kpad_s - past - s_s, hw), BF16)
        k_all = jnp.concatenate([cache_k[l].reshape(nb_s, past, hw).astype(BF16),
                                 kb.reshape(nb_s, s_s, hw), kv_pad], axis=1)
        v_all = jnp.concatenate([cache_v[l].reshape(nb_s, past, hw).astype(BF16),
                                 vb.reshape(nb_s, s_s, hw), kv_pad], axis=1)
        att_s = _stick_breaking(q, k_all, v_all, upper, nb_s, s_s, tq_s, past)
        outs[4].append(k.reshape(nb_s, s_s, SB_HEADS, SB_HEAD_DIM))
        outs[5].append(v.reshape(nb_s, s_s, SB_HEADS, SB_HEAD_DIM))
        outs[6].append(nconv_s)
        outs[7].append(nh_s.reshape(nb_s, w))

        x1_p, tok_p, code_p, gate_p, cnt_p = _outproj(lru_p, att_p, xp, mod, wo_b, ln1_g[l], ln1_b[l],
                                                      rw_pad, rb_pad, 0, s_p, tm_p, alpha)
        x1_s, tok_s, code_s, gate_s, cnt_s = _outproj(lru_s, att_s, xs, mod, wo_b, ln1_g[l], ln1_b[l],
                                                      rw_pad, rb_pad, nb_p, s_s, tm_s, alpha)

        counts = (cnt_p[0, :N_EXPERTS] + cnt_s[0, :N_EXPERTS]).astype(jnp.int32)
        padded = (counts + MOE_BLOCK - 1) // MOE_BLOCK * MOE_BLOCK
        pad_end = jnp.cumsum(padded)
        pad_start = jnp.pad((pad_end - padded).astype(F32), (0, LANES - N_EXPERTS)).reshape(1, LANES)
        block_row0 = jnp.arange(n_blocks, dtype=jnp.int32) * MOE_BLOCK
        block_e = jnp.minimum(jnp.sum(pad_end[None, :] <= block_row0[:, None], axis=1),
                              N_EXPERTS - 1).astype(jnp.int32)

        n_used = (pad_end[N_EXPERTS - 1:] // MOE_BLOCK).astype(jnp.int32)

        dest_p, carry = _rank(code_p, jnp.zeros((8, LANES), F32), pad_start, tr_p)
        dest_s, _ = _rank(code_s, carry, pad_start, tr_s)

        x_rows = _scatter(pad_end, padded, dest_p, tok_p, None, n_rows, tg_p)
        x_rows = _scatter(pad_end, padded, dest_s, tok_s, x_rows, n_rows, tg_s)
        y_rows = _experts(block_e, n_used, x_rows, w_gu, b_gu, w_down, b_down, l)

        yg_p = _sc_gather(y_rows, dest_p[:TOP_K].reshape(TOP_K * t_p))
        xp = _combine_dense(yg_p, gate_p, x1_p, mod, ln2_g[l], ln2_b[l], 0, s_p, tg_p, alpha)
        xs = _combine(dest_s, y_rows, gate_s, x1_s, mod, ln2_g[l], ln2_b[l], nb_p, s_s, tg_s, alpha)

    return (xp.reshape(nb_p, s_p, d), xs.reshape(nb_s, s_s, d),
            jnp.stack(outs[0]), jnp.stack(outs[1]), jnp.stack(outs[2]), jnp.stack(outs[3]),
            jnp.stack(outs[4]), jnp.stack(outs[5]), jnp.stack(outs[6]), jnp.stack(outs[7]))
```

```python
import functools

import jax
import jax.numpy as jnp
import numpy as np
from jax import lax
from jax.experimental import pallas as pl
from jax.experimental.pallas import tpu as pltpu
from jax.experimental.pallas import tpu_sc as plsc

F32 = jnp.float32
BF16 = jnp.bfloat16

LANES = 128
SUBLANES = 8
LRU_HEADS = 8
CONV_WIDTH = 4
RG_C = 8.0
SB_HEADS = 4
SB_HEAD_DIM = 128
SB_SCALE = SB_HEAD_DIM ** -0.5
SB_SUB = 256
SB_TQ = 512
SB_TK = 512
SB_DEAD_LOG2 = 160.0
LOG2E = 1.4426950408889634
N_EXPERTS = 32
TOP_K = 4
SWIGLU_LIMIT = 7.0
SWIGLU_ALPHA = 1.702
MOE_BLOCK = 512
SC_CORES, SC_SUBCORES = 2, 16
SC_WINDOW = 32
LN_EPS = 1e-5
NEG_BIG = -1e30
VMEM_LIMIT = 56 * 1024 * 1024


def _cparams(*sem):
    return pltpu.CompilerParams(dimension_semantics=sem, vmem_limit_bytes=VMEM_LIMIT)


def _split3(a):
    hi = a.astype(BF16)
    r1 = a - hi.astype(F32)
    mid = r1.astype(BF16)
    lo = (r1 - mid.astype(F32)).astype(BF16)
    return hi, mid, lo


def _dot_f32(a, b):
    a0, a1, a2 = _split3(a)
    b0, b1, b2 = _split3(b)
    d = functools.partial(jnp.dot, preferred_element_type=F32)
    return (d(a0, b0) + (d(a0, b1) + d(a1, b0))
            + (d(a0, b2) + d(a1, b1) + d(a2, b0)))


def _layer_norm(y, g, b):
    mu = jnp.mean(y, axis=-1, keepdims=True)
    d = y - mu
    var = jnp.mean(d * d, axis=-1, keepdims=True)
    return d * lax.rsqrt(var + LN_EPS) * g + b


def _softplus(z):
    return jnp.maximum(z, 0.0) + jnp.log(1.0 + jnp.exp(-jnp.abs(z)))


def _gelu_tanh(x):
    c = np.sqrt(2.0 / np.pi).astype(np.float32)
    return 0.5 * x * (1.0 + jnp.tanh(c * (x + 0.044715 * (x * x * x))))


def _adaln_kernel(c_ref, w_ref, b_ref, o_ref):
    c = c_ref[...]
    s = c * jax.nn.sigmoid(c)
    o_ref[0] = _dot_f32(s, w_ref[0]) + b_ref[0]


def _adaln(c_all, w_ada, b_ada):
    depth, d, n = w_ada.shape
    rows = c_all.shape[0]
    tn = 1024
    return pl.pallas_call(
        _adaln_kernel,
        grid=(depth, n // tn),
        in_specs=[pl.BlockSpec((rows, d), lambda l, j: (0, 0)),
                  pl.BlockSpec((1, d, tn), lambda l, j: (l, 0, j)),
                  pl.BlockSpec((1, 1, tn), lambda l, j: (l, 0, j))],
        out_specs=pl.BlockSpec((1, rows, tn), lambda l, j: (l, 0, j)),
        out_shape=jax.ShapeDtypeStruct((depth, rows, n), F32),
        compiler_params=_cparams("parallel", "parallel"),
        name="adaln",
    )(c_all, w_ada, b_ada.reshape(depth, 1, n))


def _ln_kernel(x_ref, g_ref, b_ref, o_ref):
    o_ref[...] = _layer_norm(x_ref[...], g_ref[...], b_ref[...])


def _ln_in(x2, g, b, tm):
    t, d = x2.shape
    return pl.pallas_call(
        _ln_kernel,
        grid=(t // tm,),
        in_specs=[pl.BlockSpec((tm, d), lambda i: (i, 0)),
                  pl.BlockSpec((1, d), lambda i: (0, 0)),
                  pl.BlockSpec((1, d), lambda i: (0, 0))],
        out_specs=pl.BlockSpec((tm, d), lambda i: (i, 0)),
        out_shape=jax.ShapeDtypeStruct((t, d), F32),
        compiler_params=_cparams("parallel"),
        name="ln_in",
    )(x2, g.reshape(1, d), b.reshape(1, d))


def _inproj_kernel(x_ref, mod_ref, w_ref, xr_ref, gy_ref, q_ref, k_ref, v_ref, kb_ref, vb_ref):
    w = xr_ref.shape[-1]
    shift = mod_ref[0, 0:1, :]
    scale = mod_ref[0, 1:2, :]
    h = (x_ref[...] * (1.0 + scale) + shift).astype(BF16)

    def proj(j):
        return jnp.dot(h, w_ref[:, j * w:(j + 1) * w], preferred_element_type=F32)

    xr_ref[...] = proj(0)
    gy_ref[...] = _gelu_tanh(proj(1))
    q_ref[...] = (proj(2) * (SB_SCALE * LOG2E)).astype(BF16)
    k = proj(3)
    k_ref[...] = k
    kb_ref[...] = k.astype(BF16)
    v = proj(4)
    v_ref[...] = v
    vb_ref[...] = v.astype(BF16)


def _inproj(x2, mod, w_in_b, seq0, seq_len, tm):
    t, d = x2.shape
    w = w_in_b.shape[1] // 5
    tiles_per_seq = seq_len // tm
    row = pl.BlockSpec((tm, w), lambda i: (i, 0))
    f32o = jax.ShapeDtypeStruct((t, w), F32)
    bf16o = jax.ShapeDtypeStruct((t, w), BF16)
    return pl.pallas_call(
        _inproj_kernel,
        grid=(t // tm,),
        in_specs=[pl.BlockSpec((tm, d), lambda i: (i, 0)),
                  pl.BlockSpec((1, 6, d), lambda i: (seq0 + i // tiles_per_seq, 0, 0)),
                  pl.BlockSpec((d, 5 * w), lambda i: (0, 0))],
        out_specs=[row] * 7,
        out_shape=[f32o, f32o, bf16o, f32o, f32o, bf16o, bf16o],
        compiler_params=_cparams("parallel"),
        name="inproj",
    )(x2, mod, w_in_b)


def _rglru_kernel(xr_ref, gy_ref, cs_ref, h0_ref, cw_ref, cb_ref, wa_ref, ba_ref, wx_ref, bx_ref,
                  lam_ref, out_ref, nconv_ref, nh_ref, tail_ref, h_ref):
    ts, w = xr_ref.shape
    j = pl.program_id(1)

    @pl.when(j == 0)
    def _():
        tail_ref[...] = jnp.zeros((SUBLANES, w), F32)
        tail_ref[SUBLANES - (CONV_WIDTH - 1):SUBLANES, :] = cs_ref[0]
        h_ref[...] = h0_ref[0]

    xr = xr_ref[...]
    prev = tail_ref[...]
    row8 = lax.broadcasted_iota(jnp.int32, (SUBLANES, w), 0)
    xc = cb_ref[...] + xr * cw_ref[CONV_WIDTH - 1:CONV_WIDTH, :]
    for dly in range(1, CONV_WIDTH):
        sh = pltpu.roll(xr, dly, 0)
        head = jnp.where(row8 < dly, pltpu.roll(prev, dly, 0), sh[0:SUBLANES, :])
        sh = jnp.concatenate([head, sh[SUBLANES:, :]], axis=0)
        xc = xc + sh * cw_ref[CONV_WIDTH - 1 - dly:CONV_WIDTH - dly, :]
    nconv_ref[0] = xr[ts - (CONV_WIDTH - 1):ts, :]
    tail_ref[...] = xr[ts - SUBLANES:ts, :]

    xcb = xc.astype(BF16)
    r = jax.nn.sigmoid(jnp.dot(xcb, wa_ref[...], preferred_element_type=F32) + ba_ref[...])
    gi = jax.nn.sigmoid(jnp.dot(xcb, wx_ref[...], preferred_element_type=F32) + bx_ref[...])
    log_a = (RG_C * r) * (-_softplus(-lam_ref[...]))
    a = jnp.exp(log_a)
    one_m = 1.0 - a * a
    b = jnp.where(one_m > 0.0, one_m * lax.rsqrt(one_m), 0.0) * (gi * xc)

    sub = lax.broadcasted_iota(jnp.int32, (ts, w), 0) & (SUBLANES - 1)
    dist = 1
    while dist < SUBLANES:
        keep = sub >= dist
        a_prev = jnp.where(keep, pltpu.roll(a, dist, 0), 1.0)
        b_prev = jnp.where(keep, pltpu.roll(b, dist, 0), 0.0)
        b = a * b_prev + b
        a = a * a_prev
        dist *= 2
    carry = h_ref[...]
    groups = []
    for g0 in range(0, ts, SUBLANES):
        hg = b[g0:g0 + SUBLANES, :] + a[g0:g0 + SUBLANES, :] * carry
        groups.append(hg)
        carry = hg[SUBLANES - 1:SUBLANES, :]
    hs = jnp.concatenate(groups, axis=0)
    h_last = carry
    h_ref[...] = h_last
    nh_ref[0] = h_last
    out_ref[...] = (hs * gy_ref[...]).astype(BF16)


def _rglru(xr, gy, conv_state, h0, cw, cb, wa, ba, wx, bx, lam, nseq, seq_len, ts):
    t, w = xr.shape
    n = seq_len // ts
    row = pl.BlockSpec((ts, w), lambda b, j: (b * n + j, 0))
    vec = pl.BlockSpec((1, w), lambda b, j: (0, 0))
    return pl.pallas_call(
        _rglru_kernel,
        grid=(nseq, n),
        in_specs=[row, row,
                  pl.BlockSpec((1, CONV_WIDTH - 1, w), lambda b, j: (b, 0, 0)),
                  pl.BlockSpec((1, 1, w), lambda b, j: (b, 0, 0)),
                  pl.BlockSpec((CONV_WIDTH, w), lambda b, j: (0, 0)),
                  vec,
                  pl.BlockSpec((w, w), lambda b, j: (0, 0)), vec,
                  pl.BlockSpec((w, w), lambda b, j: (0, 0)), vec,
                  vec],
        out_specs=[row,
                   pl.BlockSpec((1, CONV_WIDTH - 1, w), lambda b, j: (b, 0, 0)),
                   pl.BlockSpec((1, 1, w), lambda b, j: (b, 0, 0))],
        out_shape=[jax.ShapeDtypeStruct((t, w), BF16),
                   jax.ShapeDtypeStruct((nseq, CONV_WIDTH - 1, w), F32),
                   jax.ShapeDtypeStruct((nseq, 1, w), F32)],
        scratch_shapes=[pltpu.VMEM((SUBLANES, w), F32), pltpu.VMEM((1, w), F32)],
        compiler_params=_cparams("parallel", "arbitrary"),
        name="rglru",
    )(xr, gy, conv_state, h0.reshape(nseq, 1, w), cw, cb.reshape(1, w), wa, ba.reshape(1, w),
      wx, bx.reshape(1, w), lam.reshape(1, w))


def _sb_scores(q, k):
    return lax.dot_general(q, k, (((1,), (1,)), ((), ())), preferred_element_type=F32)


def _sb_block(z, v, upper, off, acc, mask):
    neg_abs = lax.bitcast_convert_type(
        lax.bitcast_convert_type(z, jnp.int32) | jnp.int32(-2 ** 31), F32)
    sp = jnp.maximum(z, 0.0) + jnp.log2(1.0 + jnp.exp2(neg_abs))
    if mask is not None:
        sp = jnp.where(mask, sp, 0.0)
    spb = sp.astype(BF16)
    later = []
    for lo in range(SB_TK - SB_SUB, -1, -SB_SUB):
        later.append(jnp.dot(spb[:, lo:lo + SB_SUB], upper, preferred_element_type=F32) + off)
        off = off + jnp.sum(sp[:, lo:lo + SB_SUB], axis=-1, keepdims=True)
    wgt = jnp.exp2((z - sp) - jnp.concatenate(later[::-1], axis=1))
    if mask is not None:
        wgt = jnp.where(mask, wgt, 0.0)
    acc = acc + jnp.dot(wgt.astype(BF16), v, preferred_element_type=F32)
    return off, acc


def _sb_kernel(q_ref, k_ref, v_ref, up_ref, o_ref, z_ref, *, tq, q_pos0):
    i = pl.program_id(2)
    tk = SB_TK
    q = q_ref[...]
    upper = up_ref[...]
    n_full = (q_pos0 + i * tq) // tk
    off = jnp.zeros((tq, 1), F32)
    acc = jnp.zeros((tq, SB_HEAD_DIM), F32)

    def block_start(j):
        return pl.multiple_of(j * tk, tk)

    z_ref[...] = _sb_scores(q, k_ref[pl.ds(block_start(jnp.maximum(n_full - 1, 0)), tk), :])

    start = block_start(n_full)
    q_pos = q_pos0 + i * tq + lax.broadcasted_iota(jnp.int32, (tq, tk), 0)
    mask = (start + lax.broadcasted_iota(jnp.int32, (tq, tk), 1)) < q_pos
    off, acc = _sb_block(_sb_scores(q, k_ref[pl.ds(start, tk), :]), v_ref[pl.ds(start, tk), :],
                         upper, off, acc, mask)

    def live(off):
        return (jnp.min(off, axis=0, keepdims=True)[0, 0] < SB_DEAD_LOG2).astype(jnp.int32)

    def cond(carry):
        t, alive, _, _ = carry
        return jnp.logical_and(t < n_full, alive > 0)

    def body(carry):
        t, _, off, acc = carry
        j = n_full - 1 - t
        z = z_ref[...]
        z_ref[...] = _sb_scores(q, k_ref[pl.ds(block_start(jnp.maximum(j - 1, 0)), tk), :])
        off, acc = _sb_block(z, v_ref[pl.ds(block_start(j), tk), :], upper, off, acc, None)
        return t + 1, live(off), off, acc

    _, _, off, acc = lax.while_loop(cond, body, (jnp.int32(0), live(off), off, acc))
    o_ref[...] = acc.astype(o_ref.dtype)


def _stick_breaking(q, k_all, v_all, upper, nseq, seq_len, tq, q_pos0):
    t, hw = q.shape
    kpad = k_all.shape[1]
    nq = seq_len // tq
    half = SB_SUB
    assert SB_TK % tq == 0 and q_pos0 % tq == 0 and kpad % SB_TK == 0
    assert kpad >= ((q_pos0 + seq_len - 1) // SB_TK + 1) * SB_TK
    kern = functools.partial(_sb_kernel, tq=tq, q_pos0=q_pos0)
    qspec = pl.BlockSpec((tq, SB_HEAD_DIM), lambda b, h, i: (b * nq + i, h))
    kvspec = pl.BlockSpec((None, kpad, SB_HEAD_DIM), lambda b, h, i: (b, 0, h))
    return pl.pallas_call(
        kern,
        grid=(nseq, SB_HEADS, nq),
        in_specs=[qspec, kvspec, kvspec, pl.BlockSpec((half, half), lambda b, h, i: (0, 0))],
        out_specs=qspec,
        out_shape=jax.ShapeDtypeStruct((t, hw), BF16),
        scratch_shapes=[pltpu.VMEM((tq, SB_TK), F32)],
        compiler_params=_cparams("parallel", "parallel", "arbitrary"),
        name="stick_breaking",
    )(q, k_all, v_all, upper)


def _outproj_kernel(lru_ref, att_ref, x_ref, mod_ref, wo_ref, g_ref, b_ref, rw_ref, rb_ref,
                    x1_ref, tok_ref, code_ref, gate_ref, cnt_ref, *, alpha):
    i = pl.program_id(0)
    w = lru_ref.shape[-1]
    out = (jnp.dot(lru_ref[...], wo_ref[0:w, :], preferred_element_type=F32)
           + jnp.dot(att_ref[...], wo_ref[w:2 * w, :], preferred_element_type=F32))
    gate1 = mod_ref[0, 2:3, :]
    x1 = _layer_norm(alpha * x_ref[...] + gate1 * out, g_ref[...], b_ref[...])
    x1_ref[...] = x1
    tok = x1 * (1.0 + mod_ref[0, 4:5, :]) + mod_ref[0, 3:4, :]
    tok_ref[...] = tok

    logits = _dot_f32(tok, rw_ref[...]) + rb_ref[...]
    lane = lax.broadcasted_iota(jnp.int32, logits.shape, 1).astype(F32)
    code = jnp.zeros(logits.shape, F32)
    gates = jnp.zeros(logits.shape, F32)
    denom = jnp.zeros((logits.shape[0], 1), F32)
    top0 = None
    for kk in range(TOP_K):
        m = jnp.max(logits, axis=-1, keepdims=True)
        idx = jnp.min(jnp.where(logits == m, lane, float(LANES)), axis=-1, keepdims=True)
        hit = lane == idx
        if kk == 0:
            top0 = m
        e = jnp.exp(m - top0)
        denom = denom + e
        code = jnp.where(hit, float(kk + 1), code)
        gates = jnp.where(lane == kk, e, gates)
        logits = jnp.where(hit, -jnp.inf, logits)
    code_ref[...] = code.astype(BF16)
    gate_ref[...] = gates / denom

    @pl.when(i == 0)
    def _():
        cnt_ref[...] = jnp.zeros(cnt_ref.shape, F32)

    cnt_ref[...] += jnp.sum((code > 0.0).astype(F32), axis=0, keepdims=True)


def _outproj(lru, att, x2, mod, wo_b, g, b, rw_pad, rb_pad, seq0, seq_len, tm, alpha):
    t, d = x2.shape
    w = lru.shape[1]
    tiles_per_seq = seq_len // tm
    rowd = pl.BlockSpec((tm, d), lambda i: (i, 0))
    roww = pl.BlockSpec((tm, w), lambda i: (i, 0))
    rowl = pl.BlockSpec((tm, LANES), lambda i: (i, 0))
    vec = pl.BlockSpec((1, d), lambda i: (0, 0))
    return pl.pallas_call(
        functools.partial(_outproj_kernel, alpha=alpha),
        grid=(t // tm,),
        in_specs=[roww, roww, rowd,
                  pl.BlockSpec((1, 6, d), lambda i: (seq0 + i // tiles_per_seq, 0, 0)),
                  pl.BlockSpec((2 * w, d), lambda i: (0, 0)),
                  vec, vec,
                  pl.BlockSpec((d, LANES), lambda i: (0, 0)),
                  pl.BlockSpec((1, LANES), lambda i: (0, 0))],
        out_specs=[rowd, rowd, rowl, rowl, pl.BlockSpec((8, LANES), lambda i: (0, 0))],
        out_shape=[jax.ShapeDtypeStruct((t, d), F32), jax.ShapeDtypeStruct((t, d), F32),
                   jax.ShapeDtypeStruct((t, LANES), BF16), jax.ShapeDtypeStruct((t, LANES), F32),
                   jax.ShapeDtypeStruct((8, LANES), F32)],
        compiler_params=_cparams("arbitrary"),
        name="outproj_router",
    )(lru, att, x2, mod, wo_b, g.reshape(1, d), b.reshape(1, d), rw_pad, rb_pad)


def _rank_kernel(code_ref, cin_ref, ps_ref, dest_ref, cout_ref, carry_ref, low_ref):
    i = pl.program_id(0)
    tr = code_ref.shape[0]

    @pl.when(i == 0)
    def _():
        carry_ref[...] = cin_ref[0:1, :]
        r = lax.broadcasted_iota(jnp.int32, (tr, tr), 0)
        c = lax.broadcasted_iota(jnp.int32, (tr, tr), 1)
        low_ref[...] = (c < r).astype(BF16)

    code = code_ref[...].astype(F32)
    member = (code > 0.0).astype(BF16)
    before = jnp.dot(low_ref[...], member, preferred_element_type=F32) + carry_ref[...]
    base = before + ps_ref[...]
    lane = lax.broadcasted_iota(jnp.int32, code.shape, 1)
    dest = jnp.zeros(code.shape, F32)
    for kk in range(TOP_K):
        d = jnp.sum(jnp.where(code == float(kk + 1), base, 0.0), axis=-1, keepdims=True)
        dest = jnp.where(lane == kk, d, dest)
    dest_ref[...] = jnp.transpose(dest)[0:8, :].astype(jnp.int32)
    carry = before[tr - 1:tr, :] + (code[tr - 1:tr, :] > 0.0).astype(F32)
    carry_ref[...] = carry
    cout_ref[...] = jnp.broadcast_to(carry, cout_ref.shape)


def _rank(code, carry_in, pad_start, tr):
    t = code.shape[0]
    rowl = pl.BlockSpec((tr, LANES), lambda i: (i, 0))
    small = pl.BlockSpec((8, LANES), lambda i: (0, 0))
    return pl.pallas_call(
        _rank_kernel,
        grid=(t // tr,),
        in_specs=[rowl, small, pl.BlockSpec((1, LANES), lambda i: (0, 0))],
        out_specs=[pl.BlockSpec((8, tr), lambda i: (0, i)), small],
        out_shape=[jax.ShapeDtypeStruct((8, t), jnp.int32),
                   jax.ShapeDtypeStruct((8, LANES), F32)],
        scratch_shapes=[pltpu.VMEM((1, LANES), F32), pltpu.VMEM((tr, tr), BF16)],
        compiler_params=_cparams("arbitrary"),
        name="moe_rank",
    )(code, carry_in, pad_start)


def _dest_window(tm):
    width = max(tm, LANES)
    return width, width // tm


def _scatter_kernel(dest_ref, tok_ref, rows_in_ref, rows_ref, sem, *, tm, per):
    del rows_in_ref
    base = (pl.program_id(0) % per) * tm

    def row_copy(t, kk):
        return pltpu.make_async_copy(tok_ref.at[pl.ds(t, 1), :],
                                     rows_ref.at[pl.ds(dest_ref[kk, base + t], 1), :], sem)

    def start(g, c):
        t0 = pl.multiple_of(g * SUBLANES, SUBLANES)
        for s in range(SUBLANES):
            for kk in range(TOP_K):
                row_copy(t0 + s, kk).start()
        return c

    def wait(g, c):
        t0 = pl.multiple_of(g * SUBLANES, SUBLANES)
        for s in range(SUBLANES):
            for kk in range(TOP_K):
                row_copy(t0 + s, kk).wait()
        return c

    lax.fori_loop(0, tm // SUBLANES, start, 0)
    lax.fori_loop(0, tm // SUBLANES, wait, 0)


def _scatter(dest, tok, rows_buf, tm):
    t, d = tok.shape
    width, per = _dest_window(tm)
    return pl.pallas_call(
        functools.partial(_scatter_kernel, tm=tm, per=per),
        grid=(t // tm,),
        in_specs=[pl.BlockSpec((8, width), lambda i: (0, i // per), memory_space=pltpu.SMEM),
                  pl.BlockSpec((tm, d), lambda i: (i, 0)),
                  pl.BlockSpec(memory_space=pl.ANY)],
        out_specs=pl.BlockSpec(memory_space=pl.ANY),
        out_shape=jax.ShapeDtypeStruct(rows_buf.shape, rows_buf.dtype),
        scratch_shapes=[pltpu.SemaphoreType.DMA(())],
        input_output_aliases={2: 0},
        compiler_params=_cparams("arbitrary"),
        name="moe_scatter",
    )(dest, tok, rows_buf)


def _expert_kernel(be_ref, nu_ref, nv_ref, x_ref, wgu_ref, bgu_ref, wd_ref, bd_ref, y_ref, wgu_b, wd_b):
    dff = wd_ref.shape[1]
    i = pl.program_id(0)

    @pl.when(i < nu_ref[0])
    def _():
        @pl.when(jnp.logical_or(i == 0, be_ref[i] != be_ref[jnp.maximum(i - 1, 0)]))
        def _():
            wgu_b[...] = wgu_ref[0].astype(BF16)
            wd_b[...] = wd_ref[0].astype(BF16)

        row = lax.broadcasted_iota(jnp.int32, (x_ref.shape[0], 1), 0)
        xb = jnp.where(row < nv_ref[i], x_ref[...], 0.0).astype(BF16)
        gu = jnp.dot(xb, wgu_b[...], preferred_element_type=F32) + bgu_ref[0]
        gate = jnp.minimum(gu[:, :dff], SWIGLU_LIMIT)
        up = jnp.clip(gu[:, dff:], -SWIGLU_LIMIT, SWIGLU_LIMIT)
        act = (up + 1.0) * gate * jax.nn.sigmoid(SWIGLU_ALPHA * gate)
        y_ref[...] = jnp.dot(act.astype(BF16), wd_b[...], preferred_element_type=F32) + bd_ref[0]


def _experts(block_e, n_used, n_valid, x_rows, wgu, bgu, wd, bd, layer):
    rows, d = x_rows.shape
    depth, ne, _, n_gu = wgu.shape
    dff = wd.shape[2]
    nb = rows // MOE_BLOCK

    def expert(i, be, nu):
        return be[jnp.minimum(i, nu[0] - 1)]

    rowspec = pl.BlockSpec((MOE_BLOCK, d), lambda i, be, nu, nv: (jnp.minimum(i, nu[0] - 1), 0))
    return pl.pallas_call(
        _expert_kernel,
        grid_spec=pltpu.PrefetchScalarGridSpec(
            num_scalar_prefetch=3,
            grid=(nb,),
            in_specs=[rowspec,
                      pl.BlockSpec((None, 1, d, n_gu), lambda i, be, nu, nv: (layer, expert(i, be, nu), 0, 0)),
                      pl.BlockSpec((None, 1, 1, n_gu), lambda i, be, nu, nv: (layer, expert(i, be, nu), 0, 0)),
                      pl.BlockSpec((None, 1, dff, d), lambda i, be, nu, nv: (layer, expert(i, be, nu), 0, 0)),
                      pl.BlockSpec((None, 1, 1, d), lambda i, be, nu, nv: (layer, expert(i, be, nu), 0, 0))],
            out_specs=rowspec,
            scratch_shapes=[pltpu.VMEM((d, n_gu), BF16), pltpu.VMEM((dff, d), BF16)]),
        out_shape=jax.ShapeDtypeStruct((rows, d), F32),
        compiler_params=_cparams("arbitrary"),
        name="moe_experts",
    )(block_e, n_used, n_valid, x_rows, wgu, bgu.reshape(depth, ne, 1, n_gu), wd,
      bd.reshape(depth, ne, 1, d))


def _sc_gather(rows, idx):
    n = idx.shape[0]
    d = rows.shape[1]
    mesh = plsc.VectorSubcoreMesh(core_axis_name="c", subcore_axis_name="s",
                                  num_cores=SC_CORES, num_subcores=SC_SUBCORES)
    n_workers = mesh.num_cores * mesh.num_subcores
    per_worker = n // n_workers
    assert per_worker * n_workers == n and per_worker % LANES == 0

    @functools.partial(pl.kernel, out_type=jax.ShapeDtypeStruct((n, d), rows.dtype), mesh=mesh,
                       scratch_types=[pltpu.VMEM((LANES,), jnp.int32),
                                      pltpu.VMEM((SC_WINDOW, d), rows.dtype),
                                      pltpu.VMEM((SC_WINDOW, d), rows.dtype),
                                      pltpu.SemaphoreType.DMA((2,)),
                                      pltpu.SemaphoreType.DMA((2,))],
                       compiler_params=pltpu.CompilerParams(use_tc_tiling_on_sc=True),
                       name="moe_gather")
    def gather(rows_hbm, idx_hbm, out_hbm, idx_vmem, buf0, buf1, gsem, wsem):
        worker = lax.axis_index("c") * mesh.num_subcores + lax.axis_index("s")
        bufs = (buf0, buf1)
        n_win = LANES // SC_WINDOW

        @pl.loop(0, per_worker // LANES)
        def _(j):
            chunk = worker * (per_worker // LANES) + j
            pltpu.sync_copy(idx_hbm.at[chunk], idx_vmem)

            def write(q):
                return pltpu.async_copy(
                    bufs[q % 2], out_hbm.at[pl.ds(chunk * LANES + q * SC_WINDOW, SC_WINDOW)],
                    wsem.at[q % 2])

            gathers, writes = [], []
            for q in range(n_win):
                if q >= 2:
                    writes[q - 2].wait()
                gathers.append(pltpu.async_copy(
                    rows_hbm.at[idx_vmem.at[pl.ds(q * SC_WINDOW, SC_WINDOW)]], bufs[q % 2],
                    gsem.at[q % 2]))
                if q >= 1:
                    gathers[q - 1].wait()
                    writes.append(write(q - 1))
            gathers[n_win - 1].wait()
            writes.append(write(n_win - 1))
            writes[n_win - 2].wait()
            writes[n_win - 1].wait()

    return gather(rows, idx.reshape(n // LANES, LANES))


def _sc_scatter(tok, dest, n_rows):
    t, d = tok.shape
    mesh = plsc.VectorSubcoreMesh(core_axis_name="c", subcore_axis_name="s",
                                  num_cores=SC_CORES, num_subcores=SC_SUBCORES)
    n_workers = mesh.num_cores * mesh.num_subcores
    per_worker = t // n_workers
    assert per_worker * n_workers == t and per_worker % LANES == 0
    chunks = t // LANES
    n_win = LANES // SC_WINDOW
    lanes16 = 16

    @functools.partial(pl.kernel, out_type=jax.ShapeDtypeStruct((n_rows, d), tok.dtype), mesh=mesh,
                       scratch_types=[pltpu.VMEM((TOP_K, LANES), jnp.int32),
                                      pltpu.VMEM((SC_WINDOW, d), tok.dtype),
                                      pltpu.VMEM((SC_WINDOW, d), tok.dtype),
                                      pltpu.SemaphoreType.DMA((2,)),
                                      pltpu.SemaphoreType.DMA((2,))],
                       compiler_params=pltpu.CompilerParams(use_tc_tiling_on_sc=True),
                       name="moe_scatter_sc")
    def scatter(tok_hbm, idx_hbm, rows_hbm, idx_vmem, buf0, buf1, lsem, ssem):
        worker = lax.axis_index("c") * mesh.num_subcores + lax.axis_index("s")
        bufs = (buf0, buf1)

        @pl.loop(0, per_worker // LANES)
        def _(j):
            chunk = worker * (per_worker // LANES) + j
            for kk in range(TOP_K):
                pltpu.sync_copy(idx_hbm.at[kk * chunks + chunk], idx_vmem.at[kk])

            def load(q):
                return pltpu.async_copy(
                    tok_hbm.at[pl.ds(chunk * LANES + q * SC_WINDOW, SC_WINDOW)], bufs[q % 2],
                    lsem.at[q % 2])

            def store_all(q):
                copies = []
                for kk in range(TOP_K):
                    for h in range(SC_WINDOW // lanes16):
                        rows16 = idx_vmem[kk, pl.ds(q * SC_WINDOW + h * lanes16, lanes16)]
                        copies.append(pltpu.async_copy(
                            bufs[q % 2].at[pl.ds(h * lanes16, lanes16)], rows_hbm.at[rows16],
                            ssem.at[q % 2]))
                return copies

            loads = [load(0)]
            stores = []
            for q in range(n_win):
                if q + 1 < n_win:
                    if q >= 1:
                        for cp in stores[q - 1]:
                            cp.wait()
                    loads.append(load(q + 1))
                loads[q].wait()
                stores.append(store_all(q))
            for q in (n_win - 2, n_win - 1):
                for cp in stores[q]:
                    cp.wait()

    return scatter(tok, dest[:TOP_K].reshape(TOP_K * chunks, LANES))


def _combine_dense_kernel(y0_ref, y1_ref, y2_ref, y3_ref, gate_ref, x_ref, mod_ref, g_ref, b_ref,
                          o_ref, *, alpha):
    gates = gate_ref[...]
    f = y0_ref[...] * gates[:, 0:1]
    for kk, y_ref in enumerate((y1_ref, y2_ref, y3_ref), start=1):
        f = f + y_ref[...] * gates[:, kk:kk + 1]
    gate2 = mod_ref[0, 5:6, :]
    o_ref[...] = _layer_norm(alpha * x_ref[...] + gate2 * f, g_ref[...], b_ref[...])


def _combine_dense(y_slots, gates, x1, mod, g, b, seq0, seq_len, tm, alpha):
    t, d = x1.shape
    nt = t // tm
    tiles_per_seq = seq_len // tm
    rowd = pl.BlockSpec((tm, d), lambda i: (i, 0))
    vec = pl.BlockSpec((1, d), lambda i: (0, 0))
    slot_specs = [pl.BlockSpec((tm, d), functools.partial(lambda i, kk: (kk * nt + i, 0), kk=kk))
                  for kk in range(TOP_K)]
    return pl.pallas_call(
        functools.partial(_combine_dense_kernel, alpha=alpha),
        grid=(nt,),
        in_specs=slot_specs + [pl.BlockSpec((tm, LANES), lambda i: (i, 0)), rowd,
                               pl.BlockSpec((1, 6, d), lambda i: (seq0 + i // tiles_per_seq, 0, 0)),
                               vec, vec],
        out_specs=rowd,
        out_shape=jax.ShapeDtypeStruct((t, d), F32),
        compiler_params=_cparams("parallel"),
        name="moe_combine",
    )(y_slots, y_slots, y_slots, y_slots, gates, x1, mod, g.reshape(1, d), b.reshape(1, d))


def _combine_kernel(dest_ref, y_hbm, gate_ref, x_ref, mod_ref, g_ref, b_ref, o_ref, ybuf, sem,
                    *, alpha, per):
    tm = x_ref.shape[0]
    base = (pl.program_id(0) % per) * tm

    def row_copy(t, kk):
        return pltpu.make_async_copy(y_hbm.at[pl.ds(dest_ref[kk, base + t], 1), :],
                                     ybuf.at[kk, pl.ds(t, 1), :], sem)

    def start(g, c):
        t0 = pl.multiple_of(g * SUBLANES, SUBLANES)
        for s in range(SUBLANES):
            for kk in range(TOP_K):
                row_copy(t0 + s, kk).start()
        return c

    def wait(g, c):
        t0 = pl.multiple_of(g * SUBLANES, SUBLANES)
        for s in range(SUBLANES):
            for kk in range(TOP_K):
                row_copy(t0 + s, kk).wait()
        return c

    lax.fori_loop(0, tm // SUBLANES, start, 0)
    lax.fori_loop(0, tm // SUBLANES, wait, 0)

    gates = gate_ref[...]
    f = ybuf[0] * gates[:, 0:1]
    for kk in range(1, TOP_K):
        f = f + ybuf[kk] * gates[:, kk:kk + 1]
    gate2 = mod_ref[0, 5:6, :]
    o_ref[...] = _layer_norm(alpha * x_ref[...] + gate2 * f, g_ref[...], b_ref[...])


def _combine(dest, y_rows, gates, x1, mod, g, b, seq0, seq_len, tm, alpha):
    t, d = x1.shape
    tiles_per_seq = seq_len // tm
    width, per = _dest_window(tm)
    rowd = pl.BlockSpec((tm, d), lambda i: (i, 0))
    vec = pl.BlockSpec((1, d), lambda i: (0, 0))
    return pl.pallas_call(
        functools.partial(_combine_kernel, alpha=alpha, per=per),
        grid=(t // tm,),
        in_specs=[pl.BlockSpec((8, width), lambda i: (0, i // per), memory_space=pltpu.SMEM),
                  pl.BlockSpec(memory_space=pl.ANY),
                  pl.BlockSpec((tm, LANES), lambda i: (i, 0)),
                  rowd,
                  pl.BlockSpec((1, 6, d), lambda i: (seq0 + i // tiles_per_seq, 0, 0)),
                  vec, vec],
        out_specs=rowd,
        out_shape=jax.ShapeDtypeStruct((t, d), F32),
        scratch_shapes=[pltpu.VMEM((TOP_K, tm, d), F32), pltpu.SemaphoreType.DMA(())],
        compiler_params=_cparams("arbitrary"),
        name="moe_combine",
    )(dest, y_rows, gates, x1, mod, g.reshape(1, d), b.reshape(1, d))


def _block_diag(wh):
    heads, hd, _ = wh.shape
    eye = jnp.eye(heads, dtype=wh.dtype)
    return jnp.einsum('hij,hg->higj', wh, eye).reshape(heads * hd, heads * hd)


def kernel(x_prompt, x_sample, cache_k, cache_v, state_conv, state_lru, c_prompt, c_sample, ln_in_g, ln_in_b, w_ada, b_ada, w_in, conv_w, conv_b, gate_a_w, gate_a_b, gate_x_w, gate_x_b, lru_lambda, w_out, ln1_g, ln1_b, router_w, router_b, w_gu, b_gu, w_down, b_down, ln2_g, ln2_b):
    nb_p, s_p, d = x_prompt.shape
    nb_s, s_s, _ = x_sample.shape
    depth = w_in.shape[0]
    past = cache_k.shape[2]
    w = state_lru.shape[-1]
    hw = SB_HEADS * SB_HEAD_DIM
    t_p, t_s = nb_p * s_p, nb_s * s_s
    alpha = float((2 * depth) ** 0.25)

    tm_p, tm_s = min(512, s_p), min(512, s_s)
    ts_p, ts_s = min(256, s_p), min(256, s_s)
    tq_p, tq_s = min(SB_TQ, s_p), min(SB_TQ, s_s)
    tg_p, tg_s = min(256, s_p), min(256, s_s)
    tr_p, tr_s = min(512, t_p), min(512, t_s)

    n_seq = nb_p + nb_s
    c_all = jnp.concatenate([c_prompt, c_sample, jnp.zeros((-n_seq % 8, d), F32)], axis=0)
    mod_all = _adaln(c_all, w_ada, b_ada).reshape(depth, c_all.shape[0], 6, d)

    xp = _ln_in(x_prompt.reshape(t_p, d), ln_in_g, ln_in_b, tm_p)
    xs = _ln_in(x_sample.reshape(t_s, d), ln_in_g, ln_in_b, tm_s)

    upper = jnp.asarray(np.arange(SB_SUB)[:, None] > np.arange(SB_SUB)[None, :], BF16)
    kpad_s = -(-(past + s_s) // SB_TK) * SB_TK
    zero_conv = jnp.zeros((nb_p, CONV_WIDTH - 1, w), F32)
    zero_h = jnp.zeros((nb_p, w), F32)

    tk_total = (t_p + t_s) * TOP_K
    n_blocks = -(-(tk_total + N_EXPERTS * (MOE_BLOCK - 1)) // MOE_BLOCK)
    n_rows = n_blocks * MOE_BLOCK

    outs = [[] for _ in range(8)]
    for l in range(depth):
        mod = mod_all[l]
        w_in_b = w_in[l].astype(BF16)
        wo_b = w_out[l].astype(BF16)
        wa = _block_diag(gate_a_w[l]).astype(BF16)
        wx = _block_diag(gate_x_w[l]).astype(BF16)
        rw_pad = jnp.pad(router_w[l], ((0, 0), (0, LANES - N_EXPERTS)))
        rb_pad = jnp.pad(router_b[l], (0, LANES - N_EXPERTS), constant_values=NEG_BIG).reshape(1, LANES)
        lru_w = (conv_w[l], conv_b[l], wa, gate_a_b[l], wx, gate_x_b[l], lru_lambda[l])

        xr, gy, q, k, v, kb, vb = _inproj(xp, mod, w_in_b, 0, s_p, tm_p)
        lru_p, nconv_p, nh_p = _rglru(xr, gy, zero_conv, zero_h, *lru_w, nb_p, s_p, ts_p)
        att_p = _stick_breaking(q, kb.reshape(nb_p, s_p, hw), vb.reshape(nb_p, s_p, hw), upper,
                                nb_p, s_p, tq_p, 0)
        outs[0].append(k.reshape(nb_p, s_p, SB_HEADS, SB_HEAD_DIM))
        outs[1].append(v.reshape(nb_p, s_p, SB_HEADS, SB_HEAD_DIM))
        outs[2].append(nconv_p)
        outs[3].append(nh_p.reshape(nb_p, w))

        xr, gy, q, k, v, kb, vb = _inproj(xs, mod, w_in_b, nb_p, s_s, tm_s)
        lru_s, nconv_s, nh_s = _rglru(xr, gy, state_conv[l], state_lru[l], *lru_w, nb_s, s_s, ts_s)
        kv_pad = jnp.zeros((nb_s, kpad_s - past - s_s, hw), BF16)
        k_all = jnp.concatenate([cache_k[l].reshape(nb_s, past, hw).astype(BF16),
                                 kb.reshape(nb_s, s_s, hw), kv_pad], axis=1)
        v_all = jnp.concatenate([cache_v[l].reshape(nb_s, past, hw).astype(BF16),
                                 vb.reshape(nb_s, s_s, hw), kv_pad], axis=1)
        att_s = _stick_breaking(q, k_all, v_all, upper, nb_s, s_s, tq_s, past)
        outs[4].append(k.reshape(nb_s, s_s, SB_HEADS, SB_HEAD_DIM))
        outs[5].append(v.reshape(nb_s, s_s, SB_HEADS, SB_HEAD_DIM))
        outs[6].append(nconv_s)
        outs[7].append(nh_s.reshape(nb_s, w))

        x1_p, tok_p, code_p, gate_p, cnt_p = _outproj(lru_p, att_p, xp, mod, wo_b, ln1_g[l], ln1_b[l],
                                                      rw_pad, rb_pad, 0, s_p, tm_p, alpha)
        x1_s, tok_s, code_s, gate_s, cnt_s = _outproj(lru_s, att_s, xs, mod, wo_b, ln1_g[l], ln1_b[l],
                                                      rw_pad, rb_pad, nb_p, s_s, tm_s, alpha)

        counts = (cnt_p[0, :N_EXPERTS] + cnt_s[0, :N_EXPERTS]).astype(jnp.int32)
        padded = (counts + MOE_BLOCK - 1) // MOE_BLOCK * MOE_BLOCK
        pad_end = jnp.cumsum(padded)
        pad_start = jnp.pad((pad_end - padded).astype(F32), (0, LANES - N_EXPERTS)).reshape(1, LANES)
        block_row0 = jnp.arange(n_blocks, dtype=jnp.int32) * MOE_BLOCK
        block_e = jnp.minimum(jnp.sum(pad_end[None, :] <= block_row0[:, None], axis=1),
                              N_EXPERTS - 1).astype(jnp.int32)

        n_used = (pad_end[N_EXPERTS - 1:] // MOE_BLOCK).astype(jnp.int32)
        n_valid = jnp.clip(counts[block_e] - (block_row0 - (pad_end - padded)[block_e]), 0, MOE_BLOCK)

        dest_p, carry = _rank(code_p, jnp.zeros((8, LANES), F32), pad_start, tr_p)
        dest_s, _ = _rank(code_s, carry, pad_start, tr_s)

        x_rows = _sc_scatter(tok_p, dest_p, n_rows)
        x_rows = _scatter(dest_s, tok_s, x_rows, tg_s)
        y_rows = _experts(block_e, n_used, n_valid, x_rows, w_gu, b_gu, w_down, b_down, l)

        yg_p = _sc_gather(y_rows, dest_p[:TOP_K].reshape(TOP_K * t_p))
        xp = _combine_dense(yg_p, gate_p, x1_p, mod, ln2_g[l], ln2_b[l], 0, s_p, tg_p, alpha)
        xs = _combine(dest_s, y_rows, gate_s, x1_s, mod, ln2_g[l], ln2_b[l], nb_p, s_s, tg_s, alpha)

    return (xp.reshape(nb_p, s_p, d), xs.reshape(nb_s, s_s, d),
            jnp.stack(outs[0]), jnp.stack(outs[1]), jnp.stack(outs[2]), jnp.stack(outs[3]),
            jnp.stack(outs[4]), jnp.stack(outs[5]), jnp.stack(outs[6]), jnp.stack(outs[7]))
```

```python
import functools

import jax
import jax.numpy as jnp
import numpy as np
from jax import lax
from jax.experimental import pallas as pl
from jax.experimental.pallas import tpu as pltpu
from jax.experimental.pallas import tpu_sc as plsc

F32 = jnp.float32
BF16 = jnp.bfloat16

LANES = 128
SUBLANES = 8
LRU_HEADS = 8
CONV_WIDTH = 4
RG_C = 8.0
SB_HEADS = 4
SB_HEAD_DIM = 128
SB_SCALE = SB_HEAD_DIM ** -0.5
SB_SUB = 256
SB_TQ = 512
SB_TK = 512
SB_DEAD_LOG2 = 160.0
LOG2E = 1.4426950408889634
N_EXPERTS = 32
TOP_K = 4
SWIGLU_LIMIT = 7.0
SWIGLU_ALPHA = 1.702
MOE_BLOCK = 512
SC_CORES, SC_SUBCORES = 2, 16
SC_WINDOW = 32
LN_EPS = 1e-5
NEG_BIG = -1e30
VMEM_LIMIT = 56 * 1024 * 1024


def _cparams(*sem):
    return pltpu.CompilerParams(dimension_semantics=sem, vmem_limit_bytes=VMEM_LIMIT)


def _split3(a):
    hi = a.astype(BF16)
    r1 = a - hi.astype(F32)
    mid = r1.astype(BF16)
    lo = (r1 - mid.astype(F32)).astype(BF16)
    return hi, mid, lo


def _dot_f32(a, b):
    a0, a1, a2 = _split3(a)
    b0, b1, b2 = _split3(b)
    d = functools.partial(jnp.dot, preferred_element_type=F32)
    return (d(a0, b0) + (d(a0, b1) + d(a1, b0))
            + (d(a0, b2) + d(a1, b1) + d(a2, b0)))


def _layer_norm(y, g, b):
    mu = jnp.mean(y, axis=-1, keepdims=True)
    d = y - mu
    var = jnp.mean(d * d, axis=-1, keepdims=True)
    return d * lax.rsqrt(var + LN_EPS) * g + b


def _softplus(z):
    return jnp.maximum(z, 0.0) + jnp.log(1.0 + jnp.exp(-jnp.abs(z)))


def _gelu_tanh(x):
    c = np.sqrt(2.0 / np.pi).astype(np.float32)
    return 0.5 * x * (1.0 + jnp.tanh(c * (x + 0.044715 * (x * x * x))))


def _adaln_kernel(c_ref, w_ref, b_ref, o_ref):
    c = c_ref[...]
    s = c * jax.nn.sigmoid(c)
    o_ref[0] = _dot_f32(s, w_ref[0]) + b_ref[0]


def _adaln(c_all, w_ada, b_ada):
    depth, d, n = w_ada.shape
    rows = c_all.shape[0]
    tn = 1024
    return pl.pallas_call(
        _adaln_kernel,
        grid=(depth, n // tn),
        in_specs=[pl.BlockSpec((rows, d), lambda l, j: (0, 0)),
                  pl.BlockSpec((1, d, tn), lambda l, j: (l, 0, j)),
                  pl.BlockSpec((1, 1, tn), lambda l, j: (l, 0, j))],
        out_specs=pl.BlockSpec((1, rows, tn), lambda l, j: (l, 0, j)),
        out_shape=jax.ShapeDtypeStruct((depth, rows, n), F32),
        compiler_params=_cparams("parallel", "parallel"),
        name="adaln",
    )(c_all, w_ada, b_ada.reshape(depth, 1, n))


def _ln_kernel(x_ref, g_ref, b_ref, o_ref):
    o_ref[...] = _layer_norm(x_ref[...], g_ref[...], b_ref[...])


def _ln_in(x2, g, b, tm):
    t, d = x2.shape
    return pl.pallas_call(
        _ln_kernel,
        grid=(t // tm,),
        in_specs=[pl.BlockSpec((tm, d), lambda i: (i, 0)),
                  pl.BlockSpec((1, d), lambda i: (0, 0)),
                  pl.BlockSpec((1, d), lambda i: (0, 0))],
        out_specs=pl.BlockSpec((tm, d), lambda i: (i, 0)),
        out_shape=jax.ShapeDtypeStruct((t, d), F32),
        compiler_params=_cparams("parallel"),
        name="ln_in",
    )(x2, g.reshape(1, d), b.reshape(1, d))


def _inproj_kernel(x_ref, mod_ref, w_ref, *rest):
    xr_ref, gy_ref, q_ref, k_ref, v_ref, kb_ref, vb_ref = rest[-7:]
    w = xr_ref.shape[-1]

    def store_heads(ref, val):
        for hh in range(SB_HEADS):
            ref[:, hh, :] = val[:, hh * SB_HEAD_DIM:(hh + 1) * SB_HEAD_DIM]

    shift = mod_ref[0, 0:1, :]
    scale = mod_ref[0, 1:2, :]
    h = (x_ref[...] * (1.0 + scale) + shift).astype(BF16)

    def proj(j):
        return jnp.dot(h, w_ref[:, j * w:(j + 1) * w], preferred_element_type=F32)

    xr_ref[...] = proj(0)
    gy_ref[...] = _gelu_tanh(proj(1))
    q_ref[...] = (proj(2) * (SB_SCALE * LOG2E)).astype(BF16)
    k = proj(3)
    store_heads(k_ref, k)
    kb_ref[...] = k.astype(BF16)
    v = proj(4)
    store_heads(v_ref, v)
    vb_ref[...] = v.astype(BF16)


def _inproj(x2, mod, w_in_b, seq0, seq_len, tm, layer, depth, kv_stacks):
    t, d = x2.shape
    w = w_in_b.shape[1] // 5
    nt = t // tm
    tiles_per_seq = seq_len // tm
    row = pl.BlockSpec((tm, w), lambda i: (i, 0))
    heads = pl.BlockSpec((tm, SB_HEADS, SB_HEAD_DIM), lambda i: (layer * nt + i, 0, 0))
    f32o = jax.ShapeDtypeStruct((t, w), F32)
    bf16o = jax.ShapeDtypeStruct((t, w), BF16)
    stack = jax.ShapeDtypeStruct((depth * t, SB_HEADS, SB_HEAD_DIM), F32)
    in_specs = [pl.BlockSpec((tm, d), lambda i: (i, 0)),
                pl.BlockSpec((1, 6, d), lambda i: (seq0 + i // tiles_per_seq, 0, 0)),
                pl.BlockSpec((d, 5 * w), lambda i: (0, 0))]
    args = [x2, mod, w_in_b]
    aliases = {}
    if kv_stacks is not None:
        in_specs += [pl.BlockSpec(memory_space=pl.ANY)] * 2
        args += list(kv_stacks)
        aliases = {3: 3, 4: 4}
    return pl.pallas_call(
        _inproj_kernel,
        grid=(nt,),
        in_specs=in_specs,
        out_specs=[row, row, row, heads, heads, row, row],
        out_shape=[f32o, f32o, bf16o, stack, stack, bf16o, bf16o],
        input_output_aliases=aliases,
        compiler_params=_cparams("parallel"),
        name="inproj",
    )(*args)


def _rglru_kernel(xr_ref, gy_ref, cs_ref, h0_ref, cw_ref, cb_ref, wa_ref, ba_ref, wx_ref, bx_ref,
                  lam_ref, out_ref, nconv_ref, nh_ref, tail_ref, h_ref):
    ts, w = xr_ref.shape
    j = pl.program_id(1)

    @pl.when(j == 0)
    def _():
        tail_ref[...] = jnp.zeros((SUBLANES, w), F32)
        tail_ref[SUBLANES - (CONV_WIDTH - 1):SUBLANES, :] = cs_ref[0]
        h_ref[...] = h0_ref[0]

    xr = xr_ref[...]
    prev = tail_ref[...]
    row8 = lax.broadcasted_iota(jnp.int32, (SUBLANES, w), 0)
    xc = cb_ref[...] + xr * cw_ref[CONV_WIDTH - 1:CONV_WIDTH, :]
    for dly in range(1, CONV_WIDTH):
        sh = pltpu.roll(xr, dly, 0)
        head = jnp.where(row8 < dly, pltpu.roll(prev, dly, 0), sh[0:SUBLANES, :])
        sh = jnp.concatenate([head, sh[SUBLANES:, :]], axis=0)
        xc = xc + sh * cw_ref[CONV_WIDTH - 1 - dly:CONV_WIDTH - dly, :]
    nconv_ref[0] = xr[ts - (CONV_WIDTH - 1):ts, :]
    tail_ref[...] = xr[ts - SUBLANES:ts, :]

    xcb = xc.astype(BF16)
    r = jax.nn.sigmoid(jnp.dot(xcb, wa_ref[...], preferred_element_type=F32) + ba_ref[...])
    gi = jax.nn.sigmoid(jnp.dot(xcb, wx_ref[...], preferred_element_type=F32) + bx_ref[...])
    log_a = (RG_C * r) * (-_softplus(-lam_ref[...]))
    a = jnp.exp(log_a)
    one_m = 1.0 - a * a
    b = jnp.where(one_m > 0.0, one_m * lax.rsqrt(one_m), 0.0) * (gi * xc)

    sub = lax.broadcasted_iota(jnp.int32, (ts, w), 0) & (SUBLANES - 1)
    dist = 1
    while dist < SUBLANES:
        keep = sub >= dist
        a_prev = jnp.where(keep, pltpu.roll(a, dist, 0), 1.0)
        b_prev = jnp.where(keep, pltpu.roll(b, dist, 0), 0.0)
        b = a * b_prev + b
        a = a * a_prev
        dist *= 2
    carry = h_ref[...]
    groups = []
    for g0 in range(0, ts, SUBLANES):
        hg = b[g0:g0 + SUBLANES, :] + a[g0:g0 + SUBLANES, :] * carry
        groups.append(hg)
        carry = hg[SUBLANES - 1:SUBLANES, :]
    hs = jnp.concatenate(groups, axis=0)
    h_last = carry
    h_ref[...] = h_last
    nh_ref[0] = h_last
    out_ref[...] = (hs * gy_ref[...]).astype(BF16)


def _rglru(xr, gy, conv_state, h0, cw, cb, wa, ba, wx, bx, lam, nseq, seq_len, ts):
    t, w = xr.shape
    n = seq_len // ts
    row = pl.BlockSpec((ts, w), lambda b, j: (b * n + j, 0))
    vec = pl.BlockSpec((1, w), lambda b, j: (0, 0))
    return pl.pallas_call(
        _rglru_kernel,
        grid=(nseq, n),
        in_specs=[row, row,
                  pl.BlockSpec((1, CONV_WIDTH - 1, w), lambda b, j: (b, 0, 0)),
                  pl.BlockSpec((1, 1, w), lambda b, j: (b, 0, 0)),
                  pl.BlockSpec((CONV_WIDTH, w), lambda b, j: (0, 0)),
                  vec,
                  pl.BlockSpec((w, w), lambda b, j: (0, 0)), vec,
                  pl.BlockSpec((w, w), lambda b, j: (0, 0)), vec,
                  vec],
        out_specs=[row,
                   pl.BlockSpec((1, CONV_WIDTH - 1, w), lambda b, j: (b, 0, 0)),
                   pl.BlockSpec((1, 1, w), lambda b, j: (b, 0, 0))],
        out_shape=[jax.ShapeDtypeStruct((t, w), BF16),
                   jax.ShapeDtypeStruct((nseq, CONV_WIDTH - 1, w), F32),
                   jax.ShapeDtypeStruct((nseq, 1, w), F32)],
        scratch_shapes=[pltpu.VMEM((SUBLANES, w), F32), pltpu.VMEM((1, w), F32)],
        compiler_params=_cparams("parallel", "arbitrary"),
        name="rglru",
    )(xr, gy, conv_state, h0.reshape(nseq, 1, w), cw, cb.reshape(1, w), wa, ba.reshape(1, w),
      wx, bx.reshape(1, w), lam.reshape(1, w))


def _sb_scores(q, k):
    return lax.dot_general(q, k, (((1,), (1,)), ((), ())), preferred_element_type=F32)


def _sb_block(z, v, upper, off, acc, mask):
    neg_abs = lax.bitcast_convert_type(
        lax.bitcast_convert_type(z, jnp.int32) | jnp.int32(-2 ** 31), F32)
    sp = jnp.maximum(z, 0.0) + jnp.log2(1.0 + jnp.exp2(neg_abs))
    if mask is not None:
        sp = jnp.where(mask, sp, 0.0)
    spb = sp.astype(BF16)
    later = []
    for lo in range(SB_TK - SB_SUB, -1, -SB_SUB):
        later.append(jnp.dot(spb[:, lo:lo + SB_SUB], upper, preferred_element_type=F32) + off)
        off = off + jnp.sum(sp[:, lo:lo + SB_SUB], axis=-1, keepdims=True)
    wgt = jnp.exp2((z - sp) - jnp.concatenate(later[::-1], axis=1))
    if mask is not None:
        wgt = jnp.where(mask, wgt, 0.0)
    acc = acc + jnp.dot(wgt.astype(BF16), v, preferred_element_type=F32)
    return off, acc


def _sb_kernel(q_ref, k_ref, v_ref, up_ref, o_ref, z_ref, *, tq, q_pos0):
    i = pl.program_id(2)
    tk = SB_TK
    q = q_ref[...]
    upper = up_ref[...]
    n_full = (q_pos0 + i * tq) // tk
    off = jnp.zeros((tq, 1), F32)
    acc = jnp.zeros((tq, SB_HEAD_DIM), F32)

    def block_start(j):
        return pl.multiple_of(j * tk, tk)

    z_ref[...] = _sb_scores(q, k_ref[pl.ds(block_start(jnp.maximum(n_full - 1, 0)), tk), :])

    start = block_start(n_full)
    q_pos = q_pos0 + i * tq + lax.broadcasted_iota(jnp.int32, (tq, tk), 0)
    mask = (start + lax.broadcasted_iota(jnp.int32, (tq, tk), 1)) < q_pos
    off, acc = _sb_block(_sb_scores(q, k_ref[pl.ds(start, tk), :]), v_ref[pl.ds(start, tk), :],
                         upper, off, acc, mask)

    def live(off):
        return (jnp.min(off, axis=0, keepdims=True)[0, 0] < SB_DEAD_LOG2).astype(jnp.int32)

    def cond(carry):
        t, alive, _, _ = carry
        return jnp.logical_and(t < n_full, alive > 0)

    def body(carry):
        t, _, off, acc = carry
        j = n_full - 1 - t
        z = z_ref[...]
        z_ref[...] = _sb_scores(q, k_ref[pl.ds(block_start(jnp.maximum(j - 1, 0)), tk), :])
        off, acc = _sb_block(z, v_ref[pl.ds(block_start(j), tk), :], upper, off, acc, None)
        return t + 1, live(off), off, acc

    _, _, off, acc = lax.while_loop(cond, body, (jnp.int32(0), live(off), off, acc))
    o_ref[...] = acc.astype(o_ref.dtype)


def _stick_breaking(q, k_all, v_all, upper, nseq, seq_len, tq, q_pos0):
    t, hw = q.shape
    kpad = k_all.shape[1]
    nq = seq_len // tq
    half = SB_SUB
    assert SB_TK % tq == 0 and q_pos0 % tq == 0 and kpad % SB_TK == 0
    assert kpad >= ((q_pos0 + seq_len - 1) // SB_TK + 1) * SB_TK
    kern = functools.partial(_sb_kernel, tq=tq, q_pos0=q_pos0)
    qspec = pl.BlockSpec((tq, SB_HEAD_DIM), lambda b, h, i: (b * nq + i, h))
    kvspec = pl.BlockSpec((None, kpad, SB_HEAD_DIM), lambda b, h, i: (b, 0, h))
    return pl.pallas_call(
        kern,
        grid=(nseq, SB_HEADS, nq),
        in_specs=[qspec, kvspec, kvspec, pl.BlockSpec((half, half), lambda b, h, i: (0, 0))],
        out_specs=qspec,
        out_shape=jax.ShapeDtypeStruct((t, hw), BF16),
        scratch_shapes=[pltpu.VMEM((tq, SB_TK), F32)],
        compiler_params=_cparams("parallel", "parallel", "arbitrary"),
        name="stick_breaking",
    )(q, k_all, v_all, upper)


def _outproj_kernel(lru_ref, att_ref, x_ref, mod_ref, wo_ref, g_ref, b_ref, rw_ref, rb_ref,
                    x1_ref, tok_ref, code_ref, gate_ref, cnt_ref, *, alpha):
    i = pl.program_id(0)
    w = lru_ref.shape[-1]
    out = (jnp.dot(lru_ref[...], wo_ref[0:w, :], preferred_element_type=F32)
           + jnp.dot(att_ref[...], wo_ref[w:2 * w, :], preferred_element_type=F32))
    gate1 = mod_ref[0, 2:3, :]
    x1 = _layer_norm(alpha * x_ref[...] + gate1 * out, g_ref[...], b_ref[...])
    x1_ref[...] = x1
    tok = x1 * (1.0 + mod_ref[0, 4:5, :]) + mod_ref[0, 3:4, :]
    tok_ref[...] = tok

    logits = _dot_f32(tok, rw_ref[...]) + rb_ref[...]
    lane = lax.broadcasted_iota(jnp.int32, logits.shape, 1).astype(F32)
    code = jnp.zeros(logits.shape, F32)
    gates = jnp.zeros(logits.shape, F32)
    denom = jnp.zeros((logits.shape[0], 1), F32)
    top0 = None
    for kk in range(TOP_K):
        m = jnp.max(logits, axis=-1, keepdims=True)
        idx = jnp.min(jnp.where(logits == m, lane, float(LANES)), axis=-1, keepdims=True)
        hit = lane == idx
        if kk == 0:
            top0 = m
        e = jnp.exp(m - top0)
        denom = denom + e
        code = jnp.where(hit, float(kk + 1), code)
        gates = jnp.where(lane == kk, e, gates)
        logits = jnp.where(hit, -jnp.inf, logits)
    code_ref[...] = code.astype(BF16)
    gate_ref[...] = gates / denom

    @pl.when(i == 0)
    def _():
        cnt_ref[...] = jnp.zeros(cnt_ref.shape, F32)

    cnt_ref[...] += jnp.sum((code > 0.0).astype(F32), axis=0, keepdims=True)


def _outproj(lru, att, x2, mod, wo_b, g, b, rw_pad, rb_pad, seq0, seq_len, tm, alpha):
    t, d = x2.shape
    w = lru.shape[1]
    tiles_per_seq = seq_len // tm
    rowd = pl.BlockSpec((tm, d), lambda i: (i, 0))
    roww = pl.BlockSpec((tm, w), lambda i: (i, 0))
    rowl = pl.BlockSpec((tm, LANES), lambda i: (i, 0))
    vec = pl.BlockSpec((1, d), lambda i: (0, 0))
    return pl.pallas_call(
        functools.partial(_outproj_kernel, alpha=alpha),
        grid=(t // tm,),
        in_specs=[roww, roww, rowd,
                  pl.BlockSpec((1, 6, d), lambda i: (seq0 + i // tiles_per_seq, 0, 0)),
                  pl.BlockSpec((2 * w, d), lambda i: (0, 0)),
                  vec, vec,
                  pl.BlockSpec((d, LANES), lambda i: (0, 0)),
                  pl.BlockSpec((1, LANES), lambda i: (0, 0))],
        out_specs=[rowd, rowd, rowl, rowl, pl.BlockSpec((8, LANES), lambda i: (0, 0))],
        out_shape=[jax.ShapeDtypeStruct((t, d), F32), jax.ShapeDtypeStruct((t, d), F32),
                   jax.ShapeDtypeStruct((t, LANES), BF16), jax.ShapeDtypeStruct((t, LANES), F32),
                   jax.ShapeDtypeStruct((8, LANES), F32)],
        compiler_params=_cparams("arbitrary"),
        name="outproj_router",
    )(lru, att, x2, mod, wo_b, g.reshape(1, d), b.reshape(1, d), rw_pad, rb_pad)


def _rank_kernel(code_ref, cin_ref, ps_ref, dest_ref, cout_ref, carry_ref, low_ref):
    i = pl.program_id(0)
    tr = code_ref.shape[0]

    @pl.when(i == 0)
    def _():
        carry_ref[...] = cin_ref[0:1, :]
        r = lax.broadcasted_iota(jnp.int32, (tr, tr), 0)
        c = lax.broadcasted_iota(jnp.int32, (tr, tr), 1)
        low_ref[...] = (c < r).astype(BF16)

    code = code_ref[...].astype(F32)
    member = (code > 0.0).astype(BF16)
    before = jnp.dot(low_ref[...], member, preferred_element_type=F32) + carry_ref[...]
    base = before + ps_ref[...]
    lane = lax.broadcasted_iota(jnp.int32, code.shape, 1)
    dest = jnp.zeros(code.shape, F32)
    for kk in range(TOP_K):
        d = jnp.sum(jnp.where(code == float(kk + 1), base, 0.0), axis=-1, keepdims=True)
        dest = jnp.where(lane == kk, d, dest)
    dest_ref[...] = jnp.transpose(dest)[0:8, :].astype(jnp.int32)
    carry = before[tr - 1:tr, :] + (code[tr - 1:tr, :] > 0.0).astype(F32)
    carry_ref[...] = carry
    cout_ref[...] = jnp.broadcast_to(carry, cout_ref.shape)


def _rank(code, carry_in, pad_start, tr):
    t = code.shape[0]
    rowl = pl.BlockSpec((tr, LANES), lambda i: (i, 0))
    small = pl.BlockSpec((8, LANES), lambda i: (0, 0))
    return pl.pallas_call(
        _rank_kernel,
        grid=(t // tr,),
        in_specs=[rowl, small, pl.BlockSpec((1, LANES), lambda i: (0, 0))],
        out_specs=[pl.BlockSpec((8, tr), lambda i: (0, i)), small],
        out_shape=[jax.ShapeDtypeStruct((8, t), jnp.int32),
                   jax.ShapeDtypeStruct((8, LANES), F32)],
        scratch_shapes=[pltpu.VMEM((1, LANES), F32), pltpu.VMEM((tr, tr), BF16)],
        compiler_params=_cparams("arbitrary"),
        name="moe_rank",
    )(code, carry_in, pad_start)


def _dest_window(tm):
    width = max(tm, LANES)
    return width, width // tm


def _scatter_kernel(dest_ref, tok_ref, rows_in_ref, rows_ref, sem, *, tm, per):
    del rows_in_ref
    base = (pl.program_id(0) % per) * tm

    def row_copy(t, kk):
        return pltpu.make_async_copy(tok_ref.at[pl.ds(t, 1), :],
                                     rows_ref.at[pl.ds(dest_ref[kk, base + t], 1), :], sem)

    def start(g, c):
        t0 = pl.multiple_of(g * SUBLANES, SUBLANES)
        for s in range(SUBLANES):
            for kk in range(TOP_K):
                row_copy(t0 + s, kk).start()
        return c

    def wait(g, c):
        t0 = pl.multiple_of(g * SUBLANES, SUBLANES)
        for s in range(SUBLANES):
            for kk in range(TOP_K):
                row_copy(t0 + s, kk).wait()
        return c

    lax.fori_loop(0, tm // SUBLANES, start, 0)
    lax.fori_loop(0, tm // SUBLANES, wait, 0)


def _scatter(dest, tok, rows_buf, tm):
    t, d = tok.shape
    width, per = _dest_window(tm)
    return pl.pallas_call(
        functools.partial(_scatter_kernel, tm=tm, per=per),
        grid=(t // tm,),
        in_specs=[pl.BlockSpec((8, width), lambda i: (0, i // per), memory_space=pltpu.SMEM),
                  pl.BlockSpec((tm, d), lambda i: (i, 0)),
                  pl.BlockSpec(memory_space=pl.ANY)],
        out_specs=pl.BlockSpec(memory_space=pl.ANY),
        out_shape=jax.ShapeDtypeStruct(rows_buf.shape, rows_buf.dtype),
        scratch_shapes=[pltpu.SemaphoreType.DMA(())],
        input_output_aliases={2: 0},
        compiler_params=_cparams("arbitrary"),
        name="moe_scatter",
    )(dest, tok, rows_buf)


def _expert_kernel(be_ref, nu_ref, nv_ref, x_ref, wgu_ref, bgu_ref, wd_ref, bd_ref, y_ref, wgu_b, wd_b):
    dff = wd_ref.shape[1]
    i = pl.program_id(0)

    @pl.when(i < nu_ref[0])
    def _():
        @pl.when(jnp.logical_or(i == 0, be_ref[i] != be_ref[jnp.maximum(i - 1, 0)]))
        def _():
            wgu_b[...] = wgu_ref[0].astype(BF16)
            wd_b[...] = wd_ref[0].astype(BF16)

        row = lax.broadcasted_iota(jnp.int32, (x_ref.shape[0], 1), 0)
        xb = jnp.where(row < nv_ref[i], x_ref[...], 0.0).astype(BF16)
        gu = jnp.dot(xb, wgu_b[...], preferred_element_type=F32) + bgu_ref[0]
        gate = jnp.minimum(gu[:, :dff], SWIGLU_LIMIT)
        up = jnp.clip(gu[:, dff:], -SWIGLU_LIMIT, SWIGLU_LIMIT)
        act = (up + 1.0) * gate * jax.nn.sigmoid(SWIGLU_ALPHA * gate)
        y_ref[...] = jnp.dot(act.astype(BF16), wd_b[...], preferred_element_type=F32) + bd_ref[0]


def _experts(block_e, n_used, n_valid, x_rows, wgu, bgu, wd, bd, layer):
    rows, d = x_rows.shape
    depth, ne, _, n_gu = wgu.shape
    dff = wd.shape[2]
    nb = rows // MOE_BLOCK

    def expert(i, be, nu):
        return be[jnp.minimum(i, nu[0] - 1)]

    rowspec = pl.BlockSpec((MOE_BLOCK, d), lambda i, be, nu, nv: (jnp.minimum(i, nu[0] - 1), 0))
    return pl.pallas_call(
        _expert_kernel,
        grid_spec=pltpu.PrefetchScalarGridSpec(
            num_scalar_prefetch=3,
            grid=(nb,),
            in_specs=[rowspec,
                      pl.BlockSpec((None, 1, d, n_gu), lambda i, be, nu, nv: (layer, expert(i, be, nu), 0, 0)),
                      pl.BlockSpec((None, 1, 1, n_gu), lambda i, be, nu, nv: (layer, expert(i, be, nu), 0, 0)),
                      pl.BlockSpec((None, 1, dff, d), lambda i, be, nu, nv: (layer, expert(i, be, nu), 0, 0)),
                      pl.BlockSpec((None, 1, 1, d), lambda i, be, nu, nv: (layer, expert(i, be, nu), 0, 0))],
            out_specs=rowspec,
            scratch_shapes=[pltpu.VMEM((d, n_gu), BF16), pltpu.VMEM((dff, d), BF16)]),
        out_shape=jax.ShapeDtypeStruct((rows, d), F32),
        compiler_params=_cparams("arbitrary"),
        name="moe_experts",
    )(block_e, n_used, n_valid, x_rows, wgu, bgu.reshape(depth, ne, 1, n_gu), wd,
      bd.reshape(depth, ne, 1, d))


def _sc_gather(rows, idx):
    n = idx.shape[0]
    d = rows.shape[1]
    mesh = plsc.VectorSubcoreMesh(core_axis_name="c", subcore_axis_name="s",
                                  num_cores=SC_CORES, num_subcores=SC_SUBCORES)
    n_workers = mesh.num_cores * mesh.num_subcores
    per_worker = n // n_workers
    assert per_worker * n_workers == n and per_worker % LANES == 0

    @functools.partial(pl.kernel, out_type=jax.ShapeDtypeStruct((n, d), rows.dtype), mesh=mesh,
                       scratch_types=[pltpu.VMEM((LANES,), jnp.int32),
                                      pltpu.VMEM((SC_WINDOW, d), rows.dtype),
                                      pltpu.VMEM((SC_WINDOW, d), rows.dtype),
                                      pltpu.SemaphoreType.DMA((2,)),
                                      pltpu.SemaphoreType.DMA((2,))],
                       compiler_params=pltpu.CompilerParams(use_tc_tiling_on_sc=True),
                       name="moe_gather")
    def gather(rows_hbm, idx_hbm, out_hbm, idx_vmem, buf0, buf1, gsem, wsem):
        worker = lax.axis_index("c") * mesh.num_subcores + lax.axis_index("s")
        bufs = (buf0, buf1)
        n_win = LANES // SC_WINDOW

        @pl.loop(0, per_worker // LANES)
        def _(j):
            chunk = worker * (per_worker // LANES) + j
            pltpu.sync_copy(idx_hbm.at[chunk], idx_vmem)

            def write(q):
                return pltpu.async_copy(
                    bufs[q % 2], out_hbm.at[pl.ds(chunk * LANES + q * SC_WINDOW, SC_WINDOW)],
                    wsem.at[q % 2])

            gathers, writes = [], []
            for q in range(n_win):
                if q >= 2:
                    writes[q - 2].wait()
                gathers.append(pltpu.async_copy(
                    rows_hbm.at[idx_vmem.at[pl.ds(q * SC_WINDOW, SC_WINDOW)]], bufs[q % 2],
                    gsem.at[q % 2]))
                if q >= 1:
                    gathers[q - 1].wait()
                    writes.append(write(q - 1))
            gathers[n_win - 1].wait()
            writes.append(write(n_win - 1))
            writes[n_win - 2].wait()
            writes[n_win - 1].wait()

    return gather(rows, idx.reshape(n // LANES, LANES))


def _sc_scatter(tok, dest, n_rows):
    t, d = tok.shape
    mesh = plsc.VectorSubcoreMesh(core_axis_name="c", subcore_axis_name="s",
                                  num_cores=SC_CORES, num_subcores=SC_SUBCORES)
    n_workers = mesh.num_cores * mesh.num_subcores
    per_worker = t // n_workers
    assert per_worker * n_workers == t and per_worker % LANES == 0
    chunks = t // LANES
    n_win = LANES // SC_WINDOW
    lanes16 = 16

    @functools.partial(pl.kernel, out_type=jax.ShapeDtypeStruct((n_rows, d), tok.dtype), mesh=mesh,
                       scratch_types=[pltpu.VMEM((TOP_K, LANES), jnp.int32),
                                      pltpu.VMEM((SC_WINDOW, d), tok.dtype),
                                      pltpu.VMEM((SC_WINDOW, d), tok.dtype),
                                      pltpu.SemaphoreType.DMA((2,)),
                                      pltpu.SemaphoreType.DMA((2,))],
                       compiler_params=pltpu.CompilerParams(use_tc_tiling_on_sc=True),
                       name="moe_scatter_sc")
    def scatter(tok_hbm, idx_hbm, rows_hbm, idx_vmem, buf0, buf1, lsem, ssem):
        worker = lax.axis_index("c") * mesh.num_subcores + lax.axis_index("s")
        bufs = (buf0, buf1)

        @pl.loop(0, per_worker // LANES)
        def _(j):
            chunk = worker * (per_worker // LANES) + j
            for kk in range(TOP_K):
                pltpu.sync_copy(idx_hbm.at[kk * chunks + chunk], idx_vmem.at[kk])

            def load(q):
                return pltpu.async_copy(
                    tok_hbm.at[pl.ds(chunk * LANES + q * SC_WINDOW, SC_WINDOW)], bufs[q % 2],
                    lsem.at[q % 2])

            def store_all(q):
                copies = []
                for kk in range(TOP_K):
                    for h in range(SC_WINDOW // lanes16):
                        rows16 = idx_vmem[kk, pl.ds(q * SC_WINDOW + h * lanes16, lanes16)]
                        copies.append(pltpu.async_copy(
                            bufs[q % 2].at[pl.ds(h * lanes16, lanes16)], rows_hbm.at[rows16],
                            ssem.at[q % 2]))
                return copies

            loads = [load(0)]
            stores = []
            for q in range(n_win):
                if q + 1 < n_win:
                    if q >= 1:
                        for cp in stores[q - 1]:
                            cp.wait()
                    loads.append(load(q + 1))
                loads[q].wait()
                stores.append(store_all(q))
            for q in (n_win - 2, n_win - 1):
                for cp in stores[q]:
                    cp.wait()

    return scatter(tok, dest[:TOP_K].reshape(TOP_K * chunks, LANES))


def _combine_dense_kernel(y0_ref, y1_ref, y2_ref, y3_ref, gate_ref, x_ref, mod_ref, g_ref, b_ref,
                          o_ref, *, alpha):
    gates = gate_ref[...]
    f = y0_ref[...] * gates[:, 0:1]
    for kk, y_ref in enumerate((y1_ref, y2_ref, y3_ref), start=1):
        f = f + y_ref[...] * gates[:, kk:kk + 1]
    gate2 = mod_ref[0, 5:6, :]
    o_ref[...] = _layer_norm(alpha * x_ref[...] + gate2 * f, g_ref[...], b_ref[...])


def _combine_dense(y_slots, gates, x1, mod, g, b, seq0, seq_len, tm, alpha):
    t, d = x1.shape
    nt = t // tm
    tiles_per_seq = seq_len // tm
    rowd = pl.BlockSpec((tm, d), lambda i: (i, 0))
    vec = pl.BlockSpec((1, d), lambda i: (0, 0))
    slot_specs = [pl.BlockSpec((tm, d), functools.partial(lambda i, kk: (kk * nt + i, 0), kk=kk))
                  for kk in range(TOP_K)]
    return pl.pallas_call(
        functools.partial(_combine_dense_kernel, alpha=alpha),
        grid=(nt,),
        in_specs=slot_specs + [pl.BlockSpec((tm, LANES), lambda i: (i, 0)), rowd,
                               pl.BlockSpec((1, 6, d), lambda i: (seq0 + i // tiles_per_seq, 0, 0)),
                               vec, vec],
        out_specs=rowd,
        out_shape=jax.ShapeDtypeStruct((t, d), F32),
        compiler_params=_cparams("parallel"),
        name="moe_combine",
    )(y_slots, y_slots, y_slots, y_slots, gates, x1, mod, g.reshape(1, d), b.reshape(1, d))


def _combine_kernel(dest_ref, y_hbm, gate_ref, x_ref, mod_ref, g_ref, b_ref, o_ref, ybuf, sem,
                    *, alpha, per):
    tm = x_ref.shape[0]
    base = (pl.program_id(0) % per) * tm

    def row_copy(t, kk):
        return pltpu.make_async_copy(y_hbm.at[pl.ds(dest_ref[kk, base + t], 1), :],
                                     ybuf.at[kk, pl.ds(t, 1), :], sem)

    def start(g, c):
        t0 = pl.multiple_of(g * SUBLANES, SUBLANES)
        for s in range(SUBLANES):
            for kk in range(TOP_K):
                row_copy(t0 + s, kk).start()
        return c

    def wait(g, c):
        t0 = pl.multiple_of(g * SUBLANES, SUBLANES)
        for s in range(SUBLANES):
            for kk in range(TOP_K):
                row_copy(t0 + s, kk).wait()
        return c

    lax.fori_loop(0, tm // SUBLANES, start, 0)
    lax.fori_loop(0, tm // SUBLANES, wait, 0)

    gates = gate_ref[...]
    f = ybuf[0] * gates[:, 0:1]
    for kk in range(1, TOP_K):
        f = f + ybuf[kk] * gates[:, kk:kk + 1]
    gate2 = mod_ref[0, 5:6, :]
    o_ref[...] = _layer_norm(alpha * x_ref[...] + gate2 * f, g_ref[...], b_ref[...])


def _combine(dest, y_rows, gates, x1, mod, g, b, seq0, seq_len, tm, alpha):
    t, d = x1.shape
    tiles_per_seq = seq_len // tm
    width, per = _dest_window(tm)
    rowd = pl.BlockSpec((tm, d), lambda i: (i, 0))
    vec = pl.BlockSpec((1, d), lambda i: (0, 0))
    return pl.pallas_call(
        functools.partial(_combine_kernel, alpha=alpha, per=per),
        grid=(t // tm,),
        in_specs=[pl.BlockSpec((8, width), lambda i: (0, i // per), memory_space=pltpu.SMEM),
                  pl.BlockSpec(memory_space=pl.ANY),
                  pl.BlockSpec((tm, LANES), lambda i: (i, 0)),
                  rowd,
                  pl.BlockSpec((1, 6, d), lambda i: (seq0 + i // tiles_per_seq, 0, 0)),
                  vec, vec],
        out_specs=rowd,
        out_shape=jax.ShapeDtypeStruct((t, d), F32),
        scratch_shapes=[pltpu.VMEM((TOP_K, tm, d), F32), pltpu.SemaphoreType.DMA(())],
        compiler_params=_cparams("arbitrary"),
        name="moe_combine",
    )(dest, y_rows, gates, x1, mod, g.reshape(1, d), b.reshape(1, d))


def _block_diag(wh):
    heads, hd, _ = wh.shape
    eye = jnp.eye(heads, dtype=wh.dtype)
    return jnp.einsum('hij,hg->higj', wh, eye).reshape(heads * hd, heads * hd)


def kernel(x_prompt, x_sample, cache_k, cache_v, state_conv, state_lru, c_prompt, c_sample, ln_in_g, ln_in_b, w_ada, b_ada, w_in, conv_w, conv_b, gate_a_w, gate_a_b, gate_x_w, gate_x_b, lru_lambda, w_out, ln1_g, ln1_b, router_w, router_b, w_gu, b_gu, w_down, b_down, ln2_g, ln2_b):
    nb_p, s_p, d = x_prompt.shape
    nb_s, s_s, _ = x_sample.shape
    depth = w_in.shape[0]
    past = cache_k.shape[2]
    w = state_lru.shape[-1]
    hw = SB_HEADS * SB_HEAD_DIM
    t_p, t_s = nb_p * s_p, nb_s * s_s
    alpha = float((2 * depth) ** 0.25)

    tm_p, tm_s = min(512, s_p), min(512, s_s)
    ts_p, ts_s = min(256, s_p), min(256, s_s)
    tq_p, tq_s = min(SB_TQ, s_p), min(SB_TQ, s_s)
    tg_p, tg_s = min(256, s_p), min(256, s_s)
    tr_p, tr_s = min(512, t_p), min(512, t_s)

    n_seq = nb_p + nb_s
    c_all = jnp.concatenate([c_prompt, c_sample, jnp.zeros((-n_seq % 8, d), F32)], axis=0)
    mod_all = _adaln(c_all, w_ada, b_ada).reshape(depth, c_all.shape[0], 6, d)

    xp = _ln_in(x_prompt.reshape(t_p, d), ln_in_g, ln_in_b, tm_p)
    xs = _ln_in(x_sample.reshape(t_s, d), ln_in_g, ln_in_b, tm_s)

    upper = jnp.asarray(np.arange(SB_SUB)[:, None] > np.arange(SB_SUB)[None, :], BF16)
    kpad_s = -(-(past + s_s) // SB_TK) * SB_TK
    zero_conv = jnp.zeros((nb_p, CONV_WIDTH - 1, w), F32)
    zero_h = jnp.zeros((nb_p, w), F32)

    tk_total = (t_p + t_s) * TOP_K
    n_blocks = -(-(tk_total + N_EXPERTS * (MOE_BLOCK - 1)) // MOE_BLOCK)
    n_rows = n_blocks * MOE_BLOCK

    outs = [[] for _ in range(8)]
    kv_p = kv_s = None
    for l in range(depth):
        mod = mod_all[l]
        w_in_b = w_in[l].astype(BF16)
        wo_b = w_out[l].astype(BF16)
        wa = _block_diag(gate_a_w[l]).astype(BF16)
        wx = _block_diag(gate_x_w[l]).astype(BF16)
        rw_pad = jnp.pad(router_w[l], ((0, 0), (0, LANES - N_EXPERTS)))
        rb_pad = jnp.pad(router_b[l], (0, LANES - N_EXPERTS), constant_values=NEG_BIG).reshape(1, LANES)
        lru_w = (conv_w[l], conv_b[l], wa, gate_a_b[l], wx, gate_x_b[l], lru_lambda[l])

        xr, gy, q, k, v, kb, vb = _inproj(xp, mod, w_in_b, 0, s_p, tm_p, l, depth, kv_p)
        kv_p = (k, v)
        lru_p, nconv_p, nh_p = _rglru(xr, gy, zero_conv, zero_h, *lru_w, nb_p, s_p, ts_p)
        att_p = _stick_breaking(q, kb.reshape(nb_p, s_p, hw), vb.reshape(nb_p, s_p, hw), upper,
                                nb_p, s_p, tq_p, 0)
        outs[2].append(nconv_p)
        outs[3].append(nh_p.reshape(nb_p, w))

        xr, gy, q, k, v, kb, vb = _inproj(xs, mod, w_in_b, nb_p, s_s, tm_s, l, depth, kv_s)
        kv_s = (k, v)
        lru_s, nconv_s, nh_s = _rglru(xr, gy, state_conv[l], state_lru[l], *lru_w, nb_s, s_s, ts_s)
        kv_pad = jnp.zeros((nb_s, kpad_s - past - s_s, hw), BF16)
        k_all = jnp.concatenate([cache_k[l].reshape(nb_s, past, hw).astype(BF16),
                                 kb.reshape(nb_s, s_s, hw), kv_pad], axis=1)
        v_all = jnp.concatenate([cache_v[l].reshape(nb_s, past, hw).astype(BF16),
                                 vb.reshape(nb_s, s_s, hw), kv_pad], axis=1)
        att_s = _stick_breaking(q, k_all, v_all, upper, nb_s, s_s, tq_s, past)
        outs[6].append(nconv_s)
        outs[7].append(nh_s.reshape(nb_s, w))

        x1_p, tok_p, code_p, gate_p, cnt_p = _outproj(lru_p, att_p, xp, mod, wo_b, ln1_g[l], ln1_b[l],
                                                      rw_pad, rb_pad, 0, s_p, tm_p, alpha)
        x1_s, tok_s, code_s, gate_s, cnt_s = _outproj(lru_s, att_s, xs, mod, wo_b, ln1_g[l], ln1_b[l],
                                                      rw_pad, rb_pad, nb_p, s_s, tm_s, alpha)

        counts = (cnt_p[0, :N_EXPERTS] + cnt_s[0, :N_EXPERTS]).astype(jnp.int32)
        padded = (counts + MOE_BLOCK - 1) // MOE_BLOCK * MOE_BLOCK
        pad_end = jnp.cumsum(padded)
        pad_start = jnp.pad((pad_end - padded).astype(F32), (0, LANES - N_EXPERTS)).reshape(1, LANES)
        block_row0 = jnp.arange(n_blocks, dtype=jnp.int32) * MOE_BLOCK
        block_e = jnp.minimum(jnp.sum(pad_end[None, :] <= block_row0[:, None], axis=1),
                              N_EXPERTS - 1).astype(jnp.int32)

        n_used = (pad_end[N_EXPERTS - 1:] // MOE_BLOCK).astype(jnp.int32)
        n_valid = jnp.clip(counts[block_e] - (block_row0 - (pad_end - padded)[block_e]), 0, MOE_BLOCK)

        dest_p, carry = _rank(code_p, jnp.zeros((8, LANES), F32), pad_start, tr_p)
        dest_s, _ = _rank(code_s, carry, pad_start, tr_s)

        x_rows = _sc_scatter(tok_p, dest_p, n_rows)
        x_rows = _scatter(dest_s, tok_s, x_rows, tg_s)
        y_rows = _experts(block_e, n_used, n_valid, x_rows, w_gu, b_gu, w_down, b_down, l)

        yg_p = _sc_gather(y_rows, dest_p[:TOP_K].reshape(TOP_K * t_p))
        xp = _combine_dense(yg_p, gate_p, x1_p, mod, ln2_g[l], ln2_b[l], 0, s_p, tg_p, alpha)
        xs = _combine(dest_s, y_rows, gate_s, x1_s, mod, ln2_g[l], ln2_b[l], nb_p, s_s, tg_s, alpha)

    return (xp.reshape(nb_p, s_p, d), xs.reshape(nb_s, s_s, d),
            kv_p[0].reshape(depth, nb_p, s_p, SB_HEADS, SB_HEAD_DIM),
            kv_p[1].reshape(depth, nb_p, s_p, SB_HEADS, SB_HEAD_DIM),
            jnp.stack(outs[2]), jnp.stack(outs[3]),
            kv_s[0].reshape(depth, nb_s, s_s, SB_HEADS, SB_HEAD_DIM),
            kv_s[1].reshape(depth, nb_s, s_s, SB_HEADS, SB_HEAD_DIM),
            jnp.stack(outs[6]), jnp.stack(outs[7]))
```

```python
import functools

import jax
import jax.numpy as jnp
import numpy as np
from jax import lax
from jax.experimental import pallas as pl
from jax.experimental.pallas import tpu as pltpu
from jax.experimental.pallas import tpu_sc as plsc

F32 = jnp.float32
BF16 = jnp.bfloat16

LANES = 128
SUBLANES = 8
LRU_HEADS = 8
CONV_WIDTH = 4
RG_C = 8.0
SB_HEADS = 4
SB_HEAD_DIM = 128
SB_SCALE = SB_HEAD_DIM ** -0.5
SB_SUB = 256
SB_TQ = 512
SB_TK = 512
SB_DEAD_LOG2 = 160.0
LOG2E = 1.4426950408889634
N_EXPERTS = 32
TOP_K = 4
SWIGLU_LIMIT = 7.0
SWIGLU_ALPHA = 1.702
MOE_BLOCK = 512
SC_CORES, SC_SUBCORES = 2, 16
SC_WINDOW = 32
LN_EPS = 1e-5
NEG_BIG = -1e30
VMEM_LIMIT = 56 * 1024 * 1024


def _cparams(*sem):
    return pltpu.CompilerParams(dimension_semantics=sem, vmem_limit_bytes=VMEM_LIMIT)


def _split3(a):
    hi = a.astype(BF16)
    r1 = a - hi.astype(F32)
    mid = r1.astype(BF16)
    lo = (r1 - mid.astype(F32)).astype(BF16)
    return hi, mid, lo


def _dot_f32(a, b):
    a0, a1, a2 = _split3(a)
    b0, b1, b2 = _split3(b)
    d = functools.partial(jnp.dot, preferred_element_type=F32)
    return (d(a0, b0) + (d(a0, b1) + d(a1, b0))
            + (d(a0, b2) + d(a1, b1) + d(a2, b0)))


def _layer_norm(y, g, b):
    mu = jnp.mean(y, axis=-1, keepdims=True)
    d = y - mu
    var = jnp.mean(d * d, axis=-1, keepdims=True)
    return d * lax.rsqrt(var + LN_EPS) * g + b


def _softplus(z):
    return jnp.maximum(z, 0.0) + jnp.log(1.0 + jnp.exp(-jnp.abs(z)))


def _pack_bf16_pairs(x):
    h = x.shape[1] // 2
    bits = lax.bitcast_convert_type(x.astype(BF16).astype(F32), jnp.int32)
    return lax.shift_right_logical(bits[:, :h], 16) | (bits[:, h:] & jnp.int32(-65536))


def _unpack_bf16_pairs(p):
    lo = lax.bitcast_convert_type(p << 16, F32)
    hi = lax.bitcast_convert_type(p & jnp.int32(-65536), F32)
    return jnp.concatenate([lo, hi], axis=1)


def _gelu_tanh(x):
    c = np.sqrt(2.0 / np.pi).astype(np.float32)
    return 0.5 * x * (1.0 + jnp.tanh(c * (x + 0.044715 * (x * x * x))))


def _adaln_kernel(c_ref, w_ref, b_ref, o_ref):
    c = c_ref[...]
    s = c * jax.nn.sigmoid(c)
    o_ref[0] = _dot_f32(s, w_ref[0]) + b_ref[0]


def _adaln(c_all, w_ada, b_ada):
    depth, d, n = w_ada.shape
    rows = c_all.shape[0]
    tn = 1024
    return pl.pallas_call(
        _adaln_kernel,
        grid=(depth, n // tn),
        in_specs=[pl.BlockSpec((rows, d), lambda l, j: (0, 0)),
                  pl.BlockSpec((1, d, tn), lambda l, j: (l, 0, j)),
                  pl.BlockSpec((1, 1, tn), lambda l, j: (l, 0, j))],
        out_specs=pl.BlockSpec((1, rows, tn), lambda l, j: (l, 0, j)),
        out_shape=jax.ShapeDtypeStruct((depth, rows, n), F32),
        compiler_params=_cparams("parallel", "parallel"),
        name="adaln",
    )(c_all, w_ada, b_ada.reshape(depth, 1, n))


def _ln_kernel(x_ref, g_ref, b_ref, o_ref):
    o_ref[...] = _layer_norm(x_ref[...], g_ref[...], b_ref[...])


def _ln_in(x2, g, b, tm):
    t, d = x2.shape
    return pl.pallas_call(
        _ln_kernel,
        grid=(t // tm,),
        in_specs=[pl.BlockSpec((tm, d), lambda i: (i, 0)),
                  pl.BlockSpec((1, d), lambda i: (0, 0)),
                  pl.BlockSpec((1, d), lambda i: (0, 0))],
        out_specs=pl.BlockSpec((tm, d), lambda i: (i, 0)),
        out_shape=jax.ShapeDtypeStruct((t, d), F32),
        compiler_params=_cparams("parallel"),
        name="ln_in",
    )(x2, g.reshape(1, d), b.reshape(1, d))


def _inproj_kernel(x_ref, mod_ref, w_ref, *rest):
    xr_ref, gy_ref, q_ref, k_ref, v_ref, kb_ref, vb_ref = rest[-7:]
    w = xr_ref.shape[-1]

    def store_heads(ref, val):
        for hh in range(SB_HEADS):
            ref[:, hh, :] = val[:, hh * SB_HEAD_DIM:(hh + 1) * SB_HEAD_DIM]

    shift = mod_ref[0, 0:1, :]
    scale = mod_ref[0, 1:2, :]
    h = (x_ref[...] * (1.0 + scale) + shift).astype(BF16)

    def proj(j):
        return jnp.dot(h, w_ref[:, j * w:(j + 1) * w], preferred_element_type=F32)

    xr_ref[...] = proj(0)
    gy_ref[...] = _gelu_tanh(proj(1))
    q_ref[...] = (proj(2) * (SB_SCALE * LOG2E)).astype(BF16)
    k = proj(3)
    store_heads(k_ref, k)
    kb_ref[...] = k.astype(BF16)
    v = proj(4)
    store_heads(v_ref, v)
    vb_ref[...] = v.astype(BF16)


def _inproj(x2, mod, w_in_b, seq0, seq_len, tm, layer, depth, kv_stacks):
    t, d = x2.shape
    w = w_in_b.shape[1] // 5
    nt = t // tm
    tiles_per_seq = seq_len // tm
    row = pl.BlockSpec((tm, w), lambda i: (i, 0))
    heads = pl.BlockSpec((tm, SB_HEADS, SB_HEAD_DIM), lambda i: (layer * nt + i, 0, 0))
    f32o = jax.ShapeDtypeStruct((t, w), F32)
    bf16o = jax.ShapeDtypeStruct((t, w), BF16)
    stack = jax.ShapeDtypeStruct((depth * t, SB_HEADS, SB_HEAD_DIM), F32)
    in_specs = [pl.BlockSpec((tm, d), lambda i: (i, 0)),
                pl.BlockSpec((1, 6, d), lambda i: (seq0 + i // tiles_per_seq, 0, 0)),
                pl.BlockSpec((d, 5 * w), lambda i: (0, 0))]
    args = [x2, mod, w_in_b]
    aliases = {}
    if kv_stacks is not None:
        in_specs += [pl.BlockSpec(memory_space=pl.ANY)] * 2
        args += list(kv_stacks)
        aliases = {3: 3, 4: 4}
    return pl.pallas_call(
        _inproj_kernel,
        grid=(nt,),
        in_specs=in_specs,
        out_specs=[row, row, row, heads, heads, row, row],
        out_shape=[f32o, f32o, bf16o, stack, stack, bf16o, bf16o],
        input_output_aliases=aliases,
        compiler_params=_cparams("parallel"),
        name="inproj",
    )(*args)


def _rglru_kernel(xr_ref, gy_ref, cs_ref, h0_ref, cw_ref, cb_ref, wa_ref, ba_ref, wx_ref, bx_ref,
                  lam_ref, out_ref, nconv_ref, nh_ref, tail_ref, h_ref):
    ts, w = xr_ref.shape
    j = pl.program_id(1)

    @pl.when(j == 0)
    def _():
        tail_ref[...] = jnp.zeros((SUBLANES, w), F32)
        tail_ref[SUBLANES - (CONV_WIDTH - 1):SUBLANES, :] = cs_ref[0]
        h_ref[...] = h0_ref[0]

    xr = xr_ref[...]
    prev = tail_ref[...]
    row8 = lax.broadcasted_iota(jnp.int32, (SUBLANES, w), 0)
    xc = cb_ref[...] + xr * cw_ref[CONV_WIDTH - 1:CONV_WIDTH, :]
    for dly in range(1, CONV_WIDTH):
        sh = pltpu.roll(xr, dly, 0)
        head = jnp.where(row8 < dly, pltpu.roll(prev, dly, 0), sh[0:SUBLANES, :])
        sh = jnp.concatenate([head, sh[SUBLANES:, :]], axis=0)
        xc = xc + sh * cw_ref[CONV_WIDTH - 1 - dly:CONV_WIDTH - dly, :]
    nconv_ref[0] = xr[ts - (CONV_WIDTH - 1):ts, :]
    tail_ref[...] = xr[ts - SUBLANES:ts, :]

    xcb = xc.astype(BF16)
    r = jax.nn.sigmoid(jnp.dot(xcb, wa_ref[...], preferred_element_type=F32) + ba_ref[...])
    gi = jax.nn.sigmoid(jnp.dot(xcb, wx_ref[...], preferred_element_type=F32) + bx_ref[...])
    log_a = (RG_C * r) * (-_softplus(-lam_ref[...]))
    a = jnp.exp(log_a)
    one_m = 1.0 - a * a
    b = jnp.where(one_m > 0.0, one_m * lax.rsqrt(one_m), 0.0) * (gi * xc)

    sub = lax.broadcasted_iota(jnp.int32, (ts, w), 0) & (SUBLANES - 1)
    dist = 1
    while dist < SUBLANES:
        keep = sub >= dist
        a_prev = jnp.where(keep, pltpu.roll(a, dist, 0), 1.0)
        b_prev = jnp.where(keep, pltpu.roll(b, dist, 0), 0.0)
        b = a * b_prev + b
        a = a * a_prev
        dist *= 2
    carry = h_ref[...]
    groups = []
    for g0 in range(0, ts, SUBLANES):
        hg = b[g0:g0 + SUBLANES, :] + a[g0:g0 + SUBLANES, :] * carry
        groups.append(hg)
        carry = hg[SUBLANES - 1:SUBLANES, :]
    hs = jnp.concatenate(groups, axis=0)
    h_last = carry
    h_ref[...] = h_last
    nh_ref[0] = h_last
    out_ref[...] = (hs * gy_ref[...]).astype(BF16)


def _rglru(xr, gy, conv_state, h0, cw, cb, wa, ba, wx, bx, lam, nseq, seq_len, ts):
    t, w = xr.shape
    n = seq_len // ts
    row = pl.BlockSpec((ts, w), lambda b, j: (b * n + j, 0))
    vec = pl.BlockSpec((1, w), lambda b, j: (0, 0))
    return pl.pallas_call(
        _rglru_kernel,
        grid=(nseq, n),
        in_specs=[row, row,
                  pl.BlockSpec((1, CONV_WIDTH - 1, w), lambda b, j: (b, 0, 0)),
                  pl.BlockSpec((1, 1, w), lambda b, j: (b, 0, 0)),
                  pl.BlockSpec((CONV_WIDTH, w), lambda b, j: (0, 0)),
                  vec,
                  pl.BlockSpec((w, w), lambda b, j: (0, 0)), vec,
                  pl.BlockSpec((w, w), lambda b, j: (0, 0)), vec,
                  vec],
        out_specs=[row,
                   pl.BlockSpec((1, CONV_WIDTH - 1, w), lambda b, j: (b, 0, 0)),
                   pl.BlockSpec((1, 1, w), lambda b, j: (b, 0, 0))],
        out_shape=[jax.ShapeDtypeStruct((t, w), BF16),
                   jax.ShapeDtypeStruct((nseq, CONV_WIDTH - 1, w), F32),
                   jax.ShapeDtypeStruct((nseq, 1, w), F32)],
        scratch_shapes=[pltpu.VMEM((SUBLANES, w), F32), pltpu.VMEM((1, w), F32)],
        compiler_params=_cparams("parallel", "arbitrary"),
        name="rglru",
    )(xr, gy, conv_state, h0.reshape(nseq, 1, w), cw, cb.reshape(1, w), wa, ba.reshape(1, w),
      wx, bx.reshape(1, w), lam.reshape(1, w))


def _sb_scores(q, k):
    return lax.dot_general(q, k, (((1,), (1,)), ((), ())), preferred_element_type=F32)


def _sb_block(z, v, upper, off, acc, mask):
    neg_abs = lax.bitcast_convert_type(
        lax.bitcast_convert_type(z, jnp.int32) | jnp.int32(-2 ** 31), F32)
    sp = jnp.maximum(z, 0.0) + jnp.log2(1.0 + jnp.exp2(neg_abs))
    if mask is not None:
        sp = jnp.where(mask, sp, 0.0)
    spb = sp.astype(BF16)
    later = []
    for lo in range(SB_TK - SB_SUB, -1, -SB_SUB):
        later.append(jnp.dot(spb[:, lo:lo + SB_SUB], upper, preferred_element_type=F32) + off)
        off = off + jnp.sum(sp[:, lo:lo + SB_SUB], axis=-1, keepdims=True)
    wgt = jnp.exp2((z - sp) - jnp.concatenate(later[::-1], axis=1))
    if mask is not None:
        wgt = jnp.where(mask, wgt, 0.0)
    acc = acc + jnp.dot(wgt.astype(BF16), v, preferred_element_type=F32)
    return off, acc


def _sb_kernel(q_ref, k_ref, v_ref, up_ref, o_ref, z_ref, *, tq, q_pos0):
    i = pl.program_id(2)
    tk = SB_TK
    q = q_ref[...]
    upper = up_ref[...]
    n_full = (q_pos0 + i * tq) // tk
    off = jnp.zeros((tq, 1), F32)
    acc = jnp.zeros((tq, SB_HEAD_DIM), F32)

    def block_start(j):
        return pl.multiple_of(j * tk, tk)

    z_ref[...] = _sb_scores(q, k_ref[pl.ds(block_start(jnp.maximum(n_full - 1, 0)), tk), :])

    start = block_start(n_full)
    q_pos = q_pos0 + i * tq + lax.broadcasted_iota(jnp.int32, (tq, tk), 0)
    mask = (start + lax.broadcasted_iota(jnp.int32, (tq, tk), 1)) < q_pos
    off, acc = _sb_block(_sb_scores(q, k_ref[pl.ds(start, tk), :]), v_ref[pl.ds(start, tk), :],
                         upper, off, acc, mask)

    def live(off):
        return (jnp.min(off, axis=0, keepdims=True)[0, 0] < SB_DEAD_LOG2).astype(jnp.int32)

    def cond(carry):
        t, alive, _, _ = carry
        return jnp.logical_and(t < n_full, alive > 0)

    def body(carry):
        t, _, off, acc = carry
        j = n_full - 1 - t
        z = z_ref[...]
        z_ref[...] = _sb_scores(q, k_ref[pl.ds(block_start(jnp.maximum(j - 1, 0)), tk), :])
        off, acc = _sb_block(z, v_ref[pl.ds(block_start(j), tk), :], upper, off, acc, None)
        return t + 1, live(off), off, acc

    _, _, off, acc = lax.while_loop(cond, body, (jnp.int32(0), live(off), off, acc))
    o_ref[...] = acc.astype(o_ref.dtype)


def _stick_breaking(q, k_all, v_all, upper, nseq, seq_len, tq, q_pos0):
    t, hw = q.shape
    kpad = k_all.shape[1]
    nq = seq_len // tq
    half = SB_SUB
    assert SB_TK % tq == 0 and q_pos0 % tq == 0 and kpad % SB_TK == 0
    assert kpad >= ((q_pos0 + seq_len - 1) // SB_TK + 1) * SB_TK
    kern = functools.partial(_sb_kernel, tq=tq, q_pos0=q_pos0)
    qspec = pl.BlockSpec((tq, SB_HEAD_DIM), lambda b, h, i: (b * nq + i, h))
    kvspec = pl.BlockSpec((None, kpad, SB_HEAD_DIM), lambda b, h, i: (b, 0, h))
    return pl.pallas_call(
        kern,
        grid=(nseq, SB_HEADS, nq),
        in_specs=[qspec, kvspec, kvspec, pl.BlockSpec((half, half), lambda b, h, i: (0, 0))],
        out_specs=qspec,
        out_shape=jax.ShapeDtypeStruct((t, hw), BF16),
        scratch_shapes=[pltpu.VMEM((tq, SB_TK), F32)],
        compiler_params=_cparams("parallel", "parallel", "arbitrary"),
        name="stick_breaking",
    )(q, k_all, v_all, upper)


def _outproj_kernel(lru_ref, att_ref, x_ref, mod_ref, wo_ref, g_ref, b_ref, rw_ref, rb_ref,
                    x1_ref, tok_ref, code_ref, gate_ref, cnt_ref, *, alpha):
    i = pl.program_id(0)
    w = lru_ref.shape[-1]
    out = (jnp.dot(lru_ref[...], wo_ref[0:w, :], preferred_element_type=F32)
           + jnp.dot(att_ref[...], wo_ref[w:2 * w, :], preferred_element_type=F32))
    gate1 = mod_ref[0, 2:3, :]
    x1 = _layer_norm(alpha * x_ref[...] + gate1 * out, g_ref[...], b_ref[...])
    x1_ref[...] = x1
    tok = x1 * (1.0 + mod_ref[0, 4:5, :]) + mod_ref[0, 3:4, :]
    tok_ref[...] = _pack_bf16_pairs(tok)

    logits = _dot_f32(tok, rw_ref[...]) + rb_ref[...]
    lane = lax.broadcasted_iota(jnp.int32, logits.shape, 1).astype(F32)
    code = jnp.zeros(logits.shape, F32)
    gates = jnp.zeros(logits.shape, F32)
    denom = jnp.zeros((logits.shape[0], 1), F32)
    top0 = None
    for kk in range(TOP_K):
        m = jnp.max(logits, axis=-1, keepdims=True)
        idx = jnp.min(jnp.where(logits == m, lane, float(LANES)), axis=-1, keepdims=True)
        hit = lane == idx
        if kk == 0:
            top0 = m
        e = jnp.exp(m - top0)
        denom = denom + e
        code = jnp.where(hit, float(kk + 1), code)
        gates = jnp.where(lane == kk, e, gates)
        logits = jnp.where(hit, -jnp.inf, logits)
    code_ref[...] = code.astype(BF16)
    gate_ref[...] = gates / denom

    @pl.when(i == 0)
    def _():
        cnt_ref[...] = jnp.zeros(cnt_ref.shape, F32)

    cnt_ref[...] += jnp.sum((code > 0.0).astype(F32), axis=0, keepdims=True)


def _outproj(lru, att, x2, mod, wo_b, g, b, rw_pad, rb_pad, seq0, seq_len, tm, alpha):
    t, d = x2.shape
    w = lru.shape[1]
    tiles_per_seq = seq_len // tm
    rowd = pl.BlockSpec((tm, d), lambda i: (i, 0))
    roww = pl.BlockSpec((tm, w), lambda i: (i, 0))
    rowl = pl.BlockSpec((tm, LANES), lambda i: (i, 0))
    vec = pl.BlockSpec((1, d), lambda i: (0, 0))
    return pl.pallas_call(
        functools.partial(_outproj_kernel, alpha=alpha),
        grid=(t // tm,),
        in_specs=[roww, roww, rowd,
                  pl.BlockSpec((1, 6, d), lambda i: (seq0 + i // tiles_per_seq, 0, 0)),
                  pl.BlockSpec((2 * w, d), lambda i: (0, 0)),
                  vec, vec,
                  pl.BlockSpec((d, LANES), lambda i: (0, 0)),
                  pl.BlockSpec((1, LANES), lambda i: (0, 0))],
        out_specs=[rowd, pl.BlockSpec((tm, d // 2), lambda i: (i, 0)), rowl, rowl,
                   pl.BlockSpec((8, LANES), lambda i: (0, 0))],
        out_shape=[jax.ShapeDtypeStruct((t, d), F32), jax.ShapeDtypeStruct((t, d // 2), jnp.int32),
                   jax.ShapeDtypeStruct((t, LANES), BF16), jax.ShapeDtypeStruct((t, LANES), F32),
                   jax.ShapeDtypeStruct((8, LANES), F32)],
        compiler_params=_cparams("arbitrary"),
        name="outproj_router",
    )(lru, att, x2, mod, wo_b, g.reshape(1, d), b.reshape(1, d), rw_pad, rb_pad)


def _rank_kernel(code_ref, cin_ref, ps_ref, dest_ref, cout_ref, carry_ref, low_ref):
    i = pl.program_id(0)
    tr = code_ref.shape[0]

    @pl.when(i == 0)
    def _():
        carry_ref[...] = cin_ref[0:1, :]
        r = lax.broadcasted_iota(jnp.int32, (tr, tr), 0)
        c = lax.broadcasted_iota(jnp.int32, (tr, tr), 1)
        low_ref[...] = (c < r).astype(BF16)

    code = code_ref[...].astype(F32)
    member = (code > 0.0).astype(BF16)
    before = jnp.dot(low_ref[...], member, preferred_element_type=F32) + carry_ref[...]
    base = before + ps_ref[...]
    lane = lax.broadcasted_iota(jnp.int32, code.shape, 1)
    dest = jnp.zeros(code.shape, F32)
    for kk in range(TOP_K):
        d = jnp.sum(jnp.where(code == float(kk + 1), base, 0.0), axis=-1, keepdims=True)
        dest = jnp.where(lane == kk, d, dest)
    dest_ref[...] = jnp.transpose(dest)[0:8, :].astype(jnp.int32)
    carry = before[tr - 1:tr, :] + (code[tr - 1:tr, :] > 0.0).astype(F32)
    carry_ref[...] = carry
    cout_ref[...] = jnp.broadcast_to(carry, cout_ref.shape)


def _rank(code, carry_in, pad_start, tr):
    t = code.shape[0]
    rowl = pl.BlockSpec((tr, LANES), lambda i: (i, 0))
    small = pl.BlockSpec((8, LANES), lambda i: (0, 0))
    return pl.pallas_call(
        _rank_kernel,
        grid=(t // tr,),
        in_specs=[rowl, small, pl.BlockSpec((1, LANES), lambda i: (0, 0))],
        out_specs=[pl.BlockSpec((8, tr), lambda i: (0, i)), small],
        out_shape=[jax.ShapeDtypeStruct((8, t), jnp.int32),
                   jax.ShapeDtypeStruct((8, LANES), F32)],
        scratch_shapes=[pltpu.VMEM((1, LANES), F32), pltpu.VMEM((tr, tr), BF16)],
        compiler_params=_cparams("arbitrary"),
        name="moe_rank",
    )(code, carry_in, pad_start)


def _dest_window(tm):
    width = max(tm, LANES)
    return width, width // tm


def _scatter_kernel(dest_ref, tok_ref, rows_in_ref, rows_ref, sem, *, tm, per):
    del rows_in_ref
    base = (pl.program_id(0) % per) * tm

    def row_copy(t, kk):
        return pltpu.make_async_copy(tok_ref.at[pl.ds(t, 1), :],
                                     rows_ref.at[pl.ds(dest_ref[kk, base + t], 1), :], sem)

    def start(g, c):
        t0 = pl.multiple_of(g * SUBLANES, SUBLANES)
        for s in range(SUBLANES):
            for kk in range(TOP_K):
                row_copy(t0 + s, kk).start()
        return c

    def wait(g, c):
        t0 = pl.multiple_of(g * SUBLANES, SUBLANES)
        for s in range(SUBLANES):
            for kk in range(TOP_K):
                row_copy(t0 + s, kk).wait()
        return c

    lax.fori_loop(0, tm // SUBLANES, start, 0)
    lax.fori_loop(0, tm // SUBLANES, wait, 0)


def _scatter(dest, tok, rows_buf, tm):
    t, d = tok.shape
    width, per = _dest_window(tm)
    return pl.pallas_call(
        functools.partial(_scatter_kernel, tm=tm, per=per),
        grid=(t // tm,),
        in_specs=[pl.BlockSpec((8, width), lambda i: (0, i // per), memory_space=pltpu.SMEM),
                  pl.BlockSpec((tm, d), lambda i: (i, 0)),
                  pl.BlockSpec(memory_space=pl.ANY)],
        out_specs=pl.BlockSpec(memory_space=pl.ANY),
        out_shape=jax.ShapeDtypeStruct(rows_buf.shape, rows_buf.dtype),
        scratch_shapes=[pltpu.SemaphoreType.DMA(())],
        input_output_aliases={2: 0},
        compiler_params=_cparams("arbitrary"),
        name="moe_scatter",
    )(dest, tok, rows_buf)


def _expert_kernel(be_ref, nu_ref, nv_ref, x_ref, wgu_ref, bgu_ref, wd_ref, bd_ref, y_ref, wgu_b, wd_b):
    dff = wd_ref.shape[1]
    i = pl.program_id(0)

    @pl.when(i < nu_ref[0])
    def _():
        @pl.when(jnp.logical_or(i == 0, be_ref[i] != be_ref[jnp.maximum(i - 1, 0)]))
        def _():
            wgu_b[...] = wgu_ref[0].astype(BF16)
            wd_b[...] = wd_ref[0].astype(BF16)

        row = lax.broadcasted_iota(jnp.int32, (x_ref.shape[0], 1), 0)
        xb = _unpack_bf16_pairs(jnp.where(row < nv_ref[i], x_ref[...], 0)).astype(BF16)
        gu = jnp.dot(xb, wgu_b[...], preferred_element_type=F32) + bgu_ref[0]
        gate = jnp.minimum(gu[:, :dff], SWIGLU_LIMIT)
        up = jnp.clip(gu[:, dff:], -SWIGLU_LIMIT, SWIGLU_LIMIT)
        act = (up + 1.0) * gate * jax.nn.sigmoid(SWIGLU_ALPHA * gate)
        y_ref[...] = _pack_bf16_pairs(
            jnp.dot(act.astype(BF16), wd_b[...], preferred_element_type=F32) + bd_ref[0])


def _experts(block_e, n_used, n_valid, x_rows, wgu, bgu, wd, bd, layer):
    rows, dp = x_rows.shape
    depth, ne, d, n_gu = wgu.shape
    dff = wd.shape[2]
    nb = rows // MOE_BLOCK

    def expert(i, be, nu):
        return be[jnp.minimum(i, nu[0] - 1)]

    rowspec = pl.BlockSpec((MOE_BLOCK, dp), lambda i, be, nu, nv: (jnp.minimum(i, nu[0] - 1), 0))
    return pl.pallas_call(
        _expert_kernel,
        grid_spec=pltpu.PrefetchScalarGridSpec(
            num_scalar_prefetch=3,
            grid=(nb,),
            in_specs=[rowspec,
                      pl.BlockSpec((None, 1, d, n_gu), lambda i, be, nu, nv: (layer, expert(i, be, nu), 0, 0)),
                      pl.BlockSpec((None, 1, 1, n_gu), lambda i, be, nu, nv: (layer, expert(i, be, nu), 0, 0)),
                      pl.BlockSpec((None, 1, dff, d), lambda i, be, nu, nv: (layer, expert(i, be, nu), 0, 0)),
                      pl.BlockSpec((None, 1, 1, d), lambda i, be, nu, nv: (layer, expert(i, be, nu), 0, 0))],
            out_specs=rowspec,
            scratch_shapes=[pltpu.VMEM((d, n_gu), BF16), pltpu.VMEM((dff, d), BF16)]),
        out_shape=jax.ShapeDtypeStruct((rows, dp), jnp.int32),
        compiler_params=_cparams("arbitrary"),
        name="moe_experts",
    )(block_e, n_used, n_valid, x_rows, wgu, bgu.reshape(depth, ne, 1, n_gu), wd,
      bd.reshape(depth, ne, 1, d))


def _sc_gather(rows, idx):
    n = idx.shape[0]
    d = rows.shape[1]
    mesh = plsc.VectorSubcoreMesh(core_axis_name="c", subcore_axis_name="s",
                                  num_cores=SC_CORES, num_subcores=SC_SUBCORES)
    n_workers = mesh.num_cores * mesh.num_subcores
    per_worker = n // n_workers
    assert per_worker * n_workers == n and per_worker % LANES == 0

    @functools.partial(pl.kernel, out_type=jax.ShapeDtypeStruct((n, d), rows.dtype), mesh=mesh,
                       scratch_types=[pltpu.VMEM((LANES,), jnp.int32),
                                      pltpu.VMEM((SC_WINDOW, d), rows.dtype),
                                      pltpu.VMEM((SC_WINDOW, d), rows.dtype),
                                      pltpu.SemaphoreType.DMA((2,)),
                                      pltpu.SemaphoreType.DMA((2,))],
                       compiler_params=pltpu.CompilerParams(use_tc_tiling_on_sc=True),
                       name="moe_gather")
    def gather(rows_hbm, idx_hbm, out_hbm, idx_vmem, buf0, buf1, gsem, wsem):
        worker = lax.axis_index("c") * mesh.num_subcores + lax.axis_index("s")
        bufs = (buf0, buf1)
        n_win = LANES // SC_WINDOW

        @pl.loop(0, per_worker // LANES)
        def _(j):
            chunk = worker * (per_worker // LANES) + j
            pltpu.sync_copy(idx_hbm.at[chunk], idx_vmem)

            def write(q):
                return pltpu.async_copy(
                    bufs[q % 2], out_hbm.at[pl.ds(chunk * LANES + q * SC_WINDOW, SC_WINDOW)],
                    wsem.at[q % 2])

            gathers, writes = [], []
            for q in range(n_win):
                if q >= 2:
                    writes[q - 2].wait()
                gathers.append(pltpu.async_copy(
                    rows_hbm.at[idx_vmem.at[pl.ds(q * SC_WINDOW, SC_WINDOW)]], bufs[q % 2],
                    gsem.at[q % 2]))
                if q >= 1:
                    gathers[q - 1].wait()
                    writes.append(write(q - 1))
            gathers[n_win - 1].wait()
            writes.append(write(n_win - 1))
            writes[n_win - 2].wait()
            writes[n_win - 1].wait()

    return gather(rows, idx.reshape(n // LANES, LANES))


def _sc_scatter(tok, dest, n_rows):
    t, d = tok.shape
    mesh = plsc.VectorSubcoreMesh(core_axis_name="c", subcore_axis_name="s",
                                  num_cores=SC_CORES, num_subcores=SC_SUBCORES)
    n_workers = mesh.num_cores * mesh.num_subcores
    per_worker = t // n_workers
    assert per_worker * n_workers == t and per_worker % LANES == 0
    chunks = t // LANES
    n_win = LANES // SC_WINDOW
    lanes16 = 16

    @functools.partial(pl.kernel, out_type=jax.ShapeDtypeStruct((n_rows, d), tok.dtype), mesh=mesh,
                       scratch_types=[pltpu.VMEM((TOP_K, LANES), jnp.int32),
                                      pltpu.VMEM((SC_WINDOW, d), tok.dtype),
                                      pltpu.VMEM((SC_WINDOW, d), tok.dtype),
                                      pltpu.SemaphoreType.DMA((2,)),
                                      pltpu.SemaphoreType.DMA((2,))],
                       compiler_params=pltpu.CompilerParams(use_tc_tiling_on_sc=True),
                       name="moe_scatter_sc")
    def scatter(tok_hbm, idx_hbm, rows_hbm, idx_vmem, buf0, buf1, lsem, ssem):
        worker = lax.axis_index("c") * mesh.num_subcores + lax.axis_index("s")
        bufs = (buf0, buf1)

        @pl.loop(0, per_worker // LANES)
        def _(j):
            chunk = worker * (per_worker // LANES) + j
            for kk in range(TOP_K):
                pltpu.sync_copy(idx_hbm.at[kk * chunks + chunk], idx_vmem.at[kk])

            def load(q):
                return pltpu.async_copy(
                    tok_hbm.at[pl.ds(chunk * LANES + q * SC_WINDOW, SC_WINDOW)], bufs[q % 2],
                    lsem.at[q % 2])

            def store_all(q):
                copies = []
                for kk in range(TOP_K):
                    for h in range(SC_WINDOW // lanes16):
                        rows16 = idx_vmem[kk, pl.ds(q * SC_WINDOW + h * lanes16, lanes16)]
                        copies.append(pltpu.async_copy(
                            bufs[q % 2].at[pl.ds(h * lanes16, lanes16)], rows_hbm.at[rows16],
                            ssem.at[q % 2]))
                return copies

            loads = [load(0)]
            stores = []
            for q in range(n_win):
                if q + 1 < n_win:
                    if q >= 1:
                        for cp in stores[q - 1]:
                            cp.wait()
                    loads.append(load(q + 1))
                loads[q].wait()
                stores.append(store_all(q))
            for q in (n_win - 2, n_win - 1):
                for cp in stores[q]:
                    cp.wait()

    return scatter(tok, dest[:TOP_K].reshape(TOP_K * chunks, LANES))


def _combine_dense_kernel(y0_ref, y1_ref, y2_ref, y3_ref, gate_ref, x_ref, mod_ref, g_ref, b_ref,
                          o_ref, *, alpha):
    gates = gate_ref[...]
    f = _unpack_bf16_pairs(y0_ref[...]) * gates[:, 0:1]
    for kk, y_ref in enumerate((y1_ref, y2_ref, y3_ref), start=1):
        f = f + _unpack_bf16_pairs(y_ref[...]) * gates[:, kk:kk + 1]
    gate2 = mod_ref[0, 5:6, :]
    o_ref[...] = _layer_norm(alpha * x_ref[...] + gate2 * f, g_ref[...], b_ref[...])


def _combine_dense(y_slots, gates, x1, mod, g, b, seq0, seq_len, tm, alpha):
    t, d = x1.shape
    nt = t // tm
    tiles_per_seq = seq_len // tm
    rowd = pl.BlockSpec((tm, d), lambda i: (i, 0))
    vec = pl.BlockSpec((1, d), lambda i: (0, 0))
    slot_specs = [pl.BlockSpec((tm, y_slots.shape[1]),
                               functools.partial(lambda i, kk: (kk * nt + i, 0), kk=kk))
                  for kk in range(TOP_K)]
    return pl.pallas_call(
        functools.partial(_combine_dense_kernel, alpha=alpha),
        grid=(nt,),
        in_specs=slot_specs + [pl.BlockSpec((tm, LANES), lambda i: (i, 0)), rowd,
                               pl.BlockSpec((1, 6, d), lambda i: (seq0 + i // tiles_per_seq, 0, 0)),
                               vec, vec],
        out_specs=rowd,
        out_shape=jax.ShapeDtypeStruct((t, d), F32),
        compiler_params=_cparams("parallel"),
        name="moe_combine",
    )(y_slots, y_slots, y_slots, y_slots, gates, x1, mod, g.reshape(1, d), b.reshape(1, d))


def _combine_kernel(dest_ref, y_hbm, gate_ref, x_ref, mod_ref, g_ref, b_ref, o_ref, ybuf, sem,
                    *, alpha, per):
    tm = x_ref.shape[0]
    base = (pl.program_id(0) % per) * tm

    def row_copy(t, kk):
        return pltpu.make_async_copy(y_hbm.at[pl.ds(dest_ref[kk, base + t], 1), :],
                                     ybuf.at[kk, pl.ds(t, 1), :], sem)

    def start(g, c):
        t0 = pl.multiple_of(g * SUBLANES, SUBLANES)
        for s in range(SUBLANES):
            for kk in range(TOP_K):
                row_copy(t0 + s, kk).start()
        return c

    def wait(g, c):
        t0 = pl.multiple_of(g * SUBLANES, SUBLANES)
        for s in range(SUBLANES):
            for kk in range(TOP_K):
                row_copy(t0 + s, kk).wait()
        return c

    lax.fori_loop(0, tm // SUBLANES, start, 0)
    lax.fori_loop(0, tm // SUBLANES, wait, 0)

    gates = gate_ref[...]
    f = _unpack_bf16_pairs(ybuf[0]) * gates[:, 0:1]
    for kk in range(1, TOP_K):
        f = f + _unpack_bf16_pairs(ybuf[kk]) * gates[:, kk:kk + 1]
    gate2 = mod_ref[0, 5:6, :]
    o_ref[...] = _layer_norm(alpha * x_ref[...] + gate2 * f, g_ref[...], b_ref[...])


def _combine(dest, y_rows, gates, x1, mod, g, b, seq0, seq_len, tm, alpha):
    t, d = x1.shape
    tiles_per_seq = seq_len // tm
    width, per = _dest_window(tm)
    rowd = pl.BlockSpec((tm, d), lambda i: (i, 0))
    vec = pl.BlockSpec((1, d), lambda i: (0, 0))
    return pl.pallas_call(
        functools.partial(_combine_kernel, alpha=alpha, per=per),
        grid=(t // tm,),
        in_specs=[pl.BlockSpec((8, width), lambda i: (0, i // per), memory_space=pltpu.SMEM),
                  pl.BlockSpec(memory_space=pl.ANY),
                  pl.BlockSpec((tm, LANES), lambda i: (i, 0)),
                  rowd,
                  pl.BlockSpec((1, 6, d), lambda i: (seq0 + i // tiles_per_seq, 0, 0)),
                  vec, vec],
        out_specs=rowd,
        out_shape=jax.ShapeDtypeStruct((t, d), F32),
        scratch_shapes=[pltpu.VMEM((TOP_K, tm, y_rows.shape[1]), y_rows.dtype),
                        pltpu.SemaphoreType.DMA(())],
        compiler_params=_cparams("arbitrary"),
        name="moe_combine",
    )(dest, y_rows, gates, x1, mod, g.reshape(1, d), b.reshape(1, d))


def _block_diag(wh):
    heads, hd, _ = wh.shape
    eye = jnp.eye(heads, dtype=wh.dtype)
    return jnp.einsum('hij,hg->higj', wh, eye).reshape(heads * hd, heads * hd)


def kernel(x_prompt, x_sample, cache_k, cache_v, state_conv, state_lru, c_prompt, c_sample, ln_in_g, ln_in_b, w_ada, b_ada, w_in, conv_w, conv_b, gate_a_w, gate_a_b, gate_x_w, gate_x_b, lru_lambda, w_out, ln1_g, ln1_b, router_w, router_b, w_gu, b_gu, w_down, b_down, ln2_g, ln2_b):
    nb_p, s_p, d = x_prompt.shape
    nb_s, s_s, _ = x_sample.shape
    depth = w_in.shape[0]
    past = cache_k.shape[2]
    w = state_lru.shape[-1]
    hw = SB_HEADS * SB_HEAD_DIM
    t_p, t_s = nb_p * s_p, nb_s * s_s
    alpha = float((2 * depth) ** 0.25)

    tm_p, tm_s = min(512, s_p), min(512, s_s)
    ts_p, ts_s = min(256, s_p), min(256, s_s)
    tq_p, tq_s = min(SB_TQ, s_p), min(SB_TQ, s_s)
    tg_p, tg_s = min(256, s_p), min(256, s_s)
    tr_p, tr_s = min(512, t_p), min(512, t_s)

    n_seq = nb_p + nb_s
    c_all = jnp.concatenate([c_prompt, c_sample, jnp.zeros((-n_seq % 8, d), F32)], axis=0)
    mod_all = _adaln(c_all, w_ada, b_ada).reshape(depth, c_all.shape[0], 6, d)

    xp = _ln_in(x_prompt.reshape(t_p, d), ln_in_g, ln_in_b, tm_p)
    xs = _ln_in(x_sample.reshape(t_s, d), ln_in_g, ln_in_b, tm_s)

    upper = jnp.asarray(np.arange(SB_SUB)[:, None] > np.arange(SB_SUB)[None, :], BF16)
    kpad_s = -(-(past + s_s) // SB_TK) * SB_TK
    zero_conv = jnp.zeros((nb_p, CONV_WIDTH - 1, w), F32)
    zero_h = jnp.zeros((nb_p, w), F32)

    tk_total = (t_p + t_s) * TOP_K
    n_blocks = -(-(tk_total + N_EXPERTS * (MOE_BLOCK - 1)) // MOE_BLOCK)
    n_rows = n_blocks * MOE_BLOCK

    outs = [[] for _ in range(8)]
    kv_p = kv_s = None
    for l in range(depth):
        mod = mod_all[l]
        w_in_b = w_in[l].astype(BF16)
        wo_b = w_out[l].astype(BF16)
        wa = _block_diag(gate_a_w[l]).astype(BF16)
        wx = _block_diag(gate_x_w[l]).astype(BF16)
        rw_pad = jnp.pad(router_w[l], ((0, 0), (0, LANES - N_EXPERTS)))
        rb_pad = jnp.pad(router_b[l], (0, LANES - N_EXPERTS), constant_values=NEG_BIG).reshape(1, LANES)
        lru_w = (conv_w[l], conv_b[l], wa, gate_a_b[l], wx, gate_x_b[l], lru_lambda[l])

        xr, gy, q, k, v, kb, vb = _inproj(xp, mod, w_in_b, 0, s_p, tm_p, l, depth, kv_p)
        kv_p = (k, v)
        lru_p, nconv_p, nh_p = _rglru(xr, gy, zero_conv, zero_h, *lru_w, nb_p, s_p, ts_p)
        att_p = _stick_breaking(q, kb.reshape(nb_p, s_p, hw), vb.reshape(nb_p, s_p, hw), upper,
                                nb_p, s_p, tq_p, 0)
        outs[2].append(nconv_p)
        outs[3].append(nh_p.reshape(nb_p, w))

        xr, gy, q, k, v, kb, vb = _inproj(xs, mod, w_in_b, nb_p, s_s, tm_s, l, depth, kv_s)
        kv_s = (k, v)
        lru_s, nconv_s, nh_s = _rglru(xr, gy, state_conv[l], state_lru[l], *lru_w, nb_s, s_s, ts_s)
        kv_pad = jnp.zeros((nb_s, kpad_s - past - s_s, hw), BF16)
        k_all = jnp.concatenate([cache_k[l].reshape(nb_s, past, hw).astype(BF16),
                                 kb.reshape(nb_s, s_s, hw), kv_pad], axis=1)
        v_all = jnp.concatenate([cache_v[l].reshape(nb_s, past, hw).astype(BF16),
                                 vb.reshape(nb_s, s_s, hw), kv_pad], axis=1)
        att_s = _stick_breaking(q, k_all, v_all, upper, nb_s, s_s, tq_s, past)
        outs[6].append(nconv_s)
        outs[7].append(nh_s.reshape(nb_s, w))

        x1_p, tok_p, code_p, gate_p, cnt_p = _outproj(lru_p, att_p, xp, mod, wo_b, ln1_g[l], ln1_b[l],
                                                      rw_pad, rb_pad, 0, s_p, tm_p, alpha)
        x1_s, tok_s, code_s, gate_s, cnt_s = _outproj(lru_s, att_s, xs, mod, wo_b, ln1_g[l], ln1_b[l],
                                                      rw_pad, rb_pad, nb_p, s_s, tm_s, alpha)

        counts = (cnt_p[0, :N_EXPERTS] + cnt_s[0, :N_EXPERTS]).astype(jnp.int32)
        padded = (counts + MOE_BLOCK - 1) // MOE_BLOCK * MOE_BLOCK
        pad_end = jnp.cumsum(padded)
        pad_start = jnp.pad((pad_end - padded).astype(F32), (0, LANES - N_EXPERTS)).reshape(1, LANES)
        block_row0 = jnp.arange(n_blocks, dtype=jnp.int32) * MOE_BLOCK
        block_e = jnp.minimum(jnp.sum(pad_end[None, :] <= block_row0[:, None], axis=1),
                              N_EXPERTS - 1).astype(jnp.int32)

        n_used = (pad_end[N_EXPERTS - 1:] // MOE_BLOCK).astype(jnp.int32)
        n_valid = jnp.clip(counts[block_e] - (block_row0 - (pad_end - padded)[block_e]), 0, MOE_BLOCK)

        dest_p, carry = _rank(code_p, jnp.zeros((8, LANES), F32), pad_start, tr_p)
        dest_s, _ = _rank(code_s, carry, pad_start, tr_s)

        x_rows = _sc_scatter(tok_p, dest_p, n_rows)
        x_rows = _scatter(dest_s, tok_s, x_rows, tg_s)
        y_rows = _experts(block_e, n_used, n_valid, x_rows, w_gu, b_gu, w_down, b_down, l)

        yg_p = _sc_gather(y_rows, dest_p[:TOP_K].reshape(TOP_K * t_p))
        xp = _combine_dense(yg_p, gate_p, x1_p, mod, ln2_g[l], ln2_b[l], 0, s_p, tg_p, alpha)
        xs = _combine(dest_s, y_rows, gate_s, x1_s, mod, ln2_g[l], ln2_b[l], nb_p, s_s, tg_s, alpha)

    return (xp.reshape(nb_p, s_p, d), xs.reshape(nb_s, s_s, d),
            kv_p[0].reshape(depth, nb_p, s_p, SB_HEADS, SB_HEAD_DIM),
            kv_p[1].reshape(depth, nb_p, s_p, SB_HEADS, SB_HEAD_DIM),
            jnp.stack(outs[2]), jnp.stack(outs[3]),
            kv_s[0].reshape(depth, nb_s, s_s, SB_HEADS, SB_HEAD_DIM),
            kv_s[1].reshape(depth, nb_s, s_s, SB_HEADS, SB_HEAD_DIM),
            jnp.stack(outs[6]), jnp.stack(outs[7]))
```

```python
import functools

import jax
import jax.numpy as jnp
import numpy as np
from jax import lax
from jax.experimental import pallas as pl
from jax.experimental.pallas import tpu as pltpu
from jax.experimental.pallas import tpu_sc as plsc

F32 = jnp.float32
BF16 = jnp.bfloat16

LANES = 128
SUBLANES = 8
LRU_HEADS = 8
CONV_WIDTH = 4
RG_C = 8.0
SB_HEADS = 4
SB_HEAD_DIM = 128
SB_SCALE = SB_HEAD_DIM ** -0.5
SB_SUB = 256
SB_TQ = 512
SB_TK = 512
SB_DEAD_LOG2 = 160.0
LOG2E = 1.4426950408889634
N_EXPERTS = 32
TOP_K = 4
SWIGLU_LIMIT = 7.0
SWIGLU_ALPHA = 1.702
MOE_BLOCK = 512
SC_CORES, SC_SUBCORES = 2, 16
SC_WINDOW = 32
LN_EPS = 1e-5
NEG_BIG = -1e30
VMEM_LIMIT = 56 * 1024 * 1024


def _cparams(*sem):
    return pltpu.CompilerParams(dimension_semantics=sem, vmem_limit_bytes=VMEM_LIMIT)


def _split3(a):
    hi = a.astype(BF16)
    r1 = a - hi.astype(F32)
    mid = r1.astype(BF16)
    lo = (r1 - mid.astype(F32)).astype(BF16)
    return hi, mid, lo


def _dot_f32(a, b):
    a0, a1, a2 = _split3(a)
    b0, b1, b2 = _split3(b)
    d = functools.partial(jnp.dot, preferred_element_type=F32)
    return (d(a0, b0) + (d(a0, b1) + d(a1, b0))
            + (d(a0, b2) + d(a1, b1) + d(a2, b0)))


def _layer_norm(y, g, b):
    mu = jnp.mean(y, axis=-1, keepdims=True)
    d = y - mu
    var = jnp.mean(d * d, axis=-1, keepdims=True)
    return d * lax.rsqrt(var + LN_EPS) * g + b


def _softplus(z):
    return jnp.maximum(z, 0.0) + jnp.log(1.0 + jnp.exp(-jnp.abs(z)))


def _pack_bf16_pairs(x):
    h = x.shape[1] // 2
    bits = lax.bitcast_convert_type(x.astype(BF16).astype(F32), jnp.int32)
    return lax.shift_right_logical(bits[:, :h], 16) | (bits[:, h:] & jnp.int32(-65536))


def _unpack_bf16_pairs(p):
    lo = lax.bitcast_convert_type(p << 16, F32)
    hi = lax.bitcast_convert_type(p & jnp.int32(-65536), F32)
    return jnp.concatenate([lo, hi], axis=1)


def _gelu_tanh(x):
    c = np.sqrt(2.0 / np.pi).astype(np.float32)
    return 0.5 * x * (1.0 + jnp.tanh(c * (x + 0.044715 * (x * x * x))))


def _adaln_kernel(c_ref, w_ref, b_ref, o_ref):
    c = c_ref[...]
    s = c * jax.nn.sigmoid(c)
    o_ref[0] = _dot_f32(s, w_ref[0]) + b_ref[0]


def _adaln(c_all, w_ada, b_ada):
    depth, d, n = w_ada.shape
    rows = c_all.shape[0]
    tn = 1024
    return pl.pallas_call(
        _adaln_kernel,
        grid=(depth, n // tn),
        in_specs=[pl.BlockSpec((rows, d), lambda l, j: (0, 0)),
                  pl.BlockSpec((1, d, tn), lambda l, j: (l, 0, j)),
                  pl.BlockSpec((1, 1, tn), lambda l, j: (l, 0, j))],
        out_specs=pl.BlockSpec((1, rows, tn), lambda l, j: (l, 0, j)),
        out_shape=jax.ShapeDtypeStruct((depth, rows, n), F32),
        compiler_params=_cparams("parallel", "parallel"),
        name="adaln",
    )(c_all, w_ada, b_ada.reshape(depth, 1, n))


def _ln_kernel(x_ref, g_ref, b_ref, o_ref):
    o_ref[...] = _layer_norm(x_ref[...], g_ref[...], b_ref[...])


def _ln_in(x2, g, b, tm):
    t, d = x2.shape
    return pl.pallas_call(
        _ln_kernel,
        grid=(t // tm,),
        in_specs=[pl.BlockSpec((tm, d), lambda i: (i, 0)),
                  pl.BlockSpec((1, d), lambda i: (0, 0)),
                  pl.BlockSpec((1, d), lambda i: (0, 0))],
        out_specs=pl.BlockSpec((tm, d), lambda i: (i, 0)),
        out_shape=jax.ShapeDtypeStruct((t, d), F32),
        compiler_params=_cparams("parallel"),
        name="ln_in",
    )(x2, g.reshape(1, d), b.reshape(1, d))


def _inproj_kernel(x_ref, mod_ref, w_ref, *rest):
    xr_ref, gy_ref, q_ref, k_ref, v_ref, kb_ref, vb_ref = rest[-7:]
    w = xr_ref.shape[-1]

    def store_heads(ref, val):
        for hh in range(SB_HEADS):
            ref[:, hh, :] = val[:, hh * SB_HEAD_DIM:(hh + 1) * SB_HEAD_DIM]

    shift = mod_ref[0, 0:1, :]
    scale = mod_ref[0, 1:2, :]
    h = (x_ref[...] * (1.0 + scale) + shift).astype(BF16)

    def proj(j):
        return jnp.dot(h, w_ref[:, j * w:(j + 1) * w], preferred_element_type=F32)

    xr_ref[...] = proj(0)
    gy_ref[...] = _gelu_tanh(proj(1))
    q_ref[...] = (proj(2) * (SB_SCALE * LOG2E)).astype(BF16)
    k = proj(3)
    store_heads(k_ref, k)
    kb_ref[...] = k.astype(BF16)
    v = proj(4)
    store_heads(v_ref, v)
    vb_ref[...] = v.astype(BF16)


def _inproj(x2, mod, w_in_b, seq0, seq_len, tm, layer, depth, kv_stacks):
    t, d = x2.shape
    w = w_in_b.shape[1] // 5
    nt = t // tm
    tiles_per_seq = seq_len // tm
    row = pl.BlockSpec((tm, w), lambda i: (i, 0))
    heads = pl.BlockSpec((tm, SB_HEADS, SB_HEAD_DIM), lambda i: (layer * nt + i, 0, 0))
    f32o = jax.ShapeDtypeStruct((t, w), F32)
    bf16o = jax.ShapeDtypeStruct((t, w), BF16)
    stack = jax.ShapeDtypeStruct((depth * t, SB_HEADS, SB_HEAD_DIM), F32)
    in_specs = [pl.BlockSpec((tm, d), lambda i: (i, 0)),
                pl.BlockSpec((1, 6, d), lambda i: (seq0 + i // tiles_per_seq, 0, 0)),
                pl.BlockSpec((d, 5 * w), lambda i: (0, 0))]
    args = [x2, mod, w_in_b]
    aliases = {}
    if kv_stacks is not None:
        in_specs += [pl.BlockSpec(memory_space=pl.ANY)] * 2
        args += list(kv_stacks)
        aliases = {3: 3, 4: 4}
    return pl.pallas_call(
        _inproj_kernel,
        grid=(nt,),
        in_specs=in_specs,
        out_specs=[row, row, row, heads, heads, row, row],
        out_shape=[f32o, f32o, bf16o, stack, stack, bf16o, bf16o],
        input_output_aliases=aliases,
        compiler_params=_cparams("parallel"),
        name="inproj",
    )(*args)


def _rglru_kernel(xr_ref, gy_ref, cs_ref, h0_ref, cw_ref, cb_ref, wa_ref, ba_ref, wx_ref, bx_ref,
                  lam_ref, out_ref, nconv_ref, nh_ref, tail_ref, h_ref):
    ts, w = xr_ref.shape
    j = pl.program_id(1)

    @pl.when(j == 0)
    def _():
        tail_ref[...] = jnp.zeros((SUBLANES, w), F32)
        tail_ref[SUBLANES - (CONV_WIDTH - 1):SUBLANES, :] = cs_ref[0]
        h_ref[...] = h0_ref[0]

    xr = xr_ref[...]
    prev = tail_ref[...]
    row8 = lax.broadcasted_iota(jnp.int32, (SUBLANES, w), 0)
    xc = cb_ref[...] + xr * cw_ref[CONV_WIDTH - 1:CONV_WIDTH, :]
    for dly in range(1, CONV_WIDTH):
        sh = pltpu.roll(xr, dly, 0)
        head = jnp.where(row8 < dly, pltpu.roll(prev, dly, 0), sh[0:SUBLANES, :])
        sh = jnp.concatenate([head, sh[SUBLANES:, :]], axis=0)
        xc = xc + sh * cw_ref[CONV_WIDTH - 1 - dly:CONV_WIDTH - dly, :]
    nconv_ref[0] = xr[ts - (CONV_WIDTH - 1):ts, :]
    tail_ref[...] = xr[ts - SUBLANES:ts, :]

    xcb = xc.astype(BF16)
    r = jax.nn.sigmoid(jnp.dot(xcb, wa_ref[...], preferred_element_type=F32) + ba_ref[...])
    gi = jax.nn.sigmoid(jnp.dot(xcb, wx_ref[...], preferred_element_type=F32) + bx_ref[...])
    log_a = (RG_C * r) * (-_softplus(-lam_ref[...]))
    a = jnp.exp(log_a)
    one_m = 1.0 - a * a
    b = jnp.where(one_m > 0.0, one_m * lax.rsqrt(one_m), 0.0) * (gi * xc)

    sub = lax.broadcasted_iota(jnp.int32, (ts, w), 0) & (SUBLANES - 1)
    dist = 1
    while dist < SUBLANES:
        keep = sub >= dist
        a_prev = jnp.where(keep, pltpu.roll(a, dist, 0), 1.0)
        b_prev = jnp.where(keep, pltpu.roll(b, dist, 0), 0.0)
        b = a * b_prev + b
        a = a * a_prev
        dist *= 2
    carry = h_ref[...]
    groups = []
    for g0 in range(0, ts, SUBLANES):
        hg = b[g0:g0 + SUBLANES, :] + a[g0:g0 + SUBLANES, :] * carry
        groups.append(hg)
        carry = hg[SUBLANES - 1:SUBLANES, :]
    hs = jnp.concatenate(groups, axis=0)
    h_last = carry
    h_ref[...] = h_last
    nh_ref[0] = h_last
    out_ref[...] = (hs * gy_ref[...]).astype(BF16)


def _rglru(xr, gy, conv_state, h0, cw, cb, wa, ba, wx, bx, lam, nseq, seq_len, ts):
    t, w = xr.shape
    n = seq_len // ts
    row = pl.BlockSpec((ts, w), lambda b, j: (b * n + j, 0))
    vec = pl.BlockSpec((1, w), lambda b, j: (0, 0))
    return pl.pallas_call(
        _rglru_kernel,
        grid=(nseq, n),
        in_specs=[row, row,
                  pl.BlockSpec((1, CONV_WIDTH - 1, w), lambda b, j: (b, 0, 0)),
                  pl.BlockSpec((1, 1, w), lambda b, j: (b, 0, 0)),
                  pl.BlockSpec((CONV_WIDTH, w), lambda b, j: (0, 0)),
                  vec,
                  pl.BlockSpec((w, w), lambda b, j: (0, 0)), vec,
                  pl.BlockSpec((w, w), lambda b, j: (0, 0)), vec,
                  vec],
        out_specs=[row,
                   pl.BlockSpec((1, CONV_WIDTH - 1, w), lambda b, j: (b, 0, 0)),
                   pl.BlockSpec((1, 1, w), lambda b, j: (b, 0, 0))],
        out_shape=[jax.ShapeDtypeStruct((t, w), BF16),
                   jax.ShapeDtypeStruct((nseq, CONV_WIDTH - 1, w), F32),
                   jax.ShapeDtypeStruct((nseq, 1, w), F32)],
        scratch_shapes=[pltpu.VMEM((SUBLANES, w), F32), pltpu.VMEM((1, w), F32)],
        compiler_params=_cparams("parallel", "arbitrary"),
        name="rglru",
    )(xr, gy, conv_state, h0.reshape(nseq, 1, w), cw, cb.reshape(1, w), wa, ba.reshape(1, w),
      wx, bx.reshape(1, w), lam.reshape(1, w))


def _sb_scores(q, k):
    return lax.dot_general(q, k, (((1,), (1,)), ((), ())), preferred_element_type=F32)


def _sb_block(z, v, upper, off, acc, mask):
    neg_abs = lax.bitcast_convert_type(
        lax.bitcast_convert_type(z, jnp.int32) | jnp.int32(-2 ** 31), F32)
    sp = jnp.maximum(z, 0.0) + jnp.log2(1.0 + jnp.exp2(neg_abs))
    if mask is not None:
        sp = jnp.where(mask, sp, 0.0)
    spb = sp.astype(BF16)
    later = []
    for lo in range(SB_TK - SB_SUB, -1, -SB_SUB):
        later.append(jnp.dot(spb[:, lo:lo + SB_SUB], upper, preferred_element_type=F32) + off)
        off = off + jnp.sum(sp[:, lo:lo + SB_SUB], axis=-1, keepdims=True)
    wgt = jnp.exp2((z - sp) - jnp.concatenate(later[::-1], axis=1))
    if mask is not None:
        wgt = jnp.where(mask, wgt, 0.0)
    acc = acc + jnp.dot(wgt.astype(BF16), v, preferred_element_type=F32)
    return off, acc


def _sb_kernel(q_ref, k_ref, v_ref, up_ref, o_ref, z_ref, *, tq, q_pos0):
    i = pl.program_id(2)
    tk = SB_TK
    q = q_ref[...]
    upper = up_ref[...]
    n_full = (q_pos0 + i * tq) // tk
    off = jnp.zeros((tq, 1), F32)
    acc = jnp.zeros((tq, SB_HEAD_DIM), F32)

    def block_start(j):
        return pl.multiple_of(j * tk, tk)

    z_ref[...] = _sb_scores(q, k_ref[pl.ds(block_start(jnp.maximum(n_full - 1, 0)), tk), :])

    start = block_start(n_full)
    q_pos = q_pos0 + i * tq + lax.broadcasted_iota(jnp.int32, (tq, tk), 0)
    mask = (start + lax.broadcasted_iota(jnp.int32, (tq, tk), 1)) < q_pos
    off, acc = _sb_block(_sb_scores(q, k_ref[pl.ds(start, tk), :]), v_ref[pl.ds(start, tk), :],
                         upper, off, acc, mask)

    def live(off):
        return (jnp.min(off, axis=0, keepdims=True)[0, 0] < SB_DEAD_LOG2).astype(jnp.int32)

    def cond(carry):
        t, alive, _, _ = carry
        return jnp.logical_and(t < n_full, alive > 0)

    def body(carry):
        t, _, off, acc = carry
        j = n_full - 1 - t
        z = z_ref[...]
        z_ref[...] = _sb_scores(q, k_ref[pl.ds(block_start(jnp.maximum(j - 1, 0)), tk), :])
        off, acc = _sb_block(z, v_ref[pl.ds(block_start(j), tk), :], upper, off, acc, None)
        return t + 1, live(off), off, acc

    _, _, off, acc = lax.while_loop(cond, body, (jnp.int32(0), live(off), off, acc))
    o_ref[...] = acc.astype(o_ref.dtype)


def _stick_breaking(q, k_all, v_all, upper, nseq, seq_len, tq, q_pos0):
    t, hw = q.shape
    kpad = k_all.shape[1]
    nq = seq_len // tq
    half = SB_SUB
    assert SB_TK % tq == 0 and q_pos0 % tq == 0 and kpad % SB_TK == 0
    assert kpad >= ((q_pos0 + seq_len - 1) // SB_TK + 1) * SB_TK
    kern = functools.partial(_sb_kernel, tq=tq, q_pos0=q_pos0)
    qspec = pl.BlockSpec((tq, SB_HEAD_DIM), lambda b, h, i: (b * nq + i, h))
    kvspec = pl.BlockSpec((None, kpad, SB_HEAD_DIM), lambda b, h, i: (b, 0, h))
    return pl.pallas_call(
        kern,
        grid=(nseq, SB_HEADS, nq),
        in_specs=[qspec, kvspec, kvspec, pl.BlockSpec((half, half), lambda b, h, i: (0, 0))],
        out_specs=qspec,
        out_shape=jax.ShapeDtypeStruct((t, hw), BF16),
        scratch_shapes=[pltpu.VMEM((tq, SB_TK), F32)],
        compiler_params=_cparams("parallel", "parallel", "arbitrary"),
        name="stick_breaking",
    )(q, k_all, v_all, upper)


def _outproj_kernel(lru_ref, att_ref, x_ref, mod_ref, wo_ref, g_ref, b_ref, rwh_ref, rwl_ref, rb_ref,
                    x1_ref, tok_ref, code_ref, gate_ref, cnt_ref, *, alpha, n_chains):
    i = pl.program_id(0)
    tm, w = lru_ref.shape
    out = (jnp.dot(lru_ref[...], wo_ref[0:w, :], preferred_element_type=F32)
           + jnp.dot(att_ref[...], wo_ref[w:2 * w, :], preferred_element_type=F32))
    gate1 = mod_ref[0, 2:3, :]

    @pl.when(i == 0)
    def _():
        cnt_ref[...] = jnp.zeros(cnt_ref.shape, F32)

    rc = tm // n_chains
    for c in range(n_chains):
        rows = slice(c * rc, (c + 1) * rc)
        x1 = _layer_norm(alpha * x_ref[rows, :] + gate1 * out[rows, :], g_ref[...], b_ref[...])
        x1_ref[rows, :] = x1
        tok = x1 * (1.0 + mod_ref[0, 4:5, :]) + mod_ref[0, 3:4, :]
        tok_ref[rows, :] = _pack_bf16_pairs(tok)

        t_hi = tok.astype(BF16)
        t_lo = (tok - t_hi.astype(F32)).astype(BF16)
        logits = (jnp.dot(t_hi, rwh_ref[...], preferred_element_type=F32)
                  + (jnp.dot(t_hi, rwl_ref[...], preferred_element_type=F32)
                     + jnp.dot(t_lo, rwh_ref[...], preferred_element_type=F32))
                  + rb_ref[...])
        lane = lax.broadcasted_iota(jnp.int32, logits.shape, 1).astype(F32)
        code = jnp.zeros(logits.shape, F32)
        gates = jnp.zeros(logits.shape, F32)
        denom = jnp.zeros((rc, 1), F32)
        top0 = None
        for kk in range(TOP_K):
            m = jnp.max(logits, axis=-1, keepdims=True)
            idx = jnp.min(jnp.where(logits == m, lane, float(LANES)), axis=-1, keepdims=True)
            hit = lane == idx
            if kk == 0:
                top0 = m
            e = jnp.exp(m - top0)
            denom = denom + e
            code = jnp.where(hit, float(kk + 1), code)
            gates = jnp.where(lane == kk, e, gates)
            logits = jnp.where(hit, -jnp.inf, logits)
        code_ref[rows, :] = code.astype(BF16)
        gate_ref[rows, :] = gates / denom
        cnt_ref[...] += jnp.sum((code > 0.0).astype(F32), axis=0, keepdims=True)


def _outproj(lru, att, x2, mod, wo_b, g, b, rw_hi, rw_lo, rb_pad, seq0, seq_len, tm, alpha):
    t, d = x2.shape
    w = lru.shape[1]
    tiles_per_seq = seq_len // tm
    n_chains = 2 if tm % 32 == 0 else 1
    rowd = pl.BlockSpec((tm, d), lambda i: (i, 0))
    roww = pl.BlockSpec((tm, w), lambda i: (i, 0))
    rowl = pl.BlockSpec((tm, LANES), lambda i: (i, 0))
    vec = pl.BlockSpec((1, d), lambda i: (0, 0))
    return pl.pallas_call(
        functools.partial(_outproj_kernel, alpha=alpha, n_chains=n_chains),
        grid=(t // tm,),
        in_specs=[roww, roww, rowd,
                  pl.BlockSpec((1, 6, d), lambda i: (seq0 + i // tiles_per_seq, 0, 0)),
                  pl.BlockSpec((2 * w, d), lambda i: (0, 0)),
                  vec, vec,
                  pl.BlockSpec((d, LANES), lambda i: (0, 0)),
                  pl.BlockSpec((d, LANES), lambda i: (0, 0)),
                  pl.BlockSpec((1, LANES), lambda i: (0, 0))],
        out_specs=[rowd, pl.BlockSpec((tm, d // 2), lambda i: (i, 0)), rowl, rowl,
                   pl.BlockSpec((8, LANES), lambda i: (0, 0))],
        out_shape=[jax.ShapeDtypeStruct((t, d), F32), jax.ShapeDtypeStruct((t, d // 2), jnp.int32),
                   jax.ShapeDtypeStruct((t, LANES), BF16), jax.ShapeDtypeStruct((t, LANES), F32),
                   jax.ShapeDtypeStruct((8, LANES), F32)],
        compiler_params=_cparams("arbitrary"),
        name="outproj_router",
    )(lru, att, x2, mod, wo_b, g.reshape(1, d), b.reshape(1, d), rw_hi, rw_lo, rb_pad)


def _rank_kernel(code_ref, cin_ref, ps_ref, dest_ref, cout_ref, carry_ref, low_ref):
    i = pl.program_id(0)
    tr = code_ref.shape[0]

    @pl.when(i == 0)
    def _():
        carry_ref[...] = cin_ref[0:1, :]
        r = lax.broadcasted_iota(jnp.int32, (tr, tr), 0)
        c = lax.broadcasted_iota(jnp.int32, (tr, tr), 1)
        low_ref[...] = (c < r).astype(BF16)

    code = code_ref[...].astype(F32)
    member = (code > 0.0).astype(BF16)
    before = jnp.dot(low_ref[...], member, preferred_element_type=F32) + carry_ref[...]
    base = before + ps_ref[...]
    lane = lax.broadcasted_iota(jnp.int32, code.shape, 1)
    dest = jnp.zeros(code.shape, F32)
    for kk in range(TOP_K):
        d = jnp.sum(jnp.where(code == float(kk + 1), base, 0.0), axis=-1, keepdims=True)
        dest = jnp.where(lane == kk, d, dest)
    dest_ref[...] = jnp.transpose(dest)[0:8, :].astype(jnp.int32)
    carry = before[tr - 1:tr, :] + (code[tr - 1:tr, :] > 0.0).astype(F32)
    carry_ref[...] = carry
    cout_ref[...] = jnp.broadcast_to(carry, cout_ref.shape)


def _rank(code, carry_in, pad_start, tr):
    t = code.shape[0]
    rowl = pl.BlockSpec((tr, LANES), lambda i: (i, 0))
    small = pl.BlockSpec((8, LANES), lambda i: (0, 0))
    return pl.pallas_call(
        _rank_kernel,
        grid=(t // tr,),
        in_specs=[rowl, small, pl.BlockSpec((1, LANES), lambda i: (0, 0))],
        out_specs=[pl.BlockSpec((8, tr), lambda i: (0, i)), small],
        out_shape=[jax.ShapeDtypeStruct((8, t), jnp.int32),
                   jax.ShapeDtypeStruct((8, LANES), F32)],
        scratch_shapes=[pltpu.VMEM((1, LANES), F32), pltpu.VMEM((tr, tr), BF16)],
        compiler_params=_cparams("arbitrary"),
        name="moe_rank",
    )(code, carry_in, pad_start)


def _dest_window(tm):
    width = max(tm, LANES)
    return width, width // tm


def _scatter_kernel(dest_ref, tok_ref, rows_in_ref, rows_ref, sem, *, tm, per):
    del rows_in_ref
    base = (pl.program_id(0) % per) * tm

    def row_copy(t, kk):
        return pltpu.make_async_copy(tok_ref.at[pl.ds(t, 1), :],
                                     rows_ref.at[pl.ds(dest_ref[kk, base + t], 1), :], sem)

    def start(g, c):
        t0 = pl.multiple_of(g * SUBLANES, SUBLANES)
        for s in range(SUBLANES):
            for kk in range(TOP_K):
                row_copy(t0 + s, kk).start()
        return c

    def wait(g, c):
        t0 = pl.multiple_of(g * SUBLANES, SUBLANES)
        for s in range(SUBLANES):
            for kk in range(TOP_K):
                row_copy(t0 + s, kk).wait()
        return c

    lax.fori_loop(0, tm // SUBLANES, start, 0)
    lax.fori_loop(0, tm // SUBLANES, wait, 0)


def _scatter(dest, tok, rows_buf, tm):
    t, d = tok.shape
    width, per = _dest_window(tm)
    return pl.pallas_call(
        functools.partial(_scatter_kernel, tm=tm, per=per),
        grid=(t // tm,),
        in_specs=[pl.BlockSpec((8, width), lambda i: (0, i // per), memory_space=pltpu.SMEM),
                  pl.BlockSpec((tm, d), lambda i: (i, 0)),
                  pl.BlockSpec(memory_space=pl.ANY)],
        out_specs=pl.BlockSpec(memory_space=pl.ANY),
        out_shape=jax.ShapeDtypeStruct(rows_buf.shape, rows_buf.dtype),
        scratch_shapes=[pltpu.SemaphoreType.DMA(())],
        input_output_aliases={2: 0},
        compiler_params=_cparams("arbitrary"),
        name="moe_scatter",
    )(dest, tok, rows_buf)


def _expert_kernel(be_ref, nu_ref, nv_ref, x_ref, wgu_ref, bgu_ref, wd_ref, bd_ref, y_ref, wgu_b, wd_b):
    dff = wd_ref.shape[1]
    i = pl.program_id(0)

    @pl.when(i < nu_ref[0])
    def _():
        @pl.when(jnp.logical_or(i == 0, be_ref[i] != be_ref[jnp.maximum(i - 1, 0)]))
        def _():
            wgu_b[...] = wgu_ref[0].astype(BF16)
            wd_b[...] = wd_ref[0].astype(BF16)

        row = lax.broadcasted_iota(jnp.int32, (x_ref.shape[0], 1), 0)
        xb = _unpack_bf16_pairs(jnp.where(row < nv_ref[i], x_ref[...], 0)).astype(BF16)
        gu = jnp.dot(xb, wgu_b[...], preferred_element_type=F32) + bgu_ref[0]
        gate = jnp.minimum(gu[:, :dff], SWIGLU_LIMIT)
        up = jnp.clip(gu[:, dff:], -SWIGLU_LIMIT, SWIGLU_LIMIT)
        act = (up + 1.0) * gate * jax.nn.sigmoid(SWIGLU_ALPHA * gate)
        y_ref[...] = _pack_bf16_pairs(
            jnp.dot(act.astype(BF16), wd_b[...], preferred_element_type=F32) + bd_ref[0])


def _experts(block_e, n_used, n_valid, x_rows, wgu, bgu, wd, bd, layer):
    rows, dp = x_rows.shape
    depth, ne, d, n_gu = wgu.shape
    dff = wd.shape[2]
    nb = rows // MOE_BLOCK

    def expert(i, be, nu):
        return be[jnp.minimum(i, nu[0] - 1)]

    rowspec = pl.BlockSpec((MOE_BLOCK, dp), lambda i, be, nu, nv: (jnp.minimum(i, nu[0] - 1), 0))
    return pl.pallas_call(
        _expert_kernel,
        grid_spec=pltpu.PrefetchScalarGridSpec(
            num_scalar_prefetch=3,
            grid=(nb,),
            in_specs=[rowspec,
                      pl.BlockSpec((None, 1, d, n_gu), lambda i, be, nu, nv: (layer, expert(i, be, nu), 0, 0)),
                      pl.BlockSpec((None, 1, 1, n_gu), lambda i, be, nu, nv: (layer, expert(i, be, nu), 0, 0)),
                      pl.BlockSpec((None, 1, dff, d), lambda i, be, nu, nv: (layer, expert(i, be, nu), 0, 0)),
                      pl.BlockSpec((None, 1, 1, d), lambda i, be, nu, nv: (layer, expert(i, be, nu), 0, 0))],
            out_specs=rowspec,
            scratch_shapes=[pltpu.VMEM((d, n_gu), BF16), pltpu.VMEM((dff, d), BF16)]),
        out_shape=jax.ShapeDtypeStruct((rows, dp), jnp.int32),
        compiler_params=_cparams("arbitrary"),
        name="moe_experts",
    )(block_e, n_used, n_valid, x_rows, wgu, bgu.reshape(depth, ne, 1, n_gu), wd,
      bd.reshape(depth, ne, 1, d))


def _sc_gather(rows, idx):
    n = idx.shape[0]
    d = rows.shape[1]
    mesh = plsc.VectorSubcoreMesh(core_axis_name="c", subcore_axis_name="s",
                                  num_cores=SC_CORES, num_subcores=SC_SUBCORES)
    n_workers = mesh.num_cores * mesh.num_subcores
    per_worker = n // n_workers
    assert per_worker * n_workers == n and per_worker % LANES == 0

    @functools.partial(pl.kernel, out_type=jax.ShapeDtypeStruct((n, d), rows.dtype), mesh=mesh,
                       scratch_types=[pltpu.VMEM((LANES,), jnp.int32),
                                      pltpu.VMEM((SC_WINDOW, d), rows.dtype),
                                      pltpu.VMEM((SC_WINDOW, d), rows.dtype),
                                      pltpu.SemaphoreType.DMA((2,)),
                                      pltpu.SemaphoreType.DMA((2,))],
                       compiler_params=pltpu.CompilerParams(use_tc_tiling_on_sc=True),
                       name="moe_gather")
    def gather(rows_hbm, idx_hbm, out_hbm, idx_vmem, buf0, buf1, gsem, wsem):
        worker = lax.axis_index("c") * mesh.num_subcores + lax.axis_index("s")
        bufs = (buf0, buf1)
        n_win = LANES // SC_WINDOW

        @pl.loop(0, per_worker // LANES)
        def _(j):
            chunk = worker * (per_worker // LANES) + j
            pltpu.sync_copy(idx_hbm.at[chunk], idx_vmem)

            def write(q):
                return pltpu.async_copy(
                    bufs[q % 2], out_hbm.at[pl.ds(chunk * LANES + q * SC_WINDOW, SC_WINDOW)],
                    wsem.at[q % 2])

            gathers, writes = [], []
            for q in range(n_win):
                if q >= 2:
                    writes[q - 2].wait()
                gathers.append(pltpu.async_copy(
                    rows_hbm.at[idx_vmem.at[pl.ds(q * SC_WINDOW, SC_WINDOW)]], bufs[q % 2],
                    gsem.at[q % 2]))
                if q >= 1:
                    gathers[q - 1].wait()
                    writes.append(write(q - 1))
            gathers[n_win - 1].wait()
            writes.append(write(n_win - 1))
            writes[n_win - 2].wait()
            writes[n_win - 1].wait()

    return gather(rows, idx.reshape(n // LANES, LANES))


def _sc_scatter(tok, dest, n_rows):
    t, d = tok.shape
    mesh = plsc.VectorSubcoreMesh(core_axis_name="c", subcore_axis_name="s",
                                  num_cores=SC_CORES, num_subcores=SC_SUBCORES)
    n_workers = mesh.num_cores * mesh.num_subcores
    per_worker = t // n_workers
    assert per_worker * n_workers == t and per_worker % LANES == 0
    chunks = t // LANES
    n_win = LANES // SC_WINDOW
    lanes16 = 16

    @functools.partial(pl.kernel, out_type=jax.ShapeDtypeStruct((n_rows, d), tok.dtype), mesh=mesh,
                       scratch_types=[pltpu.VMEM((TOP_K, LANES), jnp.int32),
                                      pltpu.VMEM((SC_WINDOW, d), tok.dtype),
                                      pltpu.VMEM((SC_WINDOW, d), tok.dtype),
                                      pltpu.SemaphoreType.DMA((2,)),
                                      pltpu.SemaphoreType.DMA((2,))],
                       compiler_params=pltpu.CompilerParams(use_tc_tiling_on_sc=True),
                       name="moe_scatter_sc")
    def scatter(tok_hbm, idx_hbm, rows_hbm, idx_vmem, buf0, buf1, lsem, ssem):
        worker = lax.axis_index("c") * mesh.num_subcores + lax.axis_index("s")
        bufs = (buf0, buf1)

        @pl.loop(0, per_worker // LANES)
        def _(j):
            chunk = worker * (per_worker // LANES) + j
            for kk in range(TOP_K):
                pltpu.sync_copy(idx_hbm.at[kk * chunks + chunk], idx_vmem.at[kk])

            def load(q):
                return pltpu.async_copy(
                    tok_hbm.at[pl.ds(chunk * LANES + q * SC_WINDOW, SC_WINDOW)], bufs[q % 2],
                    lsem.at[q % 2])

            def store_all(q):
                copies = []
                for kk in range(TOP_K):
                    for h in range(SC_WINDOW // lanes16):
                        rows16 = idx_vmem[kk, pl.ds(q * SC_WINDOW + h * lanes16, lanes16)]
                        copies.append(pltpu.async_copy(
                            bufs[q % 2].at[pl.ds(h * lanes16, lanes16)], rows_hbm.at[rows16],
                            ssem.at[q % 2]))
                return copies

            loads = [load(0)]
            stores = []
            for q in range(n_win):
                if q + 1 < n_win:
                    if q >= 1:
                        for cp in stores[q - 1]:
                            cp.wait()
                    loads.append(load(q + 1))
                loads[q].wait()
                stores.append(store_all(q))
            for q in (n_win - 2, n_win - 1):
                for cp in stores[q]:
                    cp.wait()

    return scatter(tok, dest[:TOP_K].reshape(TOP_K * chunks, LANES))


def _combine_dense_kernel(y0_ref, y1_ref, y2_ref, y3_ref, gate_ref, x_ref, mod_ref, g_ref, b_ref,
                          o_ref, *, alpha):
    gates = gate_ref[...]
    f = _unpack_bf16_pairs(y0_ref[...]) * gates[:, 0:1]
    for kk, y_ref in enumerate((y1_ref, y2_ref, y3_ref), start=1):
        f = f + _unpack_bf16_pairs(y_ref[...]) * gates[:, kk:kk + 1]
    gate2 = mod_ref[0, 5:6, :]
    o_ref[...] = _layer_norm(alpha * x_ref[...] + gate2 * f, g_ref[...], b_ref[...])


def _combine_dense(y_slots, gates, x1, mod, g, b, seq0, seq_len, tm, alpha):
    t, d = x1.shape
    nt = t // tm
    tiles_per_seq = seq_len // tm
    rowd = pl.BlockSpec((tm, d), lambda i: (i, 0))
    vec = pl.BlockSpec((1, d), lambda i: (0, 0))
    slot_specs = [pl.BlockSpec((tm, y_slots.shape[1]),
                               functools.partial(lambda i, kk: (kk * nt + i, 0), kk=kk))
                  for kk in range(TOP_K)]
    return pl.pallas_call(
        functools.partial(_combine_dense_kernel, alpha=alpha),
        grid=(nt,),
        in_specs=slot_specs + [pl.BlockSpec((tm, LANES), lambda i: (i, 0)), rowd,
                               pl.BlockSpec((1, 6, d), lambda i: (seq0 + i // tiles_per_seq, 0, 0)),
                               vec, vec],
        out_specs=rowd,
        out_shape=jax.ShapeDtypeStruct((t, d), F32),
        compiler_params=_cparams("parallel"),
        name="moe_combine",
    )(y_slots, y_slots, y_slots, y_slots, gates, x1, mod, g.reshape(1, d), b.reshape(1, d))


def _combine_kernel(dest_ref, y_hbm, gate_ref, x_ref, mod_ref, g_ref, b_ref, o_ref, ybuf, sem,
                    *, alpha, per):
    tm = x_ref.shape[0]
    base = (pl.program_id(0) % per) * tm

    def row_copy(t, kk):
        return pltpu.make_async_copy(y_hbm.at[pl.ds(dest_ref[kk, base + t], 1), :],
                                     ybuf.at[kk, pl.ds(t, 1), :], sem)

    def start(g, c):
        t0 = pl.multiple_of(g * SUBLANES, SUBLANES)
        for s in range(SUBLANES):
            for kk in range(TOP_K):
                row_copy(t0 + s, kk).start()
        return c

    def wait(g, c):
        t0 = pl.multiple_of(g * SUBLANES, SUBLANES)
        for s in range(SUBLANES):
            for kk in range(TOP_K):
                row_copy(t0 + s, kk).wait()
        return c

    lax.fori_loop(0, tm // SUBLANES, start, 0)
    lax.fori_loop(0, tm // SUBLANES, wait, 0)

    gates = gate_ref[...]
    f = _unpack_bf16_pairs(ybuf[0]) * gates[:, 0:1]
    for kk in range(1, TOP_K):
        f = f + _unpack_bf16_pairs(ybuf[kk]) * gates[:, kk:kk + 1]
    gate2 = mod_ref[0, 5:6, :]
    o_ref[...] = _layer_norm(alpha * x_ref[...] + gate2 * f, g_ref[...], b_ref[...])


def _combine(dest, y_rows, gates, x1, mod, g, b, seq0, seq_len, tm, alpha):
    t, d = x1.shape
    tiles_per_seq = seq_len // tm
    width, per = _dest_window(tm)
    rowd = pl.BlockSpec((tm, d), lambda i: (i, 0))
    vec = pl.BlockSpec((1, d), lambda i: (0, 0))
    return pl.pallas_call(
        functools.partial(_combine_kernel, alpha=alpha, per=per),
        grid=(t // tm,),
        in_specs=[pl.BlockSpec((8, width), lambda i: (0, i // per), memory_space=pltpu.SMEM),
                  pl.BlockSpec(memory_space=pl.ANY),
                  pl.BlockSpec((tm, LANES), lambda i: (i, 0)),
                  rowd,
                  pl.BlockSpec((1, 6, d), lambda i: (seq0 + i // tiles_per_seq, 0, 0)),
                  vec, vec],
        out_specs=rowd,
        out_shape=jax.ShapeDtypeStruct((t, d), F32),
        scratch_shapes=[pltpu.VMEM((TOP_K, tm, y_rows.shape[1]), y_rows.dtype),
                        pltpu.SemaphoreType.DMA(())],
        compiler_params=_cparams("arbitrary"),
        name="moe_combine",
    )(dest, y_rows, gates, x1, mod, g.reshape(1, d), b.reshape(1, d))


def _block_diag(wh):
    heads, hd, _ = wh.shape
    eye = jnp.eye(heads, dtype=wh.dtype)
    return jnp.einsum('hij,hg->higj', wh, eye).reshape(heads * hd, heads * hd)


def kernel(x_prompt, x_sample, cache_k, cache_v, state_conv, state_lru, c_prompt, c_sample, ln_in_g, ln_in_b, w_ada, b_ada, w_in, conv_w, conv_b, gate_a_w, gate_a_b, gate_x_w, gate_x_b, lru_lambda, w_out, ln1_g, ln1_b, router_w, router_b, w_gu, b_gu, w_down, b_down, ln2_g, ln2_b):
    nb_p, s_p, d = x_prompt.shape
    nb_s, s_s, _ = x_sample.shape
    depth = w_in.shape[0]
    past = cache_k.shape[2]
    w = state_lru.shape[-1]
    hw = SB_HEADS * SB_HEAD_DIM
    t_p, t_s = nb_p * s_p, nb_s * s_s
    alpha = float((2 * depth) ** 0.25)

    tm_p, tm_s = min(512, s_p), min(512, s_s)
    ts_p, ts_s = min(256, s_p), min(256, s_s)
    tq_p, tq_s = min(SB_TQ, s_p), min(SB_TQ, s_s)
    tg_p, tg_s = min(256, s_p), min(256, s_s)
    tr_p, tr_s = min(512, t_p), min(512, t_s)

    n_seq = nb_p + nb_s
    c_all = jnp.concatenate([c_prompt, c_sample, jnp.zeros((-n_seq % 8, d), F32)], axis=0)
    mod_all = _adaln(c_all, w_ada, b_ada).reshape(depth, c_all.shape[0], 6, d)

    xp = _ln_in(x_prompt.reshape(t_p, d), ln_in_g, ln_in_b, tm_p)
    xs = _ln_in(x_sample.reshape(t_s, d), ln_in_g, ln_in_b, tm_s)

    upper = jnp.asarray(np.arange(SB_SUB)[:, None] > np.arange(SB_SUB)[None, :], BF16)
    kpad_s = -(-(past + s_s) // SB_TK) * SB_TK
    zero_conv = jnp.zeros((nb_p, CONV_WIDTH - 1, w), F32)
    zero_h = jnp.zeros((nb_p, w), F32)

    tk_total = (t_p + t_s) * TOP_K
    n_blocks = -(-(tk_total + N_EXPERTS * (MOE_BLOCK - 1)) // MOE_BLOCK)
    n_rows = n_blocks * MOE_BLOCK

    outs = [[] for _ in range(8)]
    kv_p = kv_s = None
    for l in range(depth):
        mod = mod_all[l]
        w_in_b = w_in[l].astype(BF16)
        wo_b = w_out[l].astype(BF16)
        wa = _block_diag(gate_a_w[l]).astype(BF16)
        wx = _block_diag(gate_x_w[l]).astype(BF16)
        rw_pad = jnp.pad(router_w[l], ((0, 0), (0, LANES - N_EXPERTS)))
        rw_hi = rw_pad.astype(BF16)
        rw_lo = (rw_pad - rw_hi.astype(F32)).astype(BF16)
        rb_pad = jnp.pad(router_b[l], (0, LANES - N_EXPERTS), constant_values=NEG_BIG).reshape(1, LANES)
        lru_w = (conv_w[l], conv_b[l], wa, gate_a_b[l], wx, gate_x_b[l], lru_lambda[l])

        xr, gy, q, k, v, kb, vb = _inproj(xp, mod, w_in_b, 0, s_p, tm_p, l, depth, kv_p)
        kv_p = (k, v)
        lru_p, nconv_p, nh_p = _rglru(xr, gy, zero_conv, zero_h, *lru_w, nb_p, s_p, ts_p)
        att_p = _stick_breaking(q, kb.reshape(nb_p, s_p, hw), vb.reshape(nb_p, s_p, hw), upper,
                                nb_p, s_p, tq_p, 0)
        outs[2].append(nconv_p)
        outs[3].append(nh_p.reshape(nb_p, w))

        xr, gy, q, k, v, kb, vb = _inproj(xs, mod, w_in_b, nb_p, s_s, tm_s, l, depth, kv_s)
        kv_s = (k, v)
        lru_s, nconv_s, nh_s = _rglru(xr, gy, state_conv[l], state_lru[l], *lru_w, nb_s, s_s, ts_s)
        kv_pad = jnp.zeros((nb_s, kpad_s - past - s_s, hw), BF16)
        k_all = jnp.concatenate([cache_k[l].reshape(nb_s, past, hw).astype(BF16),
                                 kb.reshape(nb_s, s_s, hw), kv_pad], axis=1)
        v_all = jnp.concatenate([cache_v[l].reshape(nb_s, past, hw).astype(BF16),
                                 vb.reshape(nb_s, s_s, hw), kv_pad], axis=1)
        att_s = _stick_breaking(q, k_all, v_all, upper, nb_s, s_s, tq_s, past)
        outs[6].append(nconv_s)
        outs[7].append(nh_s.reshape(nb_s, w))

        x1_p, tok_p, code_p, gate_p, cnt_p = _outproj(lru_p, att_p, xp, mod, wo_b, ln1_g[l], ln1_b[l],
                                                      rw_hi, rw_lo, rb_pad,0, s_p, tm_p, alpha)
        x1_s, tok_s, code_s, gate_s, cnt_s = _outproj(lru_s, att_s, xs, mod, wo_b, ln1_g[l], ln1_b[l],
                                                      rw_hi, rw_lo, rb_pad,nb_p, s_s, tm_s, alpha)

        counts = (cnt_p[0, :N_EXPERTS] + cnt_s[0, :N_EXPERTS]).astype(jnp.int32)
        padded = (counts + MOE_BLOCK - 1) // MOE_BLOCK * MOE_BLOCK
        pad_end = jnp.cumsum(padded)
        pad_start = jnp.pad((pad_end - padded).astype(F32), (0, LANES - N_EXPERTS)).reshape(1, LANES)
        block_row0 = jnp.arange(n_blocks, dtype=jnp.int32) * MOE_BLOCK
        block_e = jnp.minimum(jnp.sum(pad_end[None, :] <= block_row0[:, None], axis=1),
                              N_EXPERTS - 1).astype(jnp.int32)

        n_used = (pad_end[N_EXPERTS - 1:] // MOE_BLOCK).astype(jnp.int32)
        n_valid = jnp.clip(counts[block_e] - (block_row0 - (pad_end - padded)[block_e]), 0, MOE_BLOCK)

        dest_p, carry = _rank(code_p, jnp.zeros((8, LANES), F32), pad_start, tr_p)
        dest_s, _ = _rank(code_s, carry, pad_start, tr_s)

        x_rows = _sc_scatter(tok_p, dest_p, n_rows)
        x_rows = _scatter(dest_s, tok_s, x_rows, tg_s)
        y_rows = _experts(block_e, n_used, n_valid, x_rows, w_gu, b_gu, w_down, b_down, l)

        yg_p = _sc_gather(y_rows, dest_p[:TOP_K].reshape(TOP_K * t_p))
        xp = _combine_dense(yg_p, gate_p, x1_p, mod, ln2_g[l], ln2_b[l], 0, s_p, tg_p, alpha)
        xs = _combine(dest_s, y_rows, gate_s, x1_s, mod, ln2_g[l], ln2_b[l], nb_p, s_s, tg_s, alpha)

    return (xp.reshape(nb_p, s_p, d), xs.reshape(nb_s, s_s, d),
            kv_p[0].reshape(depth, nb_p, s_p, SB_HEADS, SB_HEAD_DIM),
            kv_p[1].reshape(depth, nb_p, s_p, SB_HEADS, SB_HEAD_DIM),
            jnp.stack(outs[2]), jnp.stack(outs[3]),
            kv_s[0].reshape(depth, nb_s, s_s, SB_HEADS, SB_HEAD_DIM),
            kv_s[1].reshape(depth, nb_s, s_s, SB_HEADS, SB_HEAD_DIM),
            jnp.stack(outs[6]), jnp.stack(outs[7]))
```

```python
import functools

import jax
import jax.numpy as jnp
import numpy as np
from jax import lax
from jax.experimental import pallas as pl
from jax.experimental.pallas import tpu as pltpu
from jax.experimental.pallas import tpu_sc as plsc

F32 = jnp.float32
BF16 = jnp.bfloat16

LANES = 128
SUBLANES = 8
LRU_HEADS = 8
CONV_WIDTH = 4
RG_C = 8.0
SB_HEADS = 4
SB_HEAD_DIM = 128
SB_SCALE = SB_HEAD_DIM ** -0.5
SB_SUB = 256
SB_TQ = 256
SB_TK = 512
SB_DEAD_LOG2 = 160.0
LOG2E = 1.4426950408889634
N_EXPERTS = 32
TOP_K = 4
SWIGLU_LIMIT = 7.0
SWIGLU_ALPHA = 1.702
MOE_BLOCK = 512
SC_CORES, SC_SUBCORES = 2, 16
SC_WINDOW = 32
LN_EPS = 1e-5
NEG_BIG = -1e30
VMEM_LIMIT = 56 * 1024 * 1024


def _cparams(*sem):
    return pltpu.CompilerParams(dimension_semantics=sem, vmem_limit_bytes=VMEM_LIMIT)


def _split3(a):
    hi = a.astype(BF16)
    r1 = a - hi.astype(F32)
    mid = r1.astype(BF16)
    lo = (r1 - mid.astype(F32)).astype(BF16)
    return hi, mid, lo


def _dot_f32(a, b):
    a0, a1, a2 = _split3(a)
    b0, b1, b2 = _split3(b)
    d = functools.partial(jnp.dot, preferred_element_type=F32)
    return (d(a0, b0) + (d(a0, b1) + d(a1, b0))
            + (d(a0, b2) + d(a1, b1) + d(a2, b0)))


def _layer_norm(y, g, b):
    mu = jnp.mean(y, axis=-1, keepdims=True)
    d = y - mu
    var = jnp.mean(d * d, axis=-1, keepdims=True)
    return d * lax.rsqrt(var + LN_EPS) * g + b


def _softplus(z):
    return jnp.maximum(z, 0.0) + jnp.log(1.0 + jnp.exp(-jnp.abs(z)))


def _pack_bf16_pairs(x):
    h = x.shape[1] // 2
    bits = lax.bitcast_convert_type(x.astype(BF16).astype(F32), jnp.int32)
    return lax.shift_right_logical(bits[:, :h], 16) | (bits[:, h:] & jnp.int32(-65536))


def _unpack_bf16_pairs(p):
    lo = lax.bitcast_convert_type(p << 16, F32)
    hi = lax.bitcast_convert_type(p & jnp.int32(-65536), F32)
    return jnp.concatenate([lo, hi], axis=1)


def _gelu_tanh(x):
    c = np.sqrt(2.0 / np.pi).astype(np.float32)
    return 0.5 * x * (1.0 + jnp.tanh(c * (x + 0.044715 * (x * x * x))))


def _adaln_kernel(c_ref, w_ref, b_ref, o_ref):
    c = c_ref[...]
    s = c * jax.nn.sigmoid(c)
    o_ref[0] = _dot_f32(s, w_ref[0]) + b_ref[0]


def _adaln(c_all, w_ada, b_ada):
    depth, d, n = w_ada.shape
    rows = c_all.shape[0]
    tn = 1024
    return pl.pallas_call(
        _adaln_kernel,
        grid=(depth, n // tn),
        in_specs=[pl.BlockSpec((rows, d), lambda l, j: (0, 0)),
                  pl.BlockSpec((1, d, tn), lambda l, j: (l, 0, j)),
                  pl.BlockSpec((1, 1, tn), lambda l, j: (l, 0, j))],
        out_specs=pl.BlockSpec((1, rows, tn), lambda l, j: (l, 0, j)),
        out_shape=jax.ShapeDtypeStruct((depth, rows, n), F32),
        compiler_params=_cparams("parallel", "parallel"),
        name="adaln",
    )(c_all, w_ada, b_ada.reshape(depth, 1, n))


def _ln_kernel(x_ref, g_ref, b_ref, o_ref):
    o_ref[...] = _layer_norm(x_ref[...], g_ref[...], b_ref[...])


def _ln_in(x2, g, b, tm):
    t, d = x2.shape
    return pl.pallas_call(
        _ln_kernel,
        grid=(t // tm,),
        in_specs=[pl.BlockSpec((tm, d), lambda i: (i, 0)),
                  pl.BlockSpec((1, d), lambda i: (0, 0)),
                  pl.BlockSpec((1, d), lambda i: (0, 0))],
        out_specs=pl.BlockSpec((tm, d), lambda i: (i, 0)),
        out_shape=jax.ShapeDtypeStruct((t, d), F32),
        compiler_params=_cparams("parallel"),
        name="ln_in",
    )(x2, g.reshape(1, d), b.reshape(1, d))


def _inproj_kernel(x_ref, mod_ref, w_ref, *rest):
    xr_ref, gy_ref, q_ref, k_ref, v_ref, kb_ref, vb_ref = rest[-7:]
    w = xr_ref.shape[-1]

    def store_heads(ref, val):
        for hh in range(SB_HEADS):
            ref[:, hh, :] = val[:, hh * SB_HEAD_DIM:(hh + 1) * SB_HEAD_DIM]

    shift = mod_ref[0, 0:1, :]
    scale = mod_ref[0, 1:2, :]
    h = (x_ref[...] * (1.0 + scale) + shift).astype(BF16)

    def proj(j):
        return jnp.dot(h, w_ref[:, j * w:(j + 1) * w], preferred_element_type=F32)

    xr_ref[...] = proj(0)
    gy_ref[...] = _gelu_tanh(proj(1))
    q_ref[...] = (proj(2) * (SB_SCALE * LOG2E)).astype(BF16)
    k = proj(3)
    store_heads(k_ref, k)
    kb_ref[...] = k.astype(BF16)
    v = proj(4)
    store_heads(v_ref, v)
    vb_ref[...] = v.astype(BF16)


def _inproj(x2, mod, w_in_b, seq0, seq_len, tm, layer, depth, kv_stacks):
    t, d = x2.shape
    w = w_in_b.shape[1] // 5
    nt = t // tm
    tiles_per_seq = seq_len // tm
    row = pl.BlockSpec((tm, w), lambda i: (i, 0))
    heads = pl.BlockSpec((tm, SB_HEADS, SB_HEAD_DIM), lambda i: (layer * nt + i, 0, 0))
    f32o = jax.ShapeDtypeStruct((t, w), F32)
    bf16o = jax.ShapeDtypeStruct((t, w), BF16)
    stack = jax.ShapeDtypeStruct((depth * t, SB_HEADS, SB_HEAD_DIM), F32)
    in_specs = [pl.BlockSpec((tm, d), lambda i: (i, 0)),
                pl.BlockSpec((1, 6, d), lambda i: (seq0 + i // tiles_per_seq, 0, 0)),
                pl.BlockSpec((d, 5 * w), lambda i: (0, 0))]
    args = [x2, mod, w_in_b]
    aliases = {}
    if kv_stacks is not None:
        in_specs += [pl.BlockSpec(memory_space=pl.ANY)] * 2
        args += list(kv_stacks)
        aliases = {3: 3, 4: 4}
    return pl.pallas_call(
        _inproj_kernel,
        grid=(nt,),
        in_specs=in_specs,
        out_specs=[row, row, row, heads, heads, row, row],
        out_shape=[f32o, f32o, bf16o, stack, stack, bf16o, bf16o],
        input_output_aliases=aliases,
        compiler_params=_cparams("parallel"),
        name="inproj",
    )(*args)


def _rglru_kernel(xr_ref, gy_ref, cs_ref, h0_ref, cw_ref, cb_ref, wa_ref, ba_ref, wx_ref, bx_ref,
                  lam_ref, out_ref, nconv_ref, nh_ref, tail_ref, h_ref):
    ts, w = xr_ref.shape
    j = pl.program_id(1)

    @pl.when(j == 0)
    def _():
        tail_ref[...] = jnp.zeros((SUBLANES, w), F32)
        tail_ref[SUBLANES - (CONV_WIDTH - 1):SUBLANES, :] = cs_ref[0]
        h_ref[...] = h0_ref[0]

    xr = xr_ref[...]
    prev = tail_ref[...]
    row8 = lax.broadcasted_iota(jnp.int32, (SUBLANES, w), 0)
    xc = cb_ref[...] + xr * cw_ref[CONV_WIDTH - 1:CONV_WIDTH, :]
    for dly in range(1, CONV_WIDTH):
        sh = pltpu.roll(xr, dly, 0)
        head = jnp.where(row8 < dly, pltpu.roll(prev, dly, 0), sh[0:SUBLANES, :])
        sh = jnp.concatenate([head, sh[SUBLANES:, :]], axis=0)
        xc = xc + sh * cw_ref[CONV_WIDTH - 1 - dly:CONV_WIDTH - dly, :]
    nconv_ref[0] = xr[ts - (CONV_WIDTH - 1):ts, :]
    tail_ref[...] = xr[ts - SUBLANES:ts, :]

    xcb = xc.astype(BF16)
    r = jax.nn.sigmoid(jnp.dot(xcb, wa_ref[...], preferred_element_type=F32) + ba_ref[...])
    gi = jax.nn.sigmoid(jnp.dot(xcb, wx_ref[...], preferred_element_type=F32) + bx_ref[...])
    log_a = (RG_C * r) * (-_softplus(-lam_ref[...]))
    a = jnp.exp(log_a)
    one_m = 1.0 - a * a
    b = jnp.where(one_m > 0.0, one_m * lax.rsqrt(one_m), 0.0) * (gi * xc)

    sub = lax.broadcasted_iota(jnp.int32, (ts, w), 0) & (SUBLANES - 1)
    dist = 1
    while dist < SUBLANES:
        keep = sub >= dist
        a_prev = jnp.where(keep, pltpu.roll(a, dist, 0), 1.0)
        b_prev = jnp.where(keep, pltpu.roll(b, dist, 0), 0.0)
        b = a * b_prev + b
        a = a * a_prev
        dist *= 2
    carry = h_ref[...]
    groups = []
    for g0 in range(0, ts, SUBLANES):
        hg = b[g0:g0 + SUBLANES, :] + a[g0:g0 + SUBLANES, :] * carry
        groups.append(hg)
        carry = hg[SUBLANES - 1:SUBLANES, :]
    hs = jnp.concatenate(groups, axis=0)
    h_last = carry
    h_ref[...] = h_last
    nh_ref[0] = h_last
    out_ref[...] = (hs * gy_ref[...]).astype(BF16)


def _rglru(xr, gy, conv_state, h0, cw, cb, wa, ba, wx, bx, lam, nseq, seq_len, ts):
    t, w = xr.shape
    n = seq_len // ts
    row = pl.BlockSpec((ts, w), lambda b, j: (b * n + j, 0))
    vec = pl.BlockSpec((1, w), lambda b, j: (0, 0))
    return pl.pallas_call(
        _rglru_kernel,
        grid=(nseq, n),
        in_specs=[row, row,
                  pl.BlockSpec((1, CONV_WIDTH - 1, w), lambda b, j: (b, 0, 0)),
                  pl.BlockSpec((1, 1, w), lambda b, j: (b, 0, 0)),
                  pl.BlockSpec((CONV_WIDTH, w), lambda b, j: (0, 0)),
                  vec,
                  pl.BlockSpec((w, w), lambda b, j: (0, 0)), vec,
                  pl.BlockSpec((w, w), lambda b, j: (0, 0)), vec,
                  vec],
        out_specs=[row,
                   pl.BlockSpec((1, CONV_WIDTH - 1, w), lambda b, j: (b, 0, 0)),
                   pl.BlockSpec((1, 1, w), lambda b, j: (b, 0, 0))],
        out_shape=[jax.ShapeDtypeStruct((t, w), BF16),
                   jax.ShapeDtypeStruct((nseq, CONV_WIDTH - 1, w), F32),
                   jax.ShapeDtypeStruct((nseq, 1, w), F32)],
        scratch_shapes=[pltpu.VMEM((SUBLANES, w), F32), pltpu.VMEM((1, w), F32)],
        compiler_params=_cparams("parallel", "arbitrary"),
        name="rglru",
    )(xr, gy, conv_state, h0.reshape(nseq, 1, w), cw, cb.reshape(1, w), wa, ba.reshape(1, w),
      wx, bx.reshape(1, w), lam.reshape(1, w))


def _sb_scores(q, k):
    return lax.dot_general(q, k, (((1,), (1,)), ((), ())), preferred_element_type=F32)


def _sb_block(z, v, upper, off, acc, mask):
    neg_abs = lax.bitcast_convert_type(
        lax.bitcast_convert_type(z, jnp.int32) | jnp.int32(-2 ** 31), F32)
    sp = jnp.maximum(z, 0.0) + jnp.log2(1.0 + jnp.exp2(neg_abs))
    if mask is not None:
        sp = jnp.where(mask, sp, 0.0)
    spb = sp.astype(BF16)
    later = []
    for lo in range(SB_TK - SB_SUB, -1, -SB_SUB):
        later.append(jnp.dot(spb[:, lo:lo + SB_SUB], upper, preferred_element_type=F32) + off)
        off = off + jnp.sum(sp[:, lo:lo + SB_SUB], axis=-1, keepdims=True)
    wgt = jnp.exp2((z - sp) - jnp.concatenate(later[::-1], axis=1))
    if mask is not None:
        wgt = jnp.where(mask, wgt, 0.0)
    acc = acc + jnp.dot(wgt.astype(BF16), v, preferred_element_type=F32)
    return off, acc


def _sb_kernel(q_ref, k_ref, v_ref, up_ref, o_ref, z_ref, *, tq, q_pos0):
    i = pl.program_id(2)
    tk = SB_TK
    q = q_ref[...]
    upper = up_ref[...]
    n_full = (q_pos0 + i * tq) // tk
    off = jnp.zeros((tq, 1), F32)
    acc = jnp.zeros((tq, SB_HEAD_DIM), F32)

    def block_start(j):
        return pl.multiple_of(j * tk, tk)

    z_ref[...] = _sb_scores(q, k_ref[pl.ds(block_start(jnp.maximum(n_full - 1, 0)), tk), :])

    start = block_start(n_full)
    q_pos = q_pos0 + i * tq + lax.broadcasted_iota(jnp.int32, (tq, tk), 0)
    mask = (start + lax.broadcasted_iota(jnp.int32, (tq, tk), 1)) < q_pos
    off, acc = _sb_block(_sb_scores(q, k_ref[pl.ds(start, tk), :]), v_ref[pl.ds(start, tk), :],
                         upper, off, acc, mask)

    def live(off):
        return (jnp.min(off, axis=0, keepdims=True)[0, 0] < SB_DEAD_LOG2).astype(jnp.int32)

    def cond(carry):
        t, alive, _, _ = carry
        return jnp.logical_and(t < n_full, alive > 0)

    def body(carry):
        t, _, off, acc = carry
        j = n_full - 1 - t
        z = z_ref[...]
        z_ref[...] = _sb_scores(q, k_ref[pl.ds(block_start(jnp.maximum(j - 1, 0)), tk), :])
        off, acc = _sb_block(z, v_ref[pl.ds(block_start(j), tk), :], upper, off, acc, None)
        return t + 1, live(off), off, acc

    _, _, off, acc = lax.while_loop(cond, body, (jnp.int32(0), live(off), off, acc))
    o_ref[...] = acc.astype(o_ref.dtype)


def _stick_breaking(q, k_all, v_all, upper, nseq, seq_len, tq, q_pos0):
    t, hw = q.shape
    kpad = k_all.shape[1]
    nq = seq_len // tq
    half = SB_SUB
    assert SB_TK % tq == 0 and q_pos0 % tq == 0 and kpad % SB_TK == 0
    assert kpad >= ((q_pos0 + seq_len - 1) // SB_TK + 1) * SB_TK
    kern = functools.partial(_sb_kernel, tq=tq, q_pos0=q_pos0)
    qspec = pl.BlockSpec((tq, SB_HEAD_DIM), lambda b, h, i: (b * nq + i, h))
    kvspec = pl.BlockSpec((None, kpad, SB_HEAD_DIM), lambda b, h, i: (b, 0, h))
    return pl.pallas_call(
        kern,
        grid=(nseq, SB_HEADS, nq),
        in_specs=[qspec, kvspec, kvspec, pl.BlockSpec((half, half), lambda b, h, i: (0, 0))],
        out_specs=qspec,
        out_shape=jax.ShapeDtypeStruct((t, hw), BF16),
        scratch_shapes=[pltpu.VMEM((tq, SB_TK), F32)],
        compiler_params=_cparams("parallel", "parallel", "arbitrary"),
        name="stick_breaking",
    )(q, k_all, v_all, upper)


def _outproj_kernel(lru_ref, att_ref, x_ref, mod_ref, wo_ref, g_ref, b_ref, rwh_ref, rwl_ref, rb_ref,
                    x1_ref, tok_ref, code_ref, gate_ref, cnt_ref, *, alpha, n_chains):
    i = pl.program_id(0)
    tm, w = lru_ref.shape
    out = (jnp.dot(lru_ref[...], wo_ref[0:w, :], preferred_element_type=F32)
           + jnp.dot(att_ref[...], wo_ref[w:2 * w, :], preferred_element_type=F32))
    gate1 = mod_ref[0, 2:3, :]

    @pl.when(i == 0)
    def _():
        cnt_ref[...] = jnp.zeros(cnt_ref.shape, F32)

    rc = tm // n_chains
    for c in range(n_chains):
        rows = slice(c * rc, (c + 1) * rc)
        x1 = _layer_norm(alpha * x_ref[rows, :] + gate1 * out[rows, :], g_ref[...], b_ref[...])
        x1_ref[rows, :] = x1
        tok = x1 * (1.0 + mod_ref[0, 4:5, :]) + mod_ref[0, 3:4, :]
        tok_ref[rows, :] = _pack_bf16_pairs(tok)

        t_hi = tok.astype(BF16)
        t_lo = (tok - t_hi.astype(F32)).astype(BF16)
        logits = (jnp.dot(t_hi, rwh_ref[...], preferred_element_type=F32)
                  + (jnp.dot(t_hi, rwl_ref[...], preferred_element_type=F32)
                     + jnp.dot(t_lo, rwh_ref[...], preferred_element_type=F32))
                  + rb_ref[...])
        lane = lax.broadcasted_iota(jnp.int32, logits.shape, 1).astype(F32)
        code = jnp.zeros(logits.shape, F32)
        gates = jnp.zeros(logits.shape, F32)
        denom = jnp.zeros((rc, 1), F32)
        top0 = None
        for kk in range(TOP_K):
            m = jnp.max(logits, axis=-1, keepdims=True)
            idx = jnp.min(jnp.where(logits == m, lane, float(LANES)), axis=-1, keepdims=True)
            hit = lane == idx
            if kk == 0:
                top0 = m
            e = jnp.exp(m - top0)
            denom = denom + e
            code = jnp.where(hit, float(kk + 1), code)
            gates = jnp.where(lane == kk, e, gates)
            logits = jnp.where(hit, -jnp.inf, logits)
        code_ref[rows, :] = code.astype(BF16)
        gate_ref[rows, :] = gates / denom
        cnt_ref[...] += jnp.sum((code > 0.0).astype(F32), axis=0, keepdims=True)


def _outproj(lru, att, x2, mod, wo_b, g, b, rw_hi, rw_lo, rb_pad, seq0, seq_len, tm, alpha):
    t, d = x2.shape
    w = lru.shape[1]
    tiles_per_seq = seq_len // tm
    n_chains = 2 if tm % 32 == 0 else 1
    rowd = pl.BlockSpec((tm, d), lambda i: (i, 0))
    roww = pl.BlockSpec((tm, w), lambda i: (i, 0))
    rowl = pl.BlockSpec((tm, LANES), lambda i: (i, 0))
    vec = pl.BlockSpec((1, d), lambda i: (0, 0))
    return pl.pallas_call(
        functools.partial(_outproj_kernel, alpha=alpha, n_chains=n_chains),
        grid=(t // tm,),
        in_specs=[roww, roww, rowd,
                  pl.BlockSpec((1, 6, d), lambda i: (seq0 + i // tiles_per_seq, 0, 0)),
                  pl.BlockSpec((2 * w, d), lambda i: (0, 0)),
                  vec, vec,
                  pl.BlockSpec((d, LANES), lambda i: (0, 0)),
                  pl.BlockSpec((d, LANES), lambda i: (0, 0)),
                  pl.BlockSpec((1, LANES), lambda i: (0, 0))],
        out_specs=[rowd, pl.BlockSpec((tm, d // 2), lambda i: (i, 0)), rowl, rowl,
                   pl.BlockSpec((8, LANES), lambda i: (0, 0))],
        out_shape=[jax.ShapeDtypeStruct((t, d), F32), jax.ShapeDtypeStruct((t, d // 2), jnp.int32),
                   jax.ShapeDtypeStruct((t, LANES), BF16), jax.ShapeDtypeStruct((t, LANES), F32),
                   jax.ShapeDtypeStruct((8, LANES), F32)],
        compiler_params=_cparams("arbitrary"),
        name="outproj_router",
    )(lru, att, x2, mod, wo_b, g.reshape(1, d), b.reshape(1, d), rw_hi, rw_lo, rb_pad)


def _rank_kernel(code_ref, cin_ref, ps_ref, dest_ref, cout_ref, carry_ref, low_ref):
    i = pl.program_id(0)
    tr = code_ref.shape[0]

    @pl.when(i == 0)
    def _():
        carry_ref[...] = cin_ref[0:1, :]
        r = lax.broadcasted_iota(jnp.int32, (tr, tr), 0)
        c = lax.broadcasted_iota(jnp.int32, (tr, tr), 1)
        low_ref[...] = (c < r).astype(BF16)

    code = code_ref[...].astype(F32)
    member = (code > 0.0).astype(BF16)
    before = jnp.dot(low_ref[...], member, preferred_element_type=F32) + carry_ref[...]
    base = before + ps_ref[...]
    lane = lax.broadcasted_iota(jnp.int32, code.shape, 1)
    dest = jnp.zeros(code.shape, F32)
    for kk in range(TOP_K):
        d = jnp.sum(jnp.where(code == float(kk + 1), base, 0.0), axis=-1, keepdims=True)
        dest = jnp.where(lane == kk, d, dest)
    dest_ref[...] = jnp.transpose(dest)[0:8, :].astype(jnp.int32)
    carry = before[tr - 1:tr, :] + (code[tr - 1:tr, :] > 0.0).astype(F32)
    carry_ref[...] = carry
    cout_ref[...] = jnp.broadcast_to(carry, cout_ref.shape)


def _rank(code, carry_in, pad_start, tr):
    t = code.shape[0]
    rowl = pl.BlockSpec((tr, LANES), lambda i: (i, 0))
    small = pl.BlockSpec((8, LANES), lambda i: (0, 0))
    return pl.pallas_call(
        _rank_kernel,
        grid=(t // tr,),
        in_specs=[rowl, small, pl.BlockSpec((1, LANES), lambda i: (0, 0))],
        out_specs=[pl.BlockSpec((8, tr), lambda i: (0, i)), small],
        out_shape=[jax.ShapeDtypeStruct((8, t), jnp.int32),
                   jax.ShapeDtypeStruct((8, LANES), F32)],
        scratch_shapes=[pltpu.VMEM((1, LANES), F32), pltpu.VMEM((tr, tr), BF16)],
        compiler_params=_cparams("arbitrary"),
        name="moe_rank",
    )(code, carry_in, pad_start)


def _dest_window(tm):
    width = max(tm, LANES)
    return width, width // tm


def _scatter_kernel(dest_ref, tok_ref, rows_in_ref, rows_ref, sem, *, tm, per):
    del rows_in_ref
    base = (pl.program_id(0) % per) * tm

    def row_copy(t, kk):
        return pltpu.make_async_copy(tok_ref.at[pl.ds(t, 1), :],
                                     rows_ref.at[pl.ds(dest_ref[kk, base + t], 1), :], sem)

    def start(g, c):
        t0 = pl.multiple_of(g * SUBLANES, SUBLANES)
        for s in range(SUBLANES):
            for kk in range(TOP_K):
                row_copy(t0 + s, kk).start()
        return c

    def wait(g, c):
        t0 = pl.multiple_of(g * SUBLANES, SUBLANES)
        for s in range(SUBLANES):
            for kk in range(TOP_K):
                row_copy(t0 + s, kk).wait()
        return c

    lax.fori_loop(0, tm // SUBLANES, start, 0)
    lax.fori_loop(0, tm // SUBLANES, wait, 0)


def _scatter(dest, tok, rows_buf, tm):
    t, d = tok.shape
    width, per = _dest_window(tm)
    return pl.pallas_call(
        functools.partial(_scatter_kernel, tm=tm, per=per),
        grid=(t // tm,),
        in_specs=[pl.BlockSpec((8, width), lambda i: (0, i // per), memory_space=pltpu.SMEM),
                  pl.BlockSpec((tm, d), lambda i: (i, 0)),
                  pl.BlockSpec(memory_space=pl.ANY)],
        out_specs=pl.BlockSpec(memory_space=pl.ANY),
        out_shape=jax.ShapeDtypeStruct(rows_buf.shape, rows_buf.dtype),
        scratch_shapes=[pltpu.SemaphoreType.DMA(())],
        input_output_aliases={2: 0},
        compiler_params=_cparams("arbitrary"),
        name="moe_scatter",
    )(dest, tok, rows_buf)


def _expert_kernel(be_ref, nu_ref, nv_ref, x_ref, wgu_ref, bgu_ref, wd_ref, bd_ref, y_ref, wgu_b, wd_b):
    dff = wd_ref.shape[1]
    i = pl.program_id(0)

    @pl.when(i < nu_ref[0])
    def _():
        @pl.when(jnp.logical_or(i == 0, be_ref[i] != be_ref[jnp.maximum(i - 1, 0)]))
        def _():
            wgu_b[...] = wgu_ref[0].astype(BF16)
            wd_b[...] = wd_ref[0].astype(BF16)

        row = lax.broadcasted_iota(jnp.int32, (x_ref.shape[0], 1), 0)
        xb = _unpack_bf16_pairs(jnp.where(row < nv_ref[i], x_ref[...], 0)).astype(BF16)
        gu = jnp.dot(xb, wgu_b[...], preferred_element_type=F32) + bgu_ref[0]
        gate = jnp.minimum(gu[:, :dff], SWIGLU_LIMIT)
        up = jnp.clip(gu[:, dff:], -SWIGLU_LIMIT, SWIGLU_LIMIT)
        act = (up + 1.0) * gate * jax.nn.sigmoid(SWIGLU_ALPHA * gate)
        y_ref[...] = _pack_bf16_pairs(
            jnp.dot(act.astype(BF16), wd_b[...], preferred_element_type=F32) + bd_ref[0])


def _experts(block_e, n_used, n_valid, x_rows, wgu, bgu, wd, bd, layer):
    rows, dp = x_rows.shape
    depth, ne, d, n_gu = wgu.shape
    dff = wd.shape[2]
    nb = rows // MOE_BLOCK

    def expert(i, be, nu):
        return be[jnp.minimum(i, nu[0] - 1)]

    rowspec = pl.BlockSpec((MOE_BLOCK, dp), lambda i, be, nu, nv: (jnp.minimum(i, nu[0] - 1), 0))
    return pl.pallas_call(
        _expert_kernel,
        grid_spec=pltpu.PrefetchScalarGridSpec(
            num_scalar_prefetch=3,
            grid=(nb,),
            in_specs=[rowspec,
                      pl.BlockSpec((None, 1, d, n_gu), lambda i, be, nu, nv: (layer, expert(i, be, nu), 0, 0)),
                      pl.BlockSpec((None, 1, 1, n_gu), lambda i, be, nu, nv: (layer, expert(i, be, nu), 0, 0)),
                      pl.BlockSpec((None, 1, dff, d), lambda i, be, nu, nv: (layer, expert(i, be, nu), 0, 0)),
                      pl.BlockSpec((None, 1, 1, d), lambda i, be, nu, nv: (layer, expert(i, be, nu), 0, 0))],
            out_specs=rowspec,
            scratch_shapes=[pltpu.VMEM((d, n_gu), BF16), pltpu.VMEM((dff, d), BF16)]),
        out_shape=jax.ShapeDtypeStruct((rows, dp), jnp.int32),
        compiler_params=_cparams("arbitrary"),
        name="moe_experts",
    )(block_e, n_used, n_valid, x_rows, wgu, bgu.reshape(depth, ne, 1, n_gu), wd,
      bd.reshape(depth, ne, 1, d))


def _sc_gather(rows, idx):
    n = idx.shape[0]
    d = rows.shape[1]
    mesh = plsc.VectorSubcoreMesh(core_axis_name="c", subcore_axis_name="s",
                                  num_cores=SC_CORES, num_subcores=SC_SUBCORES)
    n_workers = mesh.num_cores * mesh.num_subcores
    per_worker = n // n_workers
    assert per_worker * n_workers == n and per_worker % LANES == 0

    @functools.partial(pl.kernel, out_type=jax.ShapeDtypeStruct((n, d), rows.dtype), mesh=mesh,
                       scratch_types=[pltpu.VMEM((LANES,), jnp.int32),
                                      pltpu.VMEM((SC_WINDOW, d), rows.dtype),
                                      pltpu.VMEM((SC_WINDOW, d), rows.dtype),
                                      pltpu.SemaphoreType.DMA((2,)),
                                      pltpu.SemaphoreType.DMA((2,))],
                       compiler_params=pltpu.CompilerParams(use_tc_tiling_on_sc=True),
                       name="moe_gather")
    def gather(rows_hbm, idx_hbm, out_hbm, idx_vmem, buf0, buf1, gsem, wsem):
        worker = lax.axis_index("c") * mesh.num_subcores + lax.axis_index("s")
        bufs = (buf0, buf1)
        n_win = LANES // SC_WINDOW

        @pl.loop(0, per_worker // LANES)
        def _(j):
            chunk = worker * (per_worker // LANES) + j
            pltpu.sync_copy(idx_hbm.at[chunk], idx_vmem)

            def write(q):
                return pltpu.async_copy(
                    bufs[q % 2], out_hbm.at[pl.ds(chunk * LANES + q * SC_WINDOW, SC_WINDOW)],
                    wsem.at[q % 2])

            gathers, writes = [], []
            for q in range(n_win):
                if q >= 2:
                    writes[q - 2].wait()
                gathers.append(pltpu.async_copy(
                    rows_hbm.at[idx_vmem.at[pl.ds(q * SC_WINDOW, SC_WINDOW)]], bufs[q % 2],
                    gsem.at[q % 2]))
                if q >= 1:
                    gathers[q - 1].wait()
                    writes.append(write(q - 1))
            gathers[n_win - 1].wait()
            writes.append(write(n_win - 1))
            writes[n_win - 2].wait()
            writes[n_win - 1].wait()

    return gather(rows, idx.reshape(n // LANES, LANES))


def _sc_scatter(tok, dest, n_rows):
    t, d = tok.shape
    mesh = plsc.VectorSubcoreMesh(core_axis_name="c", subcore_axis_name="s",
                                  num_cores=SC_CORES, num_subcores=SC_SUBCORES)
    n_workers = mesh.num_cores * mesh.num_subcores
    per_worker = t // n_workers
    assert per_worker * n_workers == t and per_worker % LANES == 0
    chunks = t // LANES
    n_win = LANES // SC_WINDOW
    lanes16 = 16

    @functools.partial(pl.kernel, out_type=jax.ShapeDtypeStruct((n_rows, d), tok.dtype), mesh=mesh,
                       scratch_types=[pltpu.VMEM((TOP_K, LANES), jnp.int32),
                                      pltpu.VMEM((SC_WINDOW, d), tok.dtype),
                                      pltpu.VMEM((SC_WINDOW, d), tok.dtype),
                                      pltpu.SemaphoreType.DMA((2,)),
                                      pltpu.SemaphoreType.DMA((2,))],
                       compiler_params=pltpu.CompilerParams(use_tc_tiling_on_sc=True),
                       name="moe_scatter_sc")
    def scatter(tok_hbm, idx_hbm, rows_hbm, idx_vmem, buf0, buf1, lsem, ssem):
        worker = lax.axis_index("c") * mesh.num_subcores + lax.axis_index("s")
        bufs = (buf0, buf1)

        @pl.loop(0, per_worker // LANES)
        def _(j):
            chunk = worker * (per_worker // LANES) + j
            for kk in range(TOP_K):
                pltpu.sync_copy(idx_hbm.at[kk * chunks + chunk], idx_vmem.at[kk])

            def load(q):
                return pltpu.async_copy(
                    tok_hbm.at[pl.ds(chunk * LANES + q * SC_WINDOW, SC_WINDOW)], bufs[q % 2],
                    lsem.at[q % 2])

            def store_all(q):
                copies = []
                for kk in range(TOP_K):
                    for h in range(SC_WINDOW // lanes16):
                        rows16 = idx_vmem[kk, pl.ds(q * SC_WINDOW + h * lanes16, lanes16)]
                        copies.append(pltpu.async_copy(
                            bufs[q % 2].at[pl.ds(h * lanes16, lanes16)], rows_hbm.at[rows16],
                            ssem.at[q % 2]))
                return copies

            loads = [load(0)]
            stores = []
            for q in range(n_win):
                if q + 1 < n_win:
                    if q >= 1:
                        for cp in stores[q - 1]:
                            cp.wait()
                    loads.append(load(q + 1))
                loads[q].wait()
                stores.append(store_all(q))
            for q in (n_win - 2, n_win - 1):
                for cp in stores[q]:
                    cp.wait()

    return scatter(tok, dest[:TOP_K].reshape(TOP_K * chunks, LANES))


def _combine_dense_kernel(y0_ref, y1_ref, y2_ref, y3_ref, gate_ref, x_ref, mod_ref, g_ref, b_ref,
                          o_ref, *, alpha):
    gates = gate_ref[...]
    f = _unpack_bf16_pairs(y0_ref[...]) * gates[:, 0:1]
    for kk, y_ref in enumerate((y1_ref, y2_ref, y3_ref), start=1):
        f = f + _unpack_bf16_pairs(y_ref[...]) * gates[:, kk:kk + 1]
    gate2 = mod_ref[0, 5:6, :]
    o_ref[...] = _layer_norm(alpha * x_ref[...] + gate2 * f, g_ref[...], b_ref[...])


def _combine_dense(y_slots, gates, x1, mod, g, b, seq0, seq_len, tm, alpha):
    t, d = x1.shape
    nt = t // tm
    tiles_per_seq = seq_len // tm
    rowd = pl.BlockSpec((tm, d), lambda i: (i, 0))
    vec = pl.BlockSpec((1, d), lambda i: (0, 0))
    slot_specs = [pl.BlockSpec((tm, y_slots.shape[1]),
                               functools.partial(lambda i, kk: (kk * nt + i, 0), kk=kk))
                  for kk in range(TOP_K)]
    return pl.pallas_call(
        functools.partial(_combine_dense_kernel, alpha=alpha),
        grid=(nt,),
        in_specs=slot_specs + [pl.BlockSpec((tm, LANES), lambda i: (i, 0)), rowd,
                               pl.BlockSpec((1, 6, d), lambda i: (seq0 + i // tiles_per_seq, 0, 0)),
                               vec, vec],
        out_specs=rowd,
        out_shape=jax.ShapeDtypeStruct((t, d), F32),
        compiler_params=_cparams("parallel"),
        name="moe_combine",
    )(y_slots, y_slots, y_slots, y_slots, gates, x1, mod, g.reshape(1, d), b.reshape(1, d))


def _combine_kernel(dest_ref, y_hbm, gate_ref, x_ref, mod_ref, g_ref, b_ref, o_ref, ybuf, sem,
                    *, alpha, per):
    tm = x_ref.shape[0]
    base = (pl.program_id(0) % per) * tm

    def row_copy(t, kk):
        return pltpu.make_async_copy(y_hbm.at[pl.ds(dest_ref[kk, base + t], 1), :],
                                     ybuf.at[kk, pl.ds(t, 1), :], sem)

    def start(g, c):
        t0 = pl.multiple_of(g * SUBLANES, SUBLANES)
        for s in range(SUBLANES):
            for kk in range(TOP_K):
                row_copy(t0 + s, kk).start()
        return c

    def wait(g, c):
        t0 = pl.multiple_of(g * SUBLANES, SUBLANES)
        for s in range(SUBLANES):
            for kk in range(TOP_K):
                row_copy(t0 + s, kk).wait()
        return c

    lax.fori_loop(0, tm // SUBLANES, start, 0)
    lax.fori_loop(0, tm // SUBLANES, wait, 0)

    gates = gate_ref[...]
    f = _unpack_bf16_pairs(ybuf[0]) * gates[:, 0:1]
    for kk in range(1, TOP_K):
        f = f + _unpack_bf16_pairs(ybuf[kk]) * gates[:, kk:kk + 1]
    gate2 = mod_ref[0, 5:6, :]
    o_ref[...] = _layer_norm(alpha * x_ref[...] + gate2 * f, g_ref[...], b_ref[...])


def _combine(dest, y_rows, gates, x1, mod, g, b, seq0, seq_len, tm, alpha):
    t, d = x1.shape
    tiles_per_seq = seq_len // tm
    width, per = _dest_window(tm)
    rowd = pl.BlockSpec((tm, d), lambda i: (i, 0))
    vec = pl.BlockSpec((1, d), lambda i: (0, 0))
    return pl.pallas_call(
        functools.partial(_combine_kernel, alpha=alpha, per=per),
        grid=(t // tm,),
        in_specs=[pl.BlockSpec((8, width), lambda i: (0, i // per), memory_space=pltpu.SMEM),
                  pl.BlockSpec(memory_space=pl.ANY),
                  pl.BlockSpec((tm, LANES), lambda i: (i, 0)),
                  rowd,
                  pl.BlockSpec((1, 6, d), lambda i: (seq0 + i // tiles_per_seq, 0, 0)),
                  vec, vec],
        out_specs=rowd,
        out_shape=jax.ShapeDtypeStruct((t, d), F32),
        scratch_shapes=[pltpu.VMEM((TOP_K, tm, y_rows.shape[1]), y_rows.dtype),
                        pltpu.SemaphoreType.DMA(())],
        compiler_params=_cparams("arbitrary"),
        name="moe_combine",
    )(dest, y_rows, gates, x1, mod, g.reshape(1, d), b.reshape(1, d))


def _block_diag(wh):
    heads, hd, _ = wh.shape
    eye = jnp.eye(heads, dtype=wh.dtype)
    return jnp.einsum('hij,hg->higj', wh, eye).reshape(heads * hd, heads * hd)


def kernel(x_prompt, x_sample, cache_k, cache_v, state_conv, state_lru, c_prompt, c_sample, ln_in_g, ln_in_b, w_ada, b_ada, w_in, conv_w, conv_b, gate_a_w, gate_a_b, gate_x_w, gate_x_b, lru_lambda, w_out, ln1_g, ln1_b, router_w, router_b, w_gu, b_gu, w_down, b_down, ln2_g, ln2_b):
    nb_p, s_p, d = x_prompt.shape
    nb_s, s_s, _ = x_sample.shape
    depth = w_in.shape[0]
    past = cache_k.shape[2]
    w = state_lru.shape[-1]
    hw = SB_HEADS * SB_HEAD_DIM
    t_p, t_s = nb_p * s_p, nb_s * s_s
    alpha = float((2 * depth) ** 0.25)

    tm_p, tm_s = min(512, s_p), min(512, s_s)
    ts_p, ts_s = min(256, s_p), min(256, s_s)
    tq_p, tq_s = min(SB_TQ, s_p), min(SB_TQ, s_s)
    tg_p, tg_s = min(256, s_p), min(256, s_s)
    tr_p, tr_s = min(512, t_p), min(512, t_s)

    n_seq = nb_p + nb_s
    c_all = jnp.concatenate([c_prompt, c_sample, jnp.zeros((-n_seq % 8, d), F32)], axis=0)
    mod_all = _adaln(c_all, w_ada, b_ada).reshape(depth, c_all.shape[0], 6, d)

    xp = _ln_in(x_prompt.reshape(t_p, d), ln_in_g, ln_in_b, tm_p)
    xs = _ln_in(x_sample.reshape(t_s, d), ln_in_g, ln_in_b, tm_s)

    upper = jnp.asarray(np.arange(SB_SUB)[:, None] > np.arange(SB_SUB)[None, :], BF16)
    kpad_s = -(-(past + s_s) // SB_TK) * SB_TK
    zero_conv = jnp.zeros((nb_p, CONV_WIDTH - 1, w), F32)
    zero_h = jnp.zeros((nb_p, w), F32)

    tk_total = (t_p + t_s) * TOP_K
    n_blocks = -(-(tk_total + N_EXPERTS * (MOE_BLOCK - 1)) // MOE_BLOCK)
    n_rows = n_blocks * MOE_BLOCK

    outs = [[] for _ in range(8)]
    kv_p = kv_s = None
    for l in range(depth):
        mod = mod_all[l]
        w_in_b = w_in[l].astype(BF16)
        wo_b = w_out[l].astype(BF16)
        wa = _block_diag(gate_a_w[l]).astype(BF16)
        wx = _block_diag(gate_x_w[l]).astype(BF16)
        rw_pad = jnp.pad(router_w[l], ((0, 0), (0, LANES - N_EXPERTS)))
        rw_hi = rw_pad.astype(BF16)
        rw_lo = (rw_pad - rw_hi.astype(F32)).astype(BF16)
        rb_pad = jnp.pad(router_b[l], (0, LANES - N_EXPERTS), constant_values=NEG_BIG).reshape(1, LANES)
        lru_w = (conv_w[l], conv_b[l], wa, gate_a_b[l], wx, gate_x_b[l], lru_lambda[l])

        xr, gy, q, k, v, kb, vb = _inproj(xp, mod, w_in_b, 0, s_p, tm_p, l, depth, kv_p)
        kv_p = (k, v)
        lru_p, nconv_p, nh_p = _rglru(xr, gy, zero_conv, zero_h, *lru_w, nb_p, s_p, ts_p)
        att_p = _stick_breaking(q, kb.reshape(nb_p, s_p, hw), vb.reshape(nb_p, s_p, hw), upper,
                                nb_p, s_p, tq_p, 0)
        outs[2].append(nconv_p)
        outs[3].append(nh_p.reshape(nb_p, w))

        xr, gy, q, k, v, kb, vb = _inproj(xs, mod, w_in_b, nb_p, s_s, tm_s, l, depth, kv_s)
        kv_s = (k, v)
        lru_s, nconv_s, nh_s = _rglru(xr, gy, state_conv[l], state_lru[l], *lru_w, nb_s, s_s, ts_s)
        kv_pad = jnp.zeros((nb_s, kpad_s - past - s_s, hw), BF16)
        k_all = jnp.concatenate([cache_k[l].reshape(nb_s, past, hw).astype(BF16),
                                 kb.reshape(nb_s, s_s, hw), kv_pad], axis=1)
        v_all = jnp.concatenate([cache_v[l].reshape(nb_s, past, hw).astype(BF16),
                                 vb.reshape(nb_s, s_s, hw), kv_pad], axis=1)
        att_s = _stick_breaking(q, k_all, v_all, upper, nb_s, s_s, tq_s, past)
        outs[6].append(nconv_s)
        outs[7].append(nh_s.reshape(nb_s, w))

        x1_p, tok_p, code_p, gate_p, cnt_p = _outproj(lru_p, att_p, xp, mod, wo_b, ln1_g[l], ln1_b[l],
                                                      rw_hi, rw_lo, rb_pad,0, s_p, tm_p, alpha)
        x1_s, tok_s, code_s, gate_s, cnt_s = _outproj(lru_s, att_s, xs, mod, wo_b, ln1_g[l], ln1_b[l],
                                                      rw_hi, rw_lo, rb_pad,nb_p, s_s, tm_s, alpha)

        counts = (cnt_p[0, :N_EXPERTS] + cnt_s[0, :N_EXPERTS]).astype(jnp.int32)
        padded = (counts + MOE_BLOCK - 1) // MOE_BLOCK * MOE_BLOCK
        pad_end = jnp.cumsum(padded)
        pad_start = jnp.pad((pad_end - padded).astype(F32), (0, LANES - N_EXPERTS)).reshape(1, LANES)
        block_row0 = jnp.arange(n_blocks, dtype=jnp.int32) * MOE_BLOCK
        block_e = jnp.minimum(jnp.sum(pad_end[None, :] <= block_row0[:, None], axis=1),
                              N_EXPERTS - 1).astype(jnp.int32)

        n_used = (pad_end[N_EXPERTS - 1:] // MOE_BLOCK).astype(jnp.int32)
        n_valid = jnp.clip(counts[block_e] - (block_row0 - (pad_end - padded)[block_e]), 0, MOE_BLOCK)

        dest_p, carry = _rank(code_p, jnp.zeros((8, LANES), F32), pad_start, tr_p)
        dest_s, _ = _rank(code_s, carry, pad_start, tr_s)

        x_rows = _sc_scatter(tok_p, dest_p, n_rows)
        x_rows = _scatter(dest_s, tok_s, x_rows, tg_s)
        y_rows = _experts(block_e, n_used, n_valid, x_rows, w_gu, b_gu, w_down, b_down, l)

        yg_p = _sc_gather(y_rows, dest_p[:TOP_K].reshape(TOP_K * t_p))
        xp = _combine_dense(yg_p, gate_p, x1_p, mod, ln2_g[l], ln2_b[l], 0, s_p, tg_p, alpha)
        xs = _combine(dest_s, y_rows, gate_s, x1_s, mod, ln2_g[l], ln2_b[l], nb_p, s_s, tg_s, alpha)

    return (xp.reshape(nb_p, s_p, d), xs.reshape(nb_s, s_s, d),
            kv_p[0].reshape(depth, nb_p, s_p, SB_HEADS, SB_HEAD_DIM),
            kv_p[1].reshape(depth, nb_p, s_p, SB_HEADS, SB_HEAD_DIM),
            jnp.stack(outs[2]), jnp.stack(outs[3]),
            kv_s[0].reshape(depth, nb_s, s_s, SB_HEADS, SB_HEAD_DIM),
            kv_s[1].reshape(depth, nb_s, s_s, SB_HEADS, SB_HEAD_DIM),
            jnp.stack(outs[6]), jnp.stack(outs[7]))
```
